```python
import math
import numpy as np
import jax
import jax.numpy as jnp
from jax import lax

D_MODEL = 1024
BATCH = 8
SEQ = 4096
DEPTH = 2

HEAD_DIM = 64
ROPE_THETA = 10000.0
EPS = 1e-6
F32 = jnp.float32

A_HEADS = 4
MOBA_BLOCK = 256
MOBA_TOPK = 3
MOBA_QCHUNK = 32

B_HEADS = 4
IDX_HEADS = 8
IDX_DIM = 64
DSA_TOPK_MAX = 256
DSA_QCHUNK = 64

C_HEADS = 4
C_VDIM = 2 * HEAD_DIM
DIFF_QCHUNK = 128

N_GROUPS = 4
EXPERTS_PER_GROUP = 8
N_EXPERTS = N_GROUPS * EXPERTS_PER_GROUP
EXPERT_TOPK = 2
D_EXPERT = 512
MOE_BLOCK = 256

A_W = A_HEADS * HEAD_DIM
B_W = B_HEADS * HEAD_DIM
C_QK_W = C_HEADS * HEAD_DIM
C_W = C_HEADS * C_VDIM

IN_SIZES = (A_W, A_W, A_W, B_W, B_W, B_W, IDX_HEADS * IDX_DIM, IDX_DIM, IDX_HEADS,
            C_QK_W, C_QK_W, C_QK_W, C_QK_W, C_W, D_MODEL, D_MODEL, D_MODEL)
D_IN = 3 * A_W + 3 * B_W + IDX_HEADS * IDX_DIM + IDX_DIM + IDX_HEADS + 4 * C_QK_W + C_W + 3 * D_MODEL

kernel_name = 'hybrid_moba_dsa_diffattn_hiermoe'


def rms_norm(x, gain):
    xf = x.astype(F32)
    y = xf * lax.rsqrt(jnp.mean(xf * xf, axis=-1, keepdims=True) + EPS)
    return (y * gain.astype(F32)).astype(x.dtype)


def rope_tables(seq_len):
    inv_freq = 1.0 / (ROPE_THETA ** (jnp.arange(0, HEAD_DIM, 2, dtype=F32) / HEAD_DIM))
    ang = jnp.arange(seq_len, dtype=F32)[:, None] * inv_freq[None, :]
    return jnp.cos(ang), jnp.sin(ang)


def apply_rope(x, cos, sin):
    half = x.shape[-1] // 2
    xf = x.astype(F32)
    x1, x2 = xf[..., :half], xf[..., half:]
    c = cos[:, None, :]
    s = sin[:, None, :]
    return jnp.concatenate([x1 * c - x2 * s, x2 * c + x1 * s], axis=-1).astype(x.dtype)


def norm_rope(t, gain, cos, sin):
    return apply_rope(rms_norm(t, gain), cos, sin)


def to_chunks(t, size):
    b, s = t.shape[:2]
    return jnp.moveaxis(t.reshape((b, s // size, size) + t.shape[2:]), 1, 0)


def from_chunks(t):
    t = jnp.moveaxis(t, 0, 1)
    return t.reshape((t.shape[0], t.shape[1] * t.shape[2]) + t.shape[3:])


def split_points():
    return [int(v) for v in np.cumsum(np.array(IN_SIZES))[:-1]]


def moba_attention(q, k, v):
    b, s, h, dh = q.shape
    nb = -(-s // MOBA_BLOCK)
    pad = nb * MOBA_BLOCK - s

    def blocks(t):
        t = jnp.pad(t, ((0, 0), (0, pad), (0, 0), (0, 0)))
        return t.reshape(b, nb, MOBA_BLOCK, h, dh).transpose(0, 3, 1, 2, 4)

    kb, vb = blocks(k), blocks(v)
    k_mean = jnp.mean(kb.astype(F32), axis=3)
    n_sel = min(MOBA_TOPK, nb - 1)
    scale = dh ** -0.5
    b_idx = jnp.arange(b)[:, None, None, None]
    h_idx = jnp.arange(h)[None, :, None, None]
    blk_pos = jnp.arange(MOBA_BLOCK)
    n_sk = n_sel * MOBA_BLOCK

    def chunk_fn(args):
        ci, qc = args
        qh = qc.transpose(0, 2, 1, 3)
        start = ci * MOBA_QCHUNK
        pos = start + jnp.arange(MOBA_QCHUNK)
        own = start // MOBA_BLOCK
        k_own = lax.dynamic_index_in_dim(kb, own, axis=2, keepdims=False)
        v_own = lax.dynamic_index_in_dim(vb, own, axis=2, keepdims=False)
        s_own = jnp.einsum('bhcd,bhkd->bhck', qh, k_own).astype(F32) * scale
        s_own = jnp.where((own * MOBA_BLOCK + blk_pos)[None, :] <= pos[:, None], s_own, -jnp.inf)
        if n_sel == 0:
            p = jax.nn.softmax(s_own, axis=-1).astype(v.dtype)
            return jnp.einsum('bhck,bhkd->bchd', p, v_own)
        gate = jnp.einsum('bhcd,bhnd->bhcn', qh.astype(F32), k_mean)
        gate = jnp.where(jnp.arange(nb) < own, gate, -jnp.inf)
        _, sel = lax.top_k(gate, n_sel)
        k_sel = kb[b_idx, h_idx, sel]
        v_sel = vb[b_idx, h_idx, sel]
        s_sel = jnp.einsum('bhcd,bhcnkd->bhcnk', qh, k_sel).astype(F32) * scale
        s_sel = jnp.where((jnp.arange(n_sel) < own)[:, None], s_sel, -jnp.inf)
        s_all = jnp.concatenate([s_sel.reshape(b, h, MOBA_QCHUNK, n_sk), s_own], axis=-1)
        p = jax.nn.softmax(s_all, axis=-1).astype(v.dtype)
        p_sel = p[..., :n_sk].reshape(b, h, MOBA_QCHUNK, n_sel, MOBA_BLOCK)
        return (jnp.einsum('bhcnk,bhcnkd->bchd', p_sel, v_sel)
                + jnp.einsum('bhck,bhkd->bchd', p[..., n_sk:], v_own))

    n = s // MOBA_QCHUNK
    out = lax.map(chunk_fn, (jnp.arange(n), to_chunks(q, MOBA_QCHUNK)))
    return from_chunks(out)


def dsa_attention(q, k, v, q_idx, k_idx, w_idx):
    b, s, h, dh = q.shape
    n_keep = min(DSA_TOPK_MAX, s // 4)
    scale = dh ** -0.5
    key_pos = jnp.arange(s)
    b_idx = jnp.arange(b)[:, None, None]
    k_idx_f = k_idx.astype(F32)

    def chunk_fn(args):
        ci, qc, qic, wc = args
        pos = ci * DSA_QCHUNK + jnp.arange(DSA_QCHUNK)
        logits = jnp.einsum('bchd,bsd->bchs', qic.astype(F32), k_idx_f) * (IDX_DIM ** -0.5)
        score = jnp.einsum('bch,bchs->bcs', wc.astype(F32), jax.nn.relu(logits))
        score = jnp.where(key_pos[None, None, :] <= pos[None, :, None], score, -jnp.inf)
        _, sel = lax.top_k(score, n_keep)
        k_sel = k[b_idx, sel]
        v_sel = v[b_idx, sel]
        sc = jnp.einsum('bchd,bckhd->bhck', qc, k_sel).astype(F32) * scale
        valid = (sel <= pos[None, :, None])[:, None]
        p = jax.nn.softmax(jnp.where(valid, sc, -jnp.inf), axis=-1).astype(v.dtype)
        return jnp.einsum('bhck,bckhd->bchd', p, v_sel)

    n = s // DSA_QCHUNK
    out = lax.map(chunk_fn, (jnp.arange(n), to_chunks(q, DSA_QCHUNK),
                             to_chunks(q_idx, DSA_QCHUNK), to_chunks(w_idx, DSA_QCHUNK)))
    return from_chunks(out)


def diff_attention(q1, q2, k1, k2, v, lam):
    b, s, h, dh = q1.shape
    scale = dh ** -0.5
    key_pos = jnp.arange(s)

    def chunk_fn(args):
        ci, q1c, q2c = args
        pos = ci * DIFF_QCHUNK + jnp.arange(DIFF_QCHUNK)
        causal = key_pos[None, :] <= pos[:, None]
        s1 = jnp.einsum('bchd,bshd->bhcs', q1c, k1).astype(F32) * scale
        s2 = jnp.einsum('bchd,bshd->bhcs', q2c, k2).astype(F32) * scale
        a1 = jax.nn.softmax(jnp.where(causal, s1, -jnp.inf), axis=-1)
        a2 = jax.nn.softmax(jnp.where(causal, s2, -jnp.inf), axis=-1)
        a = (a1 - lam * a2).astype(v.dtype)
        return jnp.einsum('bhcs,bshe->bche', a, v)

    n = s // DIFF_QCHUNK
    out = lax.map(chunk_fn, (jnp.arange(n), to_chunks(q1, DIFF_QCHUNK), to_chunks(q2, DIFF_QCHUNK)))
    return from_chunks(out)


def mixer_sublayer(x, cos, sin, layer_idx, norm_g, w_in, qk_gain, idx_k_gain, diff_lambda,
                   diff_subln_gain, w_proj_a, w_proj_b, w_proj_c, w_out):
    b, s, _ = x.shape
    hn = rms_norm(x, norm_g)
    proj = hn @ w_in
    (qa, ka, va, qb, kb, vb, qi, ki, wi, q1, q2, k1, k2, vc, ga, gb, gc) = jnp.split(proj, split_points(), axis=-1)

    qa = norm_rope(qa.reshape(b, s, A_HEADS, HEAD_DIM), qk_gain[0], cos, sin)
    ka = norm_rope(ka.reshape(b, s, A_HEADS, HEAD_DIM), qk_gain[1], cos, sin)
    o_a = moba_attention(qa, ka, va.reshape(b, s, A_HEADS, HEAD_DIM)).reshape(b, s, A_W)

    qb = norm_rope(qb.reshape(b, s, B_HEADS, HEAD_DIM), qk_gain[2], cos, sin)
    kb = norm_rope(kb.reshape(b, s, B_HEADS, HEAD_DIM), qk_gain[3], cos, sin)
    qi = apply_rope(qi.reshape(b, s, IDX_HEADS, IDX_DIM), cos, sin)
    ki = apply_rope(rms_norm(ki, idx_k_gain)[:, :, None, :], cos, sin)[:, :, 0, :]
    wi = wi * (IDX_HEADS ** -0.5)
    o_b = dsa_attention(qb, kb, vb.reshape(b, s, B_HEADS, HEAD_DIM), qi, ki, wi).reshape(b, s, B_W)

    q1 = norm_rope(q1.reshape(b, s, C_HEADS, HEAD_DIM), qk_gain[4], cos, sin)
    q2 = norm_rope(q2.reshape(b, s, C_HEADS, HEAD_DIM), qk_gain[4], cos, sin)
    k1 = norm_rope(k1.reshape(b, s, C_HEADS, HEAD_DIM), qk_gain[5], cos, sin)
    k2 = norm_rope(k2.reshape(b, s, C_HEADS, HEAD_DIM), qk_gain[5], cos, sin)
    lam_init = 0.8 - 0.6 * math.exp(-0.3 * layer_idx)
    dl = diff_lambda.astype(F32)
    lam = jnp.exp(jnp.sum(dl[0] * dl[1])) - jnp.exp(jnp.sum(dl[2] * dl[3])) + lam_init
    o_c = diff_attention(q1, q2, k1, k2, vc.reshape(b, s, C_HEADS, C_VDIM), lam)
    o_c = (rms_norm(o_c, diff_subln_gain) * (1.0 - lam_init)).reshape(b, s, C_W)

    merged = (jax.nn.sigmoid(ga) * (o_a @ w_proj_a)
              + jax.nn.sigmoid(gb) * (o_b @ w_proj_b)
              + jax.nn.sigmoid(gc) * (o_c @ w_proj_c))
    return x + merged @ w_out


def hier_moe(xn, w_group, b_group, w_router, b_router, w_e_gate, w_e_up, w_e_down):
    n, d = xn.shape
    xf = xn.astype(F32)
    g_logits = xf @ w_group.astype(F32) + b_group.astype(F32)
    g_prob = jax.nn.softmax(g_logits, axis=-1)
    _, g_idx = lax.top_k(g_logits, 1)
    g_w = jnp.take_along_axis(g_prob, g_idx, axis=1)[:, 0]
    e_logits = (xf @ w_router.astype(F32) + b_router.astype(F32)).reshape(n, N_GROUPS, EXPERTS_PER_GROUP)
    e_logits_g = jnp.take_along_axis(e_logits, g_idx[:, :, None], axis=1)[:, 0]
    e_top, e_idx = lax.top_k(e_logits_g, EXPERT_TOPK)
    weights = g_w[:, None] * jax.nn.softmax(e_top, axis=-1)
    expert = g_idx * EXPERTS_PER_GROUP + e_idx

    n_slots = n * EXPERT_TOPK
    flat_e = expert.reshape(-1)
    flat_tok = jnp.repeat(jnp.arange(n), EXPERT_TOPK)
    flat_w = weights.reshape(-1)
    order = jnp.argsort(flat_e)
    se, st, sw = flat_e[order], flat_tok[order], flat_w[order]
    counts = jax.ops.segment_sum(jnp.ones_like(flat_e), flat_e, num_segments=N_EXPERTS)
    padded = ((counts + MOE_BLOCK - 1) // MOE_BLOCK) * MOE_BLOCK
    start = jnp.cumsum(counts) - counts
    pend = jnp.cumsum(padded)
    pstart = pend - padded
    dest = pstart[se] + (jnp.arange(n_slots) - start[se])
    n_blocks = -(-(n_slots + N_EXPERTS * (MOE_BLOCK - 1)) // MOE_BLOCK)
    rows = n_blocks * MOE_BLOCK
    tok_pad = jnp.zeros((rows,), jnp.int32).at[dest].set(st)
    w_pad = jnp.zeros((rows,), F32).at[dest].set(sw)
    block_expert = jnp.clip(jnp.searchsorted(pend, jnp.arange(n_blocks) * MOE_BLOCK, side='right'),
                            0, N_EXPERTS - 1)
    xb = xn[tok_pad].reshape(n_blocks, MOE_BLOCK, d)

    def block_fn(args):
        xi, e = args
        hid = jax.nn.silu(xi @ w_e_gate[e]) * (xi @ w_e_up[e])
        return hid @ w_e_down[e]

    yb = lax.map(block_fn, (xb, block_expert)).reshape(rows, d)
    contrib = (yb.astype(F32) * w_pad[:, None]).astype(xn.dtype)
    return jnp.zeros((n, d), xn.dtype).at[tok_pad].add(contrib)


def setup_inputs(seed: int = 0) -> dict:
    key = jax.random.key(seed)
    ks = jax.random.split(key, 20)

    def nrm(k, shape, scale):
        return jax.random.normal(k, shape, F32) * scale

    return {
        'x': nrm(ks[0], (BATCH, SEQ, D_MODEL), 1.0),
        'norm_attn': 1.0 + nrm(ks[1], (DEPTH, D_MODEL), 0.02),
        'w_in': nrm(ks[2], (DEPTH, D_MODEL, D_IN), D_MODEL ** -0.5),
        'qk_gain': 1.0 + nrm(ks[3], (DEPTH, 6, HEAD_DIM), 0.02),
        'idx_k_gain': 1.0 + nrm(ks[4], (DEPTH, IDX_DIM), 0.02),
        'diff_lambda': nrm(ks[5], (DEPTH, 4, HEAD_DIM), 0.1),
        'diff_subln_gain': 1.0 + nrm(ks[6], (DEPTH, C_VDIM), 0.02),
        'w_proj_a': nrm(ks[7], (DEPTH, A_W, D_MODEL), A_W ** -0.5),
        'w_proj_b': nrm(ks[8], (DEPTH, B_W, D_MODEL), B_W ** -0.5),
        'w_proj_c': nrm(ks[9], (DEPTH, C_W, D_MODEL), C_W ** -0.5),
        'w_out': nrm(ks[10], (DEPTH, D_MODEL, D_MODEL), D_MODEL ** -0.5),
        'norm_ffn': 1.0 + nrm(ks[11], (DEPTH, D_MODEL), 0.02),
        'w_group': nrm(ks[12], (DEPTH, D_MODEL, N_GROUPS), D_MODEL ** -0.5),
        'b_group': nrm(ks[13], (DEPTH, N_GROUPS), 0.01),
        'w_router': nrm(ks[14], (DEPTH, D_MODEL, N_EXPERTS), D_MODEL ** -0.5),
        'b_router': nrm(ks[15], (DEPTH, N_EXPERTS), 0.01),
        'w_e_gate': nrm(ks[16], (DEPTH, N_EXPERTS, D_MODEL, D_EXPERT), D_MODEL ** -0.5),
        'w_e_up': nrm(ks[17], (DEPTH, N_EXPERTS, D_MODEL, D_EXPERT), D_MODEL ** -0.5),
        'w_e_down': nrm(ks[18], (DEPTH, N_EXPERTS, D_EXPERT, D_MODEL), D_EXPERT ** -0.5),
    }


def reference(x, norm_attn, w_in, qk_gain, idx_k_gain, diff_lambda, diff_subln_gain, w_proj_a,
              w_proj_b, w_proj_c, w_out, norm_ffn, w_group, b_group, w_router, b_router,
              w_e_gate, w_e_up, w_e_down):
    b, s, d = x.shape
    cos, sin = rope_tables(s)
    for l in range(DEPTH):
        x = mixer_sublayer(x, cos, sin, l, norm_attn[l], w_in[l], qk_gain[l], idx_k_gain[l],
                           diff_lambda[l], diff_subln_gain[l], w_proj_a[l], w_proj_b[l],
                           w_proj_c[l], w_out[l])
        hn = rms_norm(x, norm_ffn[l]).reshape(b * s, d)
        x = x + hier_moe(hn, w_group[l], b_group[l], w_router[l], b_router[l],
                         w_e_gate[l], w_e_up[l], w_e_down[l]).reshape(b, s, d)
    return x
```

```python
import functools
import math

import jax
import jax.numpy as jnp
from jax import lax
from jax.experimental import pallas as pl
from jax.experimental.pallas import tpu as pltpu

F32 = jnp.float32
BF16 = jnp.bfloat16

D_MODEL = 1024
HEAD_DIM = 64
ROPE_THETA = 10000.0
EPS = 1e-6
N_HEADS = 4
MOBA_BLOCK = 256
MOBA_TOPK = 3
IDX_HEADS = 8
IDX_DIM = 64
DSA_TOPK_MAX = 256
C_VDIM = 2 * HEAD_DIM
N_GROUPS = 4
EXPERTS_PER_GROUP = 8
N_EXPERTS = N_GROUPS * EXPERTS_PER_GROUP
D_EXPERT = 512

HW = N_HEADS * HEAD_DIM
LANES = 128
ROW_TILE = 256
MOE_TILE = 512
VMEM_LIMIT = 56 * 1024 * 1024

_SEG = {}
_off = 0
for _name, _w in (("qa", HW), ("ka", HW), ("va", HW), ("qb", HW), ("kb", HW), ("vb", HW),
                  ("qi", IDX_HEADS * IDX_DIM), ("kw", LANES), ("q1", HW), ("q2", HW), ("k1", HW),
                  ("k2", HW), ("vc", N_HEADS * C_VDIM), ("ga", D_MODEL), ("gb", D_MODEL),
                  ("gc", D_MODEL)):
    _SEG[_name] = (_off, _w)
    _off += _w
D_IN_PAD = _off
KW_SRC = 6 * HW + IDX_HEADS * IDX_DIM + IDX_DIM + IDX_HEADS

NEG_BIG = -1e30
INT_MIN = -(2 ** 31)
Q_SCALE = HEAD_DIM ** -0.5
NT_DIMS = (((1,), (1,)), ((), ()))


def _params(n_axes):
    return pltpu.CompilerParams(dimension_semantics=("arbitrary",) * n_axes,
                                vmem_limit_bytes=VMEM_LIMIT)


def _dot(a, b):
    return jnp.dot(a, b, preferred_element_type=F32)


def _dot_nt(a, b):
    return lax.dot_general(a, b, NT_DIMS, preferred_element_type=F32)


def _split_bf16(a):
    hi = a.astype(BF16)
    return hi, (a - hi.astype(F32)).astype(BF16)


def _swap_halves(y, width):
    lane = lax.broadcasted_iota(jnp.int32, y.shape, 1)
    first = (lane % HEAD_DIM) < (HEAD_DIM // 2)
    return jnp.where(first, pltpu.roll(y, width - HEAD_DIM // 2, 1), pltpu.roll(y, HEAD_DIM // 2, 1))


def _proj_kernel(x_ref, g_ref, w_ref, cos_ref, sin_ref, gain_ref, kgain_ref,
                 qa_ref, ka_ref, va_ref, qb_ref, kb_ref, vb_ref, qi_ref, ki_ref, wi_ref,
                 q1_ref, q2_ref, k1_ref, k2_ref, vc_ref, sg_ref, kmean_ref):
    x = x_ref[...]
    ms = jnp.mean(x * x, axis=-1, keepdims=True)
    h = (x * lax.rsqrt(ms + EPS) * g_ref[...]).astype(BF16)
    cos = cos_ref[...]
    sin = sin_ref[...]
    r = lax.broadcasted_iota(jnp.int32, (HW, HW), 0) // HEAD_DIM
    c = lax.broadcasted_iota(jnp.int32, (HW, HW), 1) // HEAD_DIM
    head_ones = (r == c).astype(BF16)

    def seg(name, lo=0, width=None):
        off, w = _SEG[name]
        width = w if width is None else width
        return _dot(h, w_ref[:, off + lo:off + lo + width])

    def rope(y):
        return y * cos + _swap_halves(y, HW) * sin

    def norm_rope(t, gain_row):
        hi, lo = _split_bf16(t * t)
        ss = _dot(hi, head_ones) + _dot(lo, head_ones)
        yn = t * lax.rsqrt(ss * (1.0 / HEAD_DIM) + EPS) * gain_ref[gain_row:gain_row + 1, :]
        return rope(yn)

    qa_ref[...] = (norm_rope(seg("qa"), 0) * Q_SCALE).astype(BF16)
    ka = norm_rope(seg("ka"), 1)
    ka_ref[...] = ka.astype(BF16)
    kmean_ref[0] = jnp.mean(ka, axis=0, keepdims=True)
    va_ref[...] = seg("va").astype(BF16)
    qb_ref[...] = (norm_rope(seg("qb"), 2) * Q_SCALE).astype(BF16)
    kb_ref[...] = norm_rope(seg("kb"), 3).astype(BF16)
    vb_ref[...] = seg("vb").astype(BF16)
    for half in range(2):
        qi_ref[:, half * HW:(half + 1) * HW] = rope(seg("qi", half * HW, HW)).astype(BF16)

    t = seg("kw")
    lane = lax.broadcasted_iota(jnp.int32, t.shape, 1)
    is_k = lane < IDX_DIM
    kms = jnp.sum(jnp.where(is_k, t * t, 0.0), axis=-1, keepdims=True) * (1.0 / IDX_DIM)
    kn = t * lax.rsqrt(kms + EPS) * kgain_ref[...]
    kr = kn * cos[:, :LANES] + _swap_halves(kn, LANES) * sin[:, :LANES]
    ki_ref[...] = kr[:, :IDX_DIM].astype(BF16)
    w_scale = (IDX_HEADS ** -0.5) * (IDX_DIM ** -0.5)
    wi_ref[...] = jnp.where(lane < IDX_HEADS, pltpu.roll(t, LANES - IDX_DIM, 1) * w_scale, 0.0)

    q1_ref[...] = (norm_rope(seg("q1"), 4) * Q_SCALE).astype(BF16)
    q2_ref[...] = (norm_rope(seg("q2"), 5) * Q_SCALE).astype(BF16)
    k1_ref[...] = norm_rope(seg("k1"), 6).astype(BF16)
    k2_ref[...] = norm_rope(seg("k2"), 7).astype(BF16)
    for half in range(2):
        vc_ref[:, half * HW:(half + 1) * HW] = seg("vc", half * HW, HW).astype(BF16)
    for gi, name in enumerate(("ga", "gb", "gc")):
        for part in range(D_MODEL // 512):
            g = seg(name, part * 512, 512)
            lo = gi * D_MODEL + part * 512
            sg_ref[:, lo:lo + 512] = (1.0 / (1.0 + jnp.exp(-g))).astype(BF16)


def _project(x2, norm_g, w_pad, cos_t, sin_t, gains, kgain, seq):
    n = x2.shape[0]
    tm = ROW_TILE
    n_pos = seq // tm
    row = lambda w: pl.BlockSpec((tm, w), lambda i: (i, 0))
    const = lambda shape: pl.BlockSpec(shape, lambda i: (0,) * len(shape))
    out_widths = [HW] * 6 + [IDX_HEADS * IDX_DIM, IDX_DIM, LANES] + [HW] * 4 + [N_HEADS * C_VDIM, 3 * D_MODEL]
    out_dtypes = [BF16] * 8 + [F32] + [BF16] * 6
    out_shape = [jax.ShapeDtypeStruct((n, w), dt) for w, dt in zip(out_widths, out_dtypes)]
    out_shape.append(jax.ShapeDtypeStruct((n // tm, 1, HW), F32))
    out_specs = [row(w) for w in out_widths] + [pl.BlockSpec((1, 1, HW), lambda i: (i, 0, 0))]
    return pl.pallas_call(
        _proj_kernel,
        grid=(n // tm,),
        in_specs=[row(D_MODEL), const((1, D_MODEL)),
                  pl.BlockSpec((D_MODEL, D_IN_PAD), lambda i: (0, 0), pipeline_mode=pl.Buffered(1)),
                  pl.BlockSpec((tm, HW), lambda i: (i % n_pos, 0)),
                  pl.BlockSpec((tm, HW), lambda i: (i % n_pos, 0)),
                  const((8, HW)), const((1, LANES))],
        out_specs=out_specs,
        out_shape=out_shape,
        compiler_params=_params(1),
        name="proj",
    )(x2, norm_g, w_pad, cos_t, sin_t, gains, kgain)


def _moba_kernel(q_ref, k_ref, v_ref, kmean_ref, o_ref, *, n_sel):
    i = pl.program_id(1)
    blk = MOBA_BLOCK
    nb = kmean_ref.shape[1]
    km = kmean_ref[0]
    rowi = lax.broadcasted_iota(jnp.int32, (blk, 1), 0)
    coli = lax.broadcasted_iota(jnp.int32, (1, blk), 1)
    bcol = lax.broadcasted_iota(jnp.int32, (blk, nb), 1)
    bcolf = bcol.astype(F32)
    for h in range(N_HEADS):
        hs = slice(h * HEAD_DIM, (h + 1) * HEAD_DIM)
        qh = q_ref[:, hs]
        km_hi, km_lo = _split_bf16(km[:, hs])
        gate = _dot_nt(qh, km_hi) + _dot_nt(qh, km_lo)
        gate = jnp.where(bcol < i, gate, -jnp.inf)
        sel = jnp.zeros((blk, nb), F32)
        for _ in range(n_sel):
            gm = jnp.max(gate, axis=-1, keepdims=True)
            is_m = (gate == gm) & (gm > -jnp.inf)
            first = jnp.min(jnp.where(is_m, bcolf, float(nb)), axis=-1, keepdims=True)
            pick = bcolf == first
            sel = jnp.where(pick, 1.0, sel)
            gate = jnp.where(pick, -jnp.inf, gate)

        def step(j, carry, diag):
            m, l, acc = carry
            kj = k_ref[pl.ds(pl.multiple_of(j * blk, blk), blk), hs]
            s = _dot_nt(qh, kj)
            if diag:
                allowed = coli <= rowi
            else:
                allowed = jnp.sum(jnp.where(bcol == j, sel, 0.0), axis=-1, keepdims=True) > 0.0
            s = jnp.where(allowed, s, NEG_BIG)
            m_new = jnp.maximum(m, jnp.max(s, axis=-1, keepdims=True))
            p = jnp.where(allowed, jnp.exp(s - m_new), 0.0)
            alpha = jnp.exp(m - m_new)
            l = alpha * l + jnp.sum(p, axis=-1, keepdims=True)
            vj = v_ref[pl.ds(pl.multiple_of(j * blk, blk), blk), hs]
            acc = alpha * acc + _dot(p.astype(BF16), vj)
            return m_new, l, acc

        init = (jnp.full((blk, 1), NEG_BIG, F32), jnp.zeros((blk, 1), F32),
                jnp.zeros((blk, HEAD_DIM), F32))
        carry = step(i, init, True)
        _, l, acc = lax.fori_loop(0, i, lambda j, cr: step(j, cr, False), carry)
        o_ref[:, hs] = (acc / l).astype(o_ref.dtype)


def _moba(q, k, v, kmean, batch, seq):
    nb = seq // MOBA_BLOCK
    n_sel = min(MOBA_TOPK, nb - 1)
    qspec = pl.BlockSpec((MOBA_BLOCK, HW), lambda b, i: (b * nb + i, 0))
    kvspec = pl.BlockSpec((seq, HW), lambda b, i: (b, 0))
    return pl.pallas_call(
        functools.partial(_moba_kernel, n_sel=n_sel),
        grid=(batch, nb),
        in_specs=[qspec, kvspec, kvspec, pl.BlockSpec((1, nb, HW), lambda b, i: (b, 0, 0))],
        out_specs=qspec,
        out_shape=jax.ShapeDtypeStruct(q.shape, BF16),
        compiler_params=_params(2),
        name="moba",
    )(q, k, v, kmean)


def _dsa_kernel(qi_ref, wi_ref, ki_ref, q_ref, k_ref, vt_ref, o_ref, key_ref, *, n_keep):
    i = pl.program_id(1)
    blk = ROW_TILE
    n_chunk = i + 1
    w_t = wi_ref[...].T
    qpos = i * blk + lax.broadcasted_iota(jnp.int32, (1, blk), 1)
    krow = lax.broadcasted_iota(jnp.int32, (blk, 1), 0)

    def score_chunk(c, carry):
        kc = ki_ref[pl.ds(pl.multiple_of(c * blk, blk), blk), :]
        sc = jnp.zeros((blk, blk), F32)
        for h in range(IDX_HEADS):
            lg = _dot_nt(kc, qi_ref[:, h * IDX_DIM:(h + 1) * IDX_DIM])
            sc = sc + w_t[h:h + 1, :] * jnp.maximum(lg, 0.0)
        sc = sc + 0.0
        sc = jnp.where(c * blk + krow <= qpos, sc, -jnp.inf)
        bits = pltpu.bitcast(sc, jnp.int32)
        key_ref[c] = jnp.where(bits < 0, bits ^ 0x7FFFFFFF, bits)
        return carry

    lax.fori_loop(0, n_chunk, score_chunk, 0)

    def count_ge(thr):
        def body(c, acc):
            return acc + jnp.sum(jnp.where(key_ref[c] >= thr, 1.0, 0.0), axis=0, keepdims=True)
        return lax.fori_loop(0, n_chunk, body, jnp.zeros((1, blk), F32))

    keep = float(n_keep)
    thr = jnp.full((1, blk), INT_MIN, jnp.int32)
    zero = jnp.zeros((1, blk), jnp.int32)
    thr = jnp.where(count_ge(zero) >= keep, zero, thr)

    def bit_step(t, thr):
        cand = thr + jnp.left_shift(jnp.int32(1), 30 - t)
        return jnp.where(count_ge(cand) >= keep, cand, thr)

    thr = lax.fori_loop(0, 31, bit_step, thr)

    for h in range(N_HEADS):
        hs = slice(h * HEAD_DIM, (h + 1) * HEAD_DIM)
        qh = q_ref[:, hs]

        def step(c, carry):
            m, l, acc = carry
            kc = k_ref[pl.ds(pl.multiple_of(c * blk, blk), blk), hs]
            s = _dot_nt(kc, qh)
            allowed = (key_ref[c] >= thr) & (c * blk + krow <= qpos)
            s = jnp.where(allowed, s, NEG_BIG)
            m_new = jnp.maximum(m, jnp.max(s, axis=0, keepdims=True))
            p = jnp.where(allowed, jnp.exp(s - m_new), 0.0)
            alpha = jnp.exp(m - m_new)
            l = alpha * l + jnp.sum(p, axis=0, keepdims=True)
            acc = alpha * acc + _dot(vt_ref[c, hs, :], p.astype(BF16))
            return m_new, l, acc

        init = (jnp.full((1, blk), NEG_BIG, F32), jnp.zeros((1, blk), F32),
                jnp.zeros((HEAD_DIM, blk), F32))
        _, l, acc = lax.fori_loop(0, n_chunk, step, init)
        o_ref[0, hs, :] = (acc / l).astype(o_ref.dtype)


def _dsa(qi, wi, ki, q, k, vt, batch, seq):
    blk = ROW_TILE
    nq = seq // blk
    n_keep = min(DSA_TOPK_MAX, seq // 4)
    qrow = lambda w: pl.BlockSpec((blk, w), lambda b, i: (b * nq + i, 0))
    full = lambda w: pl.BlockSpec((seq, w), lambda b, i: (b, 0))
    return pl.pallas_call(
        functools.partial(_dsa_kernel, n_keep=n_keep),
        grid=(batch, nq),
        in_specs=[qrow(IDX_HEADS * IDX_DIM), qrow(LANES), full(IDX_DIM), qrow(HW), full(HW),
                  pl.BlockSpec((nq, HW, blk), lambda b, i: (b, 0, 0))],
        out_specs=pl.BlockSpec((1, HW, blk), lambda b, i: (b, 0, i)),
        out_shape=jax.ShapeDtypeStruct((batch, HW, seq), BF16),
        scratch_shapes=[pltpu.VMEM((nq, blk, blk), jnp.int32)],
        compiler_params=_params(2),
        name="dsa",
    )(qi, wi, ki, q, k, vt)


def _diff_kernel(q1_ref, q2_ref, k1_ref, k2_ref, v_ref, dl_ref, gain_ref, o_ref, *, lam_init):
    i = pl.program_id(1)
    blk = ROW_TILE
    dl = dl_ref[...]
    lam = (jnp.exp(jnp.sum(dl[0:1] * dl[1:2], axis=-1, keepdims=True))
           - jnp.exp(jnp.sum(dl[2:3] * dl[3:4], axis=-1, keepdims=True)) + lam_init)
    causal = (lax.broadcasted_iota(jnp.int32, (1, blk), 1)
              <= lax.broadcasted_iota(jnp.int32, (blk, 1), 0))

    for h in range(N_HEADS):
        hs = slice(h * HEAD_DIM, (h + 1) * HEAD_DIM)
        vs = slice(h * C_VDIM, (h + 1) * C_VDIM)

        def attend(q_ref, k_ref):
            qh = q_ref[:, hs]

            def step(j, carry, diag):
                m, l, acc = carry
                rows = pl.ds(pl.multiple_of(j * blk, blk), blk)
                s = _dot_nt(qh, k_ref[rows, hs])
                if diag:
                    s = jnp.where(causal, s, NEG_BIG)
                m_new = jnp.maximum(m, jnp.max(s, axis=-1, keepdims=True))
                p = jnp.exp(s - m_new)
                alpha = jnp.exp(m - m_new)
                l = alpha * l + jnp.sum(p, axis=-1, keepdims=True)
                acc = alpha * acc + _dot(p.astype(BF16), v_ref[rows, vs])
                return m_new, l, acc

            init = (jnp.full((blk, 1), NEG_BIG, F32), jnp.zeros((blk, 1), F32),
                    jnp.zeros((blk, C_VDIM), F32))
            carry = step(i, init, True)
            _, l, acc = lax.fori_loop(0, i, lambda j, cr: step(j, cr, False), carry)
            return acc / l

        o = attend(q1_ref, k1_ref) - lam * attend(q2_ref, k2_ref)
        ms = jnp.mean(o * o, axis=-1, keepdims=True)
        o = o * lax.rsqrt(ms + EPS) * gain_ref[...] * (1.0 - lam_init)
        o_ref[:, vs] = o.astype(o_ref.dtype)


def _diff(q1, q2, k1, k2, v, dl, gain, lam_init, batch, seq):
    blk = ROW_TILE
    nq = seq // blk
    qspec = pl.BlockSpec((blk, HW), lambda b, i: (b * nq + i, 0))
    kspec = pl.BlockSpec((seq, HW), lambda b, i: (b, 0))
    vw = N_HEADS * C_VDIM
    return pl.pallas_call(
        functools.partial(_diff_kernel, lam_init=lam_init),
        grid=(batch, nq),
        in_specs=[qspec, qspec, kspec, kspec, pl.BlockSpec((seq, vw), lambda b, i: (b, 0)),
                  pl.BlockSpec((4, HEAD_DIM), lambda b, i: (0, 0)),
                  pl.BlockSpec((1, C_VDIM), lambda b, i: (0, 0))],
        out_specs=pl.BlockSpec((blk, vw), lambda b, i: (b * nq + i, 0)),
        out_shape=jax.ShapeDtypeStruct((q1.shape[0], vw), BF16),
        compiler_params=_params(2),
        name="diff",
    )(q1, q2, k1, k2, v, dl, gain)


def _merge_kernel(x_ref, oa_ref, ob_ref, oc_ref, sg_ref, wa_ref, wb_ref, wc_ref, wo_ref, g_ref,
                  wr_hi_ref, wr_lo_ref, br_ref, x1_ref, hn_ref, cw_ref):
    merged = (sg_ref[:, 0:D_MODEL].astype(F32) * _dot(oa_ref[...], wa_ref[...])
              + sg_ref[:, D_MODEL:2 * D_MODEL].astype(F32) * _dot(ob_ref[...], wb_ref[...])
              + sg_ref[:, 2 * D_MODEL:3 * D_MODEL].astype(F32) * _dot(oc_ref[...], wc_ref[...]))
    x1 = x_ref[...] + _dot(merged.astype(BF16), wo_ref[...])
    x1_ref[...] = x1
    ms = jnp.mean(x1 * x1, axis=-1, keepdims=True)
    hn = x1 * lax.rsqrt(ms + EPS) * g_ref[...]
    hn_ref[...] = hn.astype(BF16)

    hi, lo = _split_bf16(hn)
    lg = (_dot(hi, wr_hi_ref[...]) + _dot(lo, wr_hi_ref[...]) + _dot(hi, wr_lo_ref[...])
          + br_ref[...])
    lane = lax.broadcasted_iota(jnp.int32, lg.shape, 1)
    lanef = lane.astype(F32)
    far = float(LANES)
    is_g = (lane >= N_EXPERTS) & (lane < N_EXPERTS + N_GROUPS)
    gl = jnp.where(is_g, lg, -jnp.inf)
    gmax = jnp.max(gl, axis=-1, keepdims=True)
    gidx = jnp.min(jnp.where(gl == gmax, lanef, far), axis=-1, keepdims=True) - float(N_EXPERTS)
    g_w = 1.0 / jnp.sum(jnp.where(is_g, jnp.exp(gl - gmax), 0.0), axis=-1, keepdims=True)
    in_group = (lane < N_EXPERTS) & ((lane // EXPERTS_PER_GROUP).astype(F32) == gidx)
    el = jnp.where(in_group, lg, -jnp.inf)
    e1 = jnp.max(el, axis=-1, keepdims=True)
    i1 = jnp.min(jnp.where(el == e1, lanef, far), axis=-1, keepdims=True)
    el2 = jnp.where(lanef == i1, -jnp.inf, el)
    e2 = jnp.max(el2, axis=-1, keepdims=True)
    i2 = jnp.min(jnp.where(el2 == e2, lanef, far), axis=-1, keepdims=True)
    t = jnp.exp(e2 - e1)
    w1 = g_w / (1.0 + t)
    w2 = g_w * t / (1.0 + t)
    cw_ref[...] = jnp.where(lanef == i1, w1, 0.0) + jnp.where(lanef == i2, w2, 0.0)


def _merge(x2, oa, ob, oc, sg, wa, wb, wc, wo, norm_g, wr_hi, wr_lo, br):
    n = x2.shape[0]
    tm = ROW_TILE
    row = lambda w: pl.BlockSpec((tm, w), lambda i: (i, 0))
    const = lambda a: pl.BlockSpec(a.shape, lambda i: (0, 0))
    return pl.pallas_call(
        _merge_kernel,
        grid=(n // tm,),
        in_specs=[row(D_MODEL), row(HW), row(HW), row(N_HEADS * C_VDIM), row(3 * D_MODEL),
                  const(wa), const(wb), const(wc), const(wo), const(norm_g), const(wr_hi),
                  const(wr_lo), const(br)],
        out_specs=[row(D_MODEL), row(D_MODEL), row(LANES)],
        out_shape=[jax.ShapeDtypeStruct((n, D_MODEL), F32), jax.ShapeDtypeStruct((n, D_MODEL), BF16),
                   jax.ShapeDtypeStruct((n, LANES), F32)],
        compiler_params=_params(1),
        name="merge",
    )(x2, oa, ob, oc, sg, wa, wb, wc, wo, norm_g, wr_hi, wr_lo, br)


def _moe_kernel(x1_ref, hn_ref, cw_ref, wg_ref, wu_ref, wd_ref, o_ref):
    e = pl.program_id(1)

    @pl.when(e == 0)
    def _():
        o_ref[...] = x1_ref[...]

    cw = cw_ref[...]
    lane = lax.broadcasted_iota(jnp.int32, cw.shape, 1)
    w = jnp.sum(jnp.where(lane == e, cw, 0.0), axis=-1, keepdims=True)
    x = hn_ref[...]
    g = _dot(x, wg_ref[0])
    u = _dot(x, wu_ref[0])
    hid = g * (1.0 / (1.0 + jnp.exp(-g))) * u
    o_ref[...] += w * _dot(hid.astype(BF16), wd_ref[0])


def _moe(x1, hn, cw, wg, wu, wd):
    n = x1.shape[0]
    tm = min(MOE_TILE, n)
    row = lambda w: pl.BlockSpec((tm, w), lambda t, e: (t, 0))
    return pl.pallas_call(
        _moe_kernel,
        grid=(n // tm, N_EXPERTS),
        in_specs=[row(D_MODEL), row(D_MODEL), row(LANES),
                  pl.BlockSpec((1, D_MODEL, D_EXPERT), lambda t, e: (e, 0, 0)),
                  pl.BlockSpec((1, D_MODEL, D_EXPERT), lambda t, e: (e, 0, 0)),
                  pl.BlockSpec((1, D_EXPERT, D_MODEL), lambda t, e: (e, 0, 0))],
        out_specs=row(D_MODEL),
        out_shape=jax.ShapeDtypeStruct((n, D_MODEL), F32),
        compiler_params=_params(2),
        name="moe",
    )(x1, hn, cw, wg, wu, wd)


def _rope_tables(seq):
    inv_freq = 1.0 / (ROPE_THETA ** (jnp.arange(0, HEAD_DIM, 2, dtype=F32) / HEAD_DIM))
    ang = jnp.arange(seq, dtype=F32)[:, None] * inv_freq[None, :]
    cos, sin = jnp.cos(ang), jnp.sin(ang)
    cos_t = jnp.tile(jnp.concatenate([cos, cos], axis=-1), (1, N_HEADS))
    sin_t = jnp.tile(jnp.concatenate([-sin, sin], axis=-1), (1, N_HEADS))
    return cos_t, sin_t


def kernel(x, norm_attn, w_in, qk_gain, idx_k_gain, diff_lambda, diff_subln_gain, w_proj_a, w_proj_b, w_proj_c, w_out, norm_ffn, w_group, b_group, w_router, b_router, w_e_gate, w_e_up, w_e_down):
    batch, seq, d = x.shape
    assert d == D_MODEL and seq % ROW_TILE == 0 and ROW_TILE == MOBA_BLOCK
    n = batch * seq
    nq = seq // ROW_TILE
    depth = w_in.shape[0]
    cos_t, sin_t = _rope_tables(seq)
    x2 = x.reshape(n, d)
    for l in range(depth):
        w_pad = jnp.concatenate(
            [w_in[l][:, :KW_SRC], jnp.zeros((d, LANES - IDX_DIM - IDX_HEADS), F32), w_in[l][:, KW_SRC:]],
            axis=1).astype(BF16)
        gains = jnp.tile(qk_gain[l][jnp.array([0, 1, 2, 3, 4, 4, 5, 5])], (1, N_HEADS))
        kgain = jnp.pad(idx_k_gain[l], (0, LANES - IDX_DIM))[None, :]
        (qa, ka, va, qb, kb, vb, qi, ki, wi, q1, q2, k1, k2, vc, sg, kmean) = _project(
            x2, norm_attn[l][None, :], w_pad, cos_t, sin_t, gains, kgain, seq)

        o_a = _moba(qa, ka, va, kmean.reshape(batch, nq, HW), batch, seq)
        vt = vb.reshape(batch, nq, ROW_TILE, HW).transpose(0, 1, 3, 2).reshape(batch * nq, HW, ROW_TILE)
        o_bt = _dsa(qi, wi, ki, qb, kb, vt, batch, seq)
        o_b = o_bt.transpose(0, 2, 1).reshape(n, HW)
        lam_init = 0.8 - 0.6 * math.exp(-0.3 * l)
        o_c = _diff(q1, q2, k1, k2, vc, diff_lambda[l], diff_subln_gain[l][None, :], lam_init,
                    batch, seq)

        w_r = jnp.concatenate([w_router[l], w_group[l],
                               jnp.zeros((d, LANES - N_EXPERTS - N_GROUPS), F32)], axis=1)
        wr_hi = w_r.astype(BF16)
        wr_lo = (w_r - wr_hi.astype(F32)).astype(BF16)
        b_r = jnp.concatenate([b_router[l], b_group[l],
                               jnp.zeros((LANES - N_EXPERTS - N_GROUPS,), F32)])[None, :]
        x1, hn, cw = _merge(x2, o_a, o_b, o_c, sg, w_proj_a[l].astype(BF16), w_proj_b[l].astype(BF16),
                            w_proj_c[l].astype(BF16), w_out[l].astype(BF16), norm_ffn[l][None, :],
                            wr_hi, wr_lo, b_r)
        x2 = _moe(x1, hn, cw, w_e_gate[l].astype(BF16), w_e_up[l].astype(BF16),
                  w_e_down[l].astype(BF16))
    return x2.reshape(batch, seq, d)
```

```python
import functools
import math

import jax
import jax.numpy as jnp
from jax import lax
from jax.experimental import pallas as pl
from jax.experimental.pallas import tpu as pltpu

F32 = jnp.float32
BF16 = jnp.bfloat16

D_MODEL = 1024
HEAD_DIM = 64
ROPE_THETA = 10000.0
EPS = 1e-6
N_HEADS = 4
MOBA_BLOCK = 256
MOBA_TOPK = 3
IDX_HEADS = 8
IDX_DIM = 64
DSA_TOPK_MAX = 256
C_VDIM = 2 * HEAD_DIM
N_GROUPS = 4
EXPERTS_PER_GROUP = 8
N_EXPERTS = N_GROUPS * EXPERTS_PER_GROUP
D_EXPERT = 512

HW = N_HEADS * HEAD_DIM
LANES = 128
ROW_TILE = 256
MOE_TILE = 512
VMEM_LIMIT = 56 * 1024 * 1024

_SEG = {}
_off = 0
for _name, _w in (("qa", HW), ("ka", HW), ("va", HW), ("qb", HW), ("kb", HW), ("vb", HW),
                  ("qi", IDX_HEADS * IDX_DIM), ("kw", LANES), ("q1", HW), ("q2", HW), ("k1", HW),
                  ("k2", HW), ("vc", N_HEADS * C_VDIM), ("ga", D_MODEL), ("gb", D_MODEL),
                  ("gc", D_MODEL)):
    _SEG[_name] = (_off, _w)
    _off += _w
D_IN_PAD = _off
KW_SRC = 6 * HW + IDX_HEADS * IDX_DIM + IDX_DIM + IDX_HEADS

NEG_BIG = -1e30
M_FLOOR = -1e20
INT_MIN = -(2 ** 31)
LOG2E = math.log2(math.e)
Q_SCALE = HEAD_DIM ** -0.5 * LOG2E
V_PAD = 16
NT_DIMS = (((1,), (1,)), ((), ()))


def _params(n_axes):
    return pltpu.CompilerParams(dimension_semantics=("arbitrary",) * n_axes,
                                vmem_limit_bytes=VMEM_LIMIT)


def _dot(a, b):
    return jnp.dot(a, b, preferred_element_type=F32)


def _dot_nt(a, b):
    return lax.dot_general(a, b, NT_DIMS, preferred_element_type=F32)


def _split_bf16(a):
    hi = a.astype(BF16)
    return hi, (a - hi.astype(F32)).astype(BF16)


def _swap_halves(y, width):
    lane = lax.broadcasted_iota(jnp.int32, y.shape, 1)
    first = (lane % HEAD_DIM) < (HEAD_DIM // 2)
    return jnp.where(first, pltpu.roll(y, width - HEAD_DIM // 2, 1), pltpu.roll(y, HEAD_DIM // 2, 1))


def _proj_kernel(x_ref, g_ref, w_ref, cos_ref, sin_ref, gain_ref, kgain_ref,
                 qa_ref, ka_ref, va_ref, qb_ref, kb_ref, vb_ref, qi_ref, ki_ref, wi_ref,
                 q1_ref, q2_ref, k1_ref, k2_ref, vc_ref, sg_ref, kmean_ref):
    x = x_ref[...]
    ms = jnp.mean(x * x, axis=-1, keepdims=True)
    h = (x * lax.rsqrt(ms + EPS) * g_ref[...]).astype(BF16)
    cos = cos_ref[...]
    sin = sin_ref[...]
    r = lax.broadcasted_iota(jnp.int32, (HW, HW), 0) // HEAD_DIM
    c = lax.broadcasted_iota(jnp.int32, (HW, HW), 1) // HEAD_DIM
    head_ones = (r == c).astype(BF16)

    def seg(name, lo=0, width=None):
        off, w = _SEG[name]
        width = w if width is None else width
        return _dot(h, w_ref[:, off + lo:off + lo + width])

    def rope(y):
        return y * cos + _swap_halves(y, HW) * sin

    def norm_rope(t, gain_row):
        hi, lo = _split_bf16(t * t)
        ss = _dot(hi, head_ones) + _dot(lo, head_ones)
        yn = t * lax.rsqrt(ss * (1.0 / HEAD_DIM) + EPS) * gain_ref[gain_row:gain_row + 1, :]
        return rope(yn)

    qa_ref[...] = (norm_rope(seg("qa"), 0) * Q_SCALE).astype(BF16)
    ka = norm_rope(seg("ka"), 1)
    ka_ref[...] = ka.astype(BF16)
    kmean_ref[0] = jnp.mean(ka, axis=0, keepdims=True)
    va_ref[...] = seg("va").astype(BF16)
    qb_ref[...] = (norm_rope(seg("qb"), 2) * Q_SCALE).astype(BF16)
    kb_ref[...] = norm_rope(seg("kb"), 3).astype(BF16)
    vb_ref[...] = seg("vb").astype(BF16)
    for half in range(2):
        qi_ref[:, half * HW:(half + 1) * HW] = rope(seg("qi", half * HW, HW)).astype(BF16)

    t = seg("kw")
    lane = lax.broadcasted_iota(jnp.int32, t.shape, 1)
    is_k = lane < IDX_DIM
    kms = jnp.sum(jnp.where(is_k, t * t, 0.0), axis=-1, keepdims=True) * (1.0 / IDX_DIM)
    kn = t * lax.rsqrt(kms + EPS) * kgain_ref[...]
    kr = kn * cos[:, :LANES] + _swap_halves(kn, LANES) * sin[:, :LANES]
    ki_ref[...] = kr[:, :IDX_DIM].astype(BF16)
    w_scale = (IDX_HEADS ** -0.5) * (IDX_DIM ** -0.5)
    wi_ref[...] = jnp.where(lane < IDX_HEADS, pltpu.roll(t, LANES - IDX_DIM, 1) * w_scale, 0.0)

    q1_ref[...] = (norm_rope(seg("q1"), 4) * Q_SCALE).astype(BF16)
    q2_ref[...] = (norm_rope(seg("q2"), 5) * Q_SCALE).astype(BF16)
    k1_ref[...] = norm_rope(seg("k1"), 6).astype(BF16)
    k2_ref[...] = norm_rope(seg("k2"), 7).astype(BF16)
    for half in range(2):
        vc_ref[:, half * HW:(half + 1) * HW] = seg("vc", half * HW, HW).astype(BF16)
    for gi, name in enumerate(("ga", "gb", "gc")):
        for part in range(D_MODEL // 512):
            g = seg(name, part * 512, 512)
            lo = gi * D_MODEL + part * 512
            sg_ref[:, lo:lo + 512] = (1.0 / (1.0 + jnp.exp(-g))).astype(BF16)


def _project(x2, norm_g, w_pad, cos_t, sin_t, gains, kgain, seq):
    n = x2.shape[0]
    tm = ROW_TILE
    n_pos = seq // tm
    row = lambda w: pl.BlockSpec((tm, w), lambda i: (i, 0))
    const = lambda shape: pl.BlockSpec(shape, lambda i: (0,) * len(shape))
    out_widths = [HW] * 6 + [IDX_HEADS * IDX_DIM, IDX_DIM, LANES] + [HW] * 4 + [N_HEADS * C_VDIM, 3 * D_MODEL]
    out_dtypes = [BF16] * 8 + [F32] + [BF16] * 6
    out_shape = [jax.ShapeDtypeStruct((n, w), dt) for w, dt in zip(out_widths, out_dtypes)]
    out_shape.append(jax.ShapeDtypeStruct((n // tm, 1, HW), F32))
    out_specs = [row(w) for w in out_widths] + [pl.BlockSpec((1, 1, HW), lambda i: (i, 0, 0))]
    return pl.pallas_call(
        _proj_kernel,
        grid=(n // tm,),
        in_specs=[row(D_MODEL), const((1, D_MODEL)),
                  pl.BlockSpec((D_MODEL, D_IN_PAD), lambda i: (0, 0), pipeline_mode=pl.Buffered(1)),
                  pl.BlockSpec((tm, HW), lambda i: (i % n_pos, 0)),
                  pl.BlockSpec((tm, HW), lambda i: (i % n_pos, 0)),
                  const((8, HW)), const((1, LANES))],
        out_specs=out_specs,
        out_shape=out_shape,
        compiler_params=_params(1),
        name="proj",
    )(x2, norm_g, w_pad, cos_t, sin_t, gains, kgain)


SUB_KEYS = 256


def _online_update(ss, ms, acc_ref, vts, col_oks=None):
    ps, out = [], []
    for c, s in enumerate(ss):
        m = ms[c]
        m_new = jnp.maximum(m, jnp.max(s, axis=0, keepdims=True))
        m_eff = jnp.maximum(m_new, M_FLOOR)
        if col_oks is not None:
            m_new = jnp.where(col_oks[c], m_new, m)
            m_eff = jnp.where(col_oks[c], m_eff, -NEG_BIG)
        out.append(m_new)
        ps.append((jnp.exp2(m - m_new), jnp.exp2(s - m_eff).astype(BF16)))
    for c, (alpha, p) in enumerate(ps):
        acc_ref[c] = alpha * acc_ref[c] + _dot(vts[c], p)
    return out


def _init_max(n_chains, tq):
    return tuple(jnp.full((1, tq), NEG_BIG, F32) for _ in range(n_chains))


def _normalized(acc_ref, c, dv):
    acc = acc_ref[c]
    return acc[:dv] / acc[dv:dv + 1]


def _head_slice(h, width=HEAD_DIM):
    return slice(h * width, (h + 1) * width)


def _moba_kernel(qt_ref, k_ref, vt_ref, kmean_ref, o_ref, acc_ref, *, n_sel):
    i = pl.program_id(1)
    blk = MOBA_BLOCK
    nb = kmean_ref.shape[1]
    n_part = blk // SUB_KEYS
    km = kmean_ref[0]
    brow = lax.broadcasted_iota(jnp.int32, (nb, blk), 0)
    browf = brow.astype(F32)
    causal = (lax.broadcasted_iota(jnp.int32, (blk, 1), 0)
              <= lax.broadcasted_iota(jnp.int32, (1, blk), 1))
    qts = [qt_ref[0, _head_slice(h), :] for h in range(N_HEADS)]
    sels = []
    for h in range(N_HEADS):
        km_hi, km_lo = _split_bf16(km[:, _head_slice(h)])
        gate = _dot(km_hi, qts[h]) + _dot(km_lo, qts[h])
        gate = jnp.where(brow < i, gate, -jnp.inf)
        sel = jnp.zeros((nb, blk), F32)
        for _ in range(n_sel):
            gm = jnp.max(gate, axis=0, keepdims=True)
            is_m = (gate == gm) & (gm > -jnp.inf)
            first = jnp.min(jnp.where(is_m, browf, float(nb)), axis=0, keepdims=True)
            pick = browf == first
            sel = jnp.where(pick, 1.0, sel)
            gate = jnp.where(pick, -jnp.inf, gate)
        sels.append(sel)
    acc_ref[...] = jnp.zeros(acc_ref.shape, F32)

    def scores(j, part):
        rows = pl.ds(pl.multiple_of(j * blk + part * SUB_KEYS, SUB_KEYS), SUB_KEYS)
        return [_dot(k_ref[rows, _head_slice(h)], qts[h]) for h in range(N_HEADS)]

    def values(j, part):
        cols = slice(part * SUB_KEYS, (part + 1) * SUB_KEYS)
        return [vt_ref[j, _head_slice(h, HEAD_DIM + V_PAD), cols] for h in range(N_HEADS)]

    def body(j, stats):
        oks = [jnp.sum(jnp.where(brow == j, sels[h], 0.0), axis=0, keepdims=True) > 0.0
               for h in range(N_HEADS)]
        for part in range(n_part):
            stats = _online_update(scores(j, part), stats, acc_ref, values(j, part), oks)
        return tuple(stats)

    stats = lax.fori_loop(0, i, body, _init_max(N_HEADS, blk))
    for part in range(n_part):
        cmask = causal[part * SUB_KEYS:(part + 1) * SUB_KEYS, :]
        ss = [jnp.where(cmask, s, NEG_BIG) for s in scores(i, part)]
        stats = _online_update(ss, stats, acc_ref, values(i, part))
    for h in range(N_HEADS):
        o_ref[0, _head_slice(h), :] = _normalized(acc_ref, h, HEAD_DIM).astype(o_ref.dtype)


def _moba(qt, k, vt, kmean, batch, seq):
    blk = MOBA_BLOCK
    nb = seq // blk
    n_sel = min(MOBA_TOPK, nb - 1)
    tspec = pl.BlockSpec((1, HW, blk), lambda b, i: (b, 0, i))
    return pl.pallas_call(
        functools.partial(_moba_kernel, n_sel=n_sel),
        grid=(batch, nb),
        in_specs=[tspec,
                  pl.BlockSpec((seq, HW), lambda b, i: (b, 0)),
                  pl.BlockSpec((nb, vt.shape[1], blk), lambda b, i: (b, 0, 0)),
                  pl.BlockSpec((1, nb, HW), lambda b, i: (b, 0, 0))],
        out_specs=tspec,
        out_shape=jax.ShapeDtypeStruct((batch, HW, seq), BF16),
        scratch_shapes=[pltpu.VMEM((N_HEADS, HEAD_DIM + V_PAD, blk), F32)],
        compiler_params=_params(2),
        name="moba",
    )(qt, k, vt, kmean)


def _dsa_kernel(qit_ref, wi_ref, ki_ref, qt_ref, k_ref, vt_ref, o_ref, key_ref, acc_ref, *, n_keep,
                idx_bits):
    i = pl.program_id(1)
    blk = ROW_TILE
    n_chunk = i + 1
    n_part = blk // SUB_KEYS
    w_t = wi_ref[...].T
    qpos = i * blk + lax.broadcasted_iota(jnp.int32, (1, blk), 1)
    krow = lax.broadcasted_iota(jnp.int32, (blk, 1), 0)

    def score_chunk(c, carry):
        kc = ki_ref[pl.ds(pl.multiple_of(c * blk, blk), blk), :]
        lgs = [_dot(kc, qit_ref[0, _head_slice(h, IDX_DIM), :]) for h in range(IDX_HEADS)]
        sc = jnp.zeros((blk, blk), F32)
        for h in range(IDX_HEADS):
            sc = sc + w_t[h:h + 1, :] * jnp.maximum(lgs[h], 0.0)
        sc = sc + 0.0
        bits = pltpu.bitcast(sc, jnp.int32)
        key = jnp.where(bits < 0, bits ^ 0x7FFFFFFF, bits)
        key_ref[c] = jnp.where(c * blk + krow <= qpos, key, INT_MIN)
        return carry

    lax.fori_loop(0, n_chunk, score_chunk, 0)

    @pl.when(n_chunk % 2 == 1)
    def _():
        key_ref[n_chunk] = jnp.full((blk, blk), INT_MIN, jnp.int32)

    def count(pred):
        def body(c2, acc):
            for c in (2 * c2, 2 * c2 + 1):
                hit = jnp.where(pred(c, key_ref[c]), 1.0, 0.0)
                acc = acc + jnp.sum(hit.reshape(blk // 8, 8, blk), axis=0)
            return acc
        acc = lax.fori_loop(0, (n_chunk + 1) // 2, body, jnp.zeros((8, blk), F32))
        return jnp.sum(acc, axis=0, keepdims=True)

    keep = float(n_keep)
    thr = jnp.full((1, blk), INT_MIN, jnp.int32)
    zero = jnp.zeros((1, blk), jnp.int32)
    thr = jnp.where(count(lambda c, k: k >= zero) >= keep, zero, thr)

    def bit_step(t, thr):
        cand = thr + jnp.left_shift(jnp.int32(1), 30 - t)
        return jnp.where(count(lambda c, k: k >= cand) >= keep, cand, thr)

    thr = lax.fori_loop(0, 31, bit_step, thr)

    tied = (count(lambda c, k: k >= thr) > keep) & (thr > INT_MIN)

    @pl.when(jnp.max(jnp.where(tied, 1.0, 0.0)) > 0.0)
    def _():
        need = keep - count(lambda c, k: k > thr)

        def idx_step(t, last):
            cand = last + jnp.left_shift(jnp.int32(1), idx_bits - 1 - t)
            n = count(lambda c, k: (k == thr) & (c * blk + krow < cand))
            return jnp.where(n < need, cand, last)

        last = lax.fori_loop(0, idx_bits, idx_step, zero)

        def demote(c, carry):
            k = key_ref[c]
            hit = (k == thr) & (c * blk + krow > last) & tied
            key_ref[c] = jnp.where(hit, thr - 1, k)
            return carry

        lax.fori_loop(0, n_chunk, demote, 0)

    thr_eff = jnp.maximum(thr, INT_MIN + 1)
    qts = [qt_ref[0, _head_slice(h), :] for h in range(N_HEADS)]
    acc_ref[...] = jnp.zeros(acc_ref.shape, F32)

    def body(c, stats):
        for part in range(n_part):
            lo = part * SUB_KEYS
            rows = pl.ds(pl.multiple_of(c * blk + lo, SUB_KEYS), SUB_KEYS)
            allowed = key_ref[c, lo:lo + SUB_KEYS, :] >= thr_eff
            ss = [jnp.where(allowed, _dot(k_ref[rows, _head_slice(h)], qts[h]), NEG_BIG)
                  for h in range(N_HEADS)]
            vts = [vt_ref[c, _head_slice(h, HEAD_DIM + V_PAD), lo:lo + SUB_KEYS] for h in range(N_HEADS)]
            stats = _online_update(ss, stats, acc_ref, vts)
        return tuple(stats)

    lax.fori_loop(0, n_chunk, body, _init_max(N_HEADS, blk))
    for h in range(N_HEADS):
        o_ref[0, _head_slice(h), :] = _normalized(acc_ref, h, HEAD_DIM).astype(o_ref.dtype)


def _dsa(qit, wi, ki, qt, k, vt, batch, seq):
    blk = ROW_TILE
    nq = seq // blk
    n_keep = min(DSA_TOPK_MAX, seq // 4)
    tspec = lambda w: pl.BlockSpec((1, w, blk), lambda b, i: (b, 0, i))
    full = lambda w: pl.BlockSpec((seq, w), lambda b, i: (b, 0))
    return pl.pallas_call(
        functools.partial(_dsa_kernel, n_keep=n_keep, idx_bits=(seq - 1).bit_length()),
        grid=(batch, nq),
        in_specs=[tspec(IDX_HEADS * IDX_DIM), pl.BlockSpec((blk, LANES), lambda b, i: (b * nq + i, 0)),
                  full(IDX_DIM), tspec(HW), full(HW),
                  pl.BlockSpec((nq, vt.shape[1], blk), lambda b, i: (b, 0, 0))],
        out_specs=tspec(HW),
        out_shape=jax.ShapeDtypeStruct((batch, HW, seq), BF16),
        scratch_shapes=[pltpu.VMEM((nq + nq % 2, blk, blk), jnp.int32),
                        pltpu.VMEM((N_HEADS, HEAD_DIM + V_PAD, blk), F32)],
        compiler_params=_params(2),
        name="dsa",
    )(qit, wi, ki, qt, k, vt)


DIFF_GROUP = 2


def _diff_kernel(q1t_ref, q2t_ref, k1_ref, k2_ref, vt_ref, dl_ref, gain_ref, o_ref, acc_ref, *, lam_init):
    i = pl.program_id(1)
    blk = ROW_TILE
    n_part = blk // SUB_KEYS
    dl = dl_ref[...]
    lam = (jnp.exp(jnp.sum(dl[0:1] * dl[1:2], axis=-1, keepdims=True))
           - jnp.exp(jnp.sum(dl[2:3] * dl[3:4], axis=-1, keepdims=True)) + lam_init)
    causal = (lax.broadcasted_iota(jnp.int32, (blk, 1), 0)
              <= lax.broadcasted_iota(jnp.int32, (1, blk), 1))
    maps = ((q1t_ref, k1_ref), (q2t_ref, k2_ref))

    for h0 in range(0, N_HEADS, DIFF_GROUP):
        chains = [(h, mp) for h in range(h0, h0 + DIFF_GROUP) for mp in range(2)]
        qts = [maps[mp][0][0, _head_slice(h), :] for h, mp in chains]
        acc_ref[...] = jnp.zeros(acc_ref.shape, F32)

        def step(j, stats, diag):
            for part in range(n_part):
                lo = part * SUB_KEYS
                rows = pl.ds(pl.multiple_of(j * blk + lo, SUB_KEYS), SUB_KEYS)
                ss = [_dot(maps[mp][1][rows, _head_slice(h)], qts[c]) for c, (h, mp) in enumerate(chains)]
                if diag:
                    ss = [jnp.where(causal[lo:lo + SUB_KEYS, :], s, NEG_BIG) for s in ss]
                vts = [vt_ref[j, _head_slice(h, C_VDIM + V_PAD), lo:lo + SUB_KEYS] for h, _ in chains]
                stats = _online_update(ss, stats, acc_ref, vts)
            return tuple(stats)

        stats = lax.fori_loop(0, i, lambda j, st: step(j, st, False), _init_max(len(chains), blk))
        step(i, stats, True)
        for g in range(DIFF_GROUP):
            h = h0 + g
            o = _normalized(acc_ref, 2 * g, C_VDIM) - lam * _normalized(acc_ref, 2 * g + 1, C_VDIM)
            ms = jnp.mean(o * o, axis=0, keepdims=True)
            o = o * lax.rsqrt(ms + EPS) * gain_ref[...] * (1.0 - lam_init)
            o_ref[0, _head_slice(h, C_VDIM), :] = o.astype(o_ref.dtype)


def _diff(q1t, q2t, k1, k2, vt, dl, gain, lam_init, batch, seq):
    blk = ROW_TILE
    nq = seq // blk
    tspec = lambda w: pl.BlockSpec((1, w, blk), lambda b, i: (b, 0, i))
    kspec = pl.BlockSpec((seq, HW), lambda b, i: (b, 0))
    vw = N_HEADS * C_VDIM
    return pl.pallas_call(
        functools.partial(_diff_kernel, lam_init=lam_init),
        grid=(batch, nq),
        in_specs=[tspec(HW), tspec(HW), kspec, kspec,
                  pl.BlockSpec((nq, vt.shape[1], blk), lambda b, i: (b, 0, 0)),
                  pl.BlockSpec((4, HEAD_DIM), lambda b, i: (0, 0)),
                  pl.BlockSpec((C_VDIM, 1), lambda b, i: (0, 0))],
        out_specs=tspec(vw),
        out_shape=jax.ShapeDtypeStruct((batch, vw, seq), BF16),
        scratch_shapes=[pltpu.VMEM((2 * DIFF_GROUP, C_VDIM + V_PAD, blk), F32)],
        compiler_params=_params(2),
        name="diff",
    )(q1t, q2t, k1, k2, vt, dl, gain)


def _merge_kernel(x_ref, oa_ref, ob_ref, oc_ref, sg_ref, wa_ref, wb_ref, wc_ref, wo_ref, g_ref,
                  wr_hi_ref, wr_lo_ref, br_ref, x1_ref, hn_ref, cw_ref):
    merged = (sg_ref[:, 0:D_MODEL].astype(F32) * _dot(oa_ref[...], wa_ref[...])
              + sg_ref[:, D_MODEL:2 * D_MODEL].astype(F32) * _dot(ob_ref[...], wb_ref[...])
              + sg_ref[:, 2 * D_MODEL:3 * D_MODEL].astype(F32) * _dot(oc_ref[...], wc_ref[...]))
    x1 = x_ref[...] + _dot(merged.astype(BF16), wo_ref[...])
    x1_ref[...] = x1
    ms = jnp.mean(x1 * x1, axis=-1, keepdims=True)
    hn = x1 * lax.rsqrt(ms + EPS) * g_ref[...]
    hn_ref[...] = hn.astype(BF16)

    hi, lo = _split_bf16(hn)
    lg = (_dot(hi, wr_hi_ref[...]) + _dot(lo, wr_hi_ref[...]) + _dot(hi, wr_lo_ref[...])
          + br_ref[...])
    lane = lax.broadcasted_iota(jnp.int32, lg.shape, 1)
    lanef = lane.astype(F32)
    far = float(LANES)
    is_g = (lane >= N_EXPERTS) & (lane < N_EXPERTS + N_GROUPS)
    gl = jnp.where(is_g, lg, -jnp.inf)
    gmax = jnp.max(gl, axis=-1, keepdims=True)
    gidx = jnp.min(jnp.where(gl == gmax, lanef, far), axis=-1, keepdims=True) - float(N_EXPERTS)
    g_w = 1.0 / jnp.sum(jnp.where(is_g, jnp.exp(gl - gmax), 0.0), axis=-1, keepdims=True)
    in_group = (lane < N_EXPERTS) & ((lane // EXPERTS_PER_GROUP).astype(F32) == gidx)
    el = jnp.where(in_group, lg, -jnp.inf)
    e1 = jnp.max(el, axis=-1, keepdims=True)
    i1 = jnp.min(jnp.where(el == e1, lanef, far), axis=-1, keepdims=True)
    el2 = jnp.where(lanef == i1, -jnp.inf, el)
    e2 = jnp.max(el2, axis=-1, keepdims=True)
    i2 = jnp.min(jnp.where(el2 == e2, lanef, far), axis=-1, keepdims=True)
    t = jnp.exp(e2 - e1)
    w1 = g_w / (1.0 + t)
    w2 = g_w * t / (1.0 + t)
    cw_ref[...] = jnp.where(lanef == i1, w1, 0.0) + jnp.where(lanef == i2, w2, 0.0)


def _merge(x2, oa, ob, oc, sg, wa, wb, wc, wo, norm_g, wr_hi, wr_lo, br):
    n = x2.shape[0]
    tm = ROW_TILE
    row = lambda w: pl.BlockSpec((tm, w), lambda i: (i, 0))
    const = lambda a: pl.BlockSpec(a.shape, lambda i: (0, 0))
    return pl.pallas_call(
        _merge_kernel,
        grid=(n // tm,),
        in_specs=[row(D_MODEL), row(HW), row(HW), row(N_HEADS * C_VDIM), row(3 * D_MODEL),
                  const(wa), const(wb), const(wc), const(wo), const(norm_g), const(wr_hi),
                  const(wr_lo), const(br)],
        out_specs=[row(D_MODEL), row(D_MODEL), row(LANES)],
        out_shape=[jax.ShapeDtypeStruct((n, D_MODEL), F32), jax.ShapeDtypeStruct((n, D_MODEL), BF16),
                   jax.ShapeDtypeStruct((n, LANES), F32)],
        compiler_params=_params(1),
        name="merge",
    )(x2, oa, ob, oc, sg, wa, wb, wc, wo, norm_g, wr_hi, wr_lo, br)


def _moe_kernel(x1_ref, hn_ref, cw_ref, wg_ref, wu_ref, wd_ref, o_ref):
    e = pl.program_id(1)

    @pl.when(e == 0)
    def _():
        o_ref[...] = x1_ref[...]

    cw = cw_ref[...]
    lane = lax.broadcasted_iota(jnp.int32, cw.shape, 1)
    w = jnp.sum(jnp.where(lane == e, cw, 0.0), axis=-1, keepdims=True)
    x = hn_ref[...]
    g = _dot(x, wg_ref[0])
    u = _dot(x, wu_ref[0])
    hid = g * (1.0 / (1.0 + jnp.exp(-g))) * u
    o_ref[...] += w * _dot(hid.astype(BF16), wd_ref[0])


def _moe(x1, hn, cw, wg, wu, wd):
    n = x1.shape[0]
    tm = min(MOE_TILE, n)
    row = lambda w: pl.BlockSpec((tm, w), lambda t, e: (t, 0))
    return pl.pallas_call(
        _moe_kernel,
        grid=(n // tm, N_EXPERTS),
        in_specs=[row(D_MODEL), row(D_MODEL), row(LANES),
                  pl.BlockSpec((1, D_MODEL, D_EXPERT), lambda t, e: (e, 0, 0)),
                  pl.BlockSpec((1, D_MODEL, D_EXPERT), lambda t, e: (e, 0, 0)),
                  pl.BlockSpec((1, D_EXPERT, D_MODEL), lambda t, e: (e, 0, 0))],
        out_specs=row(D_MODEL),
        out_shape=jax.ShapeDtypeStruct((n, D_MODEL), F32),
        compiler_params=_params(2),
        name="moe",
    )(x1, hn, cw, wg, wu, wd)


def _rope_tables(seq):
    inv_freq = 1.0 / (ROPE_THETA ** (jnp.arange(0, HEAD_DIM, 2, dtype=F32) / HEAD_DIM))
    ang = jnp.arange(seq, dtype=F32)[:, None] * inv_freq[None, :]
    cos, sin = jnp.cos(ang), jnp.sin(ang)
    cos_t = jnp.tile(jnp.concatenate([cos, cos], axis=-1), (1, N_HEADS))
    sin_t = jnp.tile(jnp.concatenate([-sin, sin], axis=-1), (1, N_HEADS))
    return cos_t, sin_t


def _to_t(v, batch, n_chunk):
    tokens = v.shape[0]
    v = v.reshape(tokens, N_HEADS, -1)
    pad = jnp.zeros((tokens, N_HEADS, V_PAD), v.dtype).at[:, :, 0].set(1)
    v = jnp.concatenate([v, pad], axis=2).reshape(tokens, -1)
    feat = v.shape[1]
    return (v.reshape(batch, n_chunk, ROW_TILE, feat).transpose(0, 1, 3, 2)
            .reshape(batch * n_chunk, feat, ROW_TILE))


def _from_t(o_t):
    b, feat, seq = o_t.shape
    return o_t.transpose(0, 2, 1).reshape(b * seq, feat)


def kernel(x, norm_attn, w_in, qk_gain, idx_k_gain, diff_lambda, diff_subln_gain, w_proj_a, w_proj_b, w_proj_c, w_out, norm_ffn, w_group, b_group, w_router, b_router, w_e_gate, w_e_up, w_e_down):
    batch, seq, d = x.shape
    assert d == D_MODEL and seq % ROW_TILE == 0 and ROW_TILE == MOBA_BLOCK
    n = batch * seq
    nq = seq // ROW_TILE
    depth = w_in.shape[0]
    cos_t, sin_t = _rope_tables(seq)
    x2 = x.reshape(n, d)
    for l in range(depth):
        w_pad = jnp.concatenate(
            [w_in[l][:, :KW_SRC], jnp.zeros((d, LANES - IDX_DIM - IDX_HEADS), F32), w_in[l][:, KW_SRC:]],
            axis=1).astype(BF16)
        gains = jnp.tile(qk_gain[l][jnp.array([0, 1, 2, 3, 4, 4, 5, 5])], (1, N_HEADS))
        kgain = jnp.pad(idx_k_gain[l], (0, LANES - IDX_DIM))[None, :]
        (qa, ka, va, qb, kb, vb, qi, ki, wi, q1, q2, k1, k2, vc, sg, kmean) = _project(
            x2, norm_attn[l][None, :], w_pad, cos_t, sin_t, gains, kgain, seq)

        feat_major = lambda t: t.reshape(batch, seq, t.shape[1]).transpose(0, 2, 1)
        o_a = _from_t(_moba(feat_major(qa), ka, _to_t(va, batch, nq), kmean.reshape(batch, nq, HW),
                            batch, seq))
        o_b = _from_t(_dsa(feat_major(qi), wi, ki, feat_major(qb), kb, _to_t(vb, batch, nq), batch, seq))
        lam_init = 0.8 - 0.6 * math.exp(-0.3 * l)
        o_c = _from_t(_diff(feat_major(q1), feat_major(q2), k1, k2, _to_t(vc, batch, nq), diff_lambda[l],
                            diff_subln_gain[l][:, None], lam_init, batch, seq))

        w_r = jnp.concatenate([w_router[l], w_group[l],
                               jnp.zeros((d, LANES - N_EXPERTS - N_GROUPS), F32)], axis=1)
        wr_hi = w_r.astype(BF16)
        wr_lo = (w_r - wr_hi.astype(F32)).astype(BF16)
        b_r = jnp.concatenate([b_router[l], b_group[l],
                               jnp.zeros((LANES - N_EXPERTS - N_GROUPS,), F32)])[None, :]
        x1, hn, cw = _merge(x2, o_a, o_b, o_c, sg, w_proj_a[l].astype(BF16), w_proj_b[l].astype(BF16),
                            w_proj_c[l].astype(BF16), w_out[l].astype(BF16), norm_ffn[l][None, :],
                            wr_hi, wr_lo, b_r)
        x2 = _moe(x1, hn, cw, w_e_gate[l].astype(BF16), w_e_up[l].astype(BF16),
                  w_e_down[l].astype(BF16))
    return x2.reshape(batch, seq, d)
```

```python
import functools
import math

import jax
import jax.numpy as jnp
from jax import lax
from jax.experimental import pallas as pl
from jax.experimental.pallas import tpu as pltpu

F32 = jnp.float32
BF16 = jnp.bfloat16

D_MODEL = 1024
HEAD_DIM = 64
ROPE_THETA = 10000.0
EPS = 1e-6
N_HEADS = 4
MOBA_BLOCK = 256
MOBA_TOPK = 3
IDX_HEADS = 8
IDX_DIM = 64
DSA_TOPK_MAX = 256
C_VDIM = 2 * HEAD_DIM
N_GROUPS = 4
EXPERTS_PER_GROUP = 8
N_EXPERTS = N_GROUPS * EXPERTS_PER_GROUP
D_EXPERT = 512

HW = N_HEADS * HEAD_DIM
LANES = 128
ROW_TILE = 256
VMEM_LIMIT = 56 * 1024 * 1024

_SEG = {}
_off = 0
for _name, _w in (("qa", HW), ("ka", HW), ("va", HW), ("qb", HW), ("kb", HW), ("vb", HW),
                  ("qi", IDX_HEADS * IDX_DIM), ("kw", LANES), ("q1", HW), ("q2", HW), ("k1", HW),
                  ("k2", HW), ("vc", N_HEADS * C_VDIM), ("ga", D_MODEL), ("gb", D_MODEL),
                  ("gc", D_MODEL)):
    _SEG[_name] = (_off, _w)
    _off += _w
D_IN_PAD = _off
KW_SRC = 6 * HW + IDX_HEADS * IDX_DIM + IDX_DIM + IDX_HEADS

NEG_BIG = -1e30
M_FLOOR = -1e20
INT_MIN = -(2 ** 31)
LOG2E = math.log2(math.e)
Q_SCALE = HEAD_DIM ** -0.5 * LOG2E
V_PAD = 16
NT_DIMS = (((1,), (1,)), ((), ()))


def _params(n_axes):
    return pltpu.CompilerParams(dimension_semantics=("arbitrary",) * n_axes,
                                vmem_limit_bytes=VMEM_LIMIT)


def _dot(a, b):
    return jnp.dot(a, b, preferred_element_type=F32)


def _dot_nt(a, b):
    return lax.dot_general(a, b, NT_DIMS, preferred_element_type=F32)


def _split_bf16(a):
    hi = a.astype(BF16)
    return hi, (a - hi.astype(F32)).astype(BF16)


def _swap_halves(y, width):
    lane = lax.broadcasted_iota(jnp.int32, y.shape, 1)
    first = (lane % HEAD_DIM) < (HEAD_DIM // 2)
    return jnp.where(first, pltpu.roll(y, width - HEAD_DIM // 2, 1), pltpu.roll(y, HEAD_DIM // 2, 1))


def _proj_kernel(x_ref, g_ref, w_ref, cos_ref, sin_ref, gain_ref, kgain_ref,
                 qa_ref, ka_ref, va_ref, qb_ref, kb_ref, vb_ref, qi_ref, ki_ref, wi_ref,
                 q1_ref, q2_ref, k1_ref, k2_ref, vc_ref, sg_ref, kmean_ref):
    x = x_ref[...]
    ms = jnp.mean(x * x, axis=-1, keepdims=True)
    h = (x * lax.rsqrt(ms + EPS) * g_ref[...]).astype(BF16)
    cos = cos_ref[...]
    sin = sin_ref[...]
    r = lax.broadcasted_iota(jnp.int32, (HW, HW), 0) // HEAD_DIM
    c = lax.broadcasted_iota(jnp.int32, (HW, HW), 1) // HEAD_DIM
    head_ones = (r == c).astype(BF16)

    def seg(name, lo=0, width=None):
        off, w = _SEG[name]
        width = w if width is None else width
        return _dot(h, w_ref[:, off + lo:off + lo + width])

    def rope(y):
        return y * cos + _swap_halves(y, HW) * sin

    def norm_rope(t, gain_row):
        hi, lo = _split_bf16(t * t)
        ss = _dot(hi, head_ones) + _dot(lo, head_ones)
        yn = t * lax.rsqrt(ss * (1.0 / HEAD_DIM) + EPS) * gain_ref[gain_row:gain_row + 1, :]
        return rope(yn)

    qa_ref[...] = (norm_rope(seg("qa"), 0) * Q_SCALE).astype(BF16)
    ka = norm_rope(seg("ka"), 1)
    ka_ref[...] = ka.astype(BF16)
    kmean_ref[0] = jnp.mean(ka, axis=0, keepdims=True)
    va_ref[...] = seg("va").astype(BF16)
    qb_ref[...] = (norm_rope(seg("qb"), 2) * Q_SCALE).astype(BF16)
    kb_ref[...] = norm_rope(seg("kb"), 3).astype(BF16)
    vb_ref[...] = seg("vb").astype(BF16)
    for half in range(2):
        qi_ref[:, half * HW:(half + 1) * HW] = rope(seg("qi", half * HW, HW)).astype(BF16)

    t = seg("kw")
    lane = lax.broadcasted_iota(jnp.int32, t.shape, 1)
    is_k = lane < IDX_DIM
    kms = jnp.sum(jnp.where(is_k, t * t, 0.0), axis=-1, keepdims=True) * (1.0 / IDX_DIM)
    kn = t * lax.rsqrt(kms + EPS) * kgain_ref[...]
    kr = kn * cos[:, :LANES] + _swap_halves(kn, LANES) * sin[:, :LANES]
    ki_ref[...] = kr[:, :IDX_DIM].astype(BF16)
    w_scale = (IDX_HEADS ** -0.5) * (IDX_DIM ** -0.5)
    wi_ref[...] = jnp.where(lane < IDX_HEADS, pltpu.roll(t, LANES - IDX_DIM, 1) * w_scale, 0.0)

    q1_ref[...] = (norm_rope(seg("q1"), 4) * Q_SCALE).astype(BF16)
    q2_ref[...] = (norm_rope(seg("q2"), 5) * Q_SCALE).astype(BF16)
    k1_ref[...] = norm_rope(seg("k1"), 6).astype(BF16)
    k2_ref[...] = norm_rope(seg("k2"), 7).astype(BF16)
    for half in range(2):
        vc_ref[:, half * HW:(half + 1) * HW] = seg("vc", half * HW, HW).astype(BF16)
    for gi, name in enumerate(("ga", "gb", "gc")):
        for part in range(D_MODEL // 512):
            g = seg(name, part * 512, 512)
            lo = gi * D_MODEL + part * 512
            sg_ref[:, lo:lo + 512] = (1.0 / (1.0 + jnp.exp(-g))).astype(BF16)


def _project(x2, norm_g, w_pad, cos_t, sin_t, gains, kgain, seq):
    n = x2.shape[0]
    tm = ROW_TILE
    n_pos = seq // tm
    row = lambda w: pl.BlockSpec((tm, w), lambda i: (i, 0))
    const = lambda shape: pl.BlockSpec(shape, lambda i: (0,) * len(shape))
    out_widths = [HW] * 6 + [IDX_HEADS * IDX_DIM, IDX_DIM, LANES] + [HW] * 4 + [N_HEADS * C_VDIM, 3 * D_MODEL]
    out_dtypes = [BF16] * 8 + [F32] + [BF16] * 6
    out_shape = [jax.ShapeDtypeStruct((n, w), dt) for w, dt in zip(out_widths, out_dtypes)]
    out_shape.append(jax.ShapeDtypeStruct((n // tm, 1, HW), F32))
    out_specs = [row(w) for w in out_widths] + [pl.BlockSpec((1, 1, HW), lambda i: (i, 0, 0))]
    return pl.pallas_call(
        _proj_kernel,
        grid=(n // tm,),
        in_specs=[row(D_MODEL), const((1, D_MODEL)),
                  pl.BlockSpec((D_MODEL, D_IN_PAD), lambda i: (0, 0), pipeline_mode=pl.Buffered(1)),
                  pl.BlockSpec((tm, HW), lambda i: (i % n_pos, 0)),
                  pl.BlockSpec((tm, HW), lambda i: (i % n_pos, 0)),
                  const((8, HW)), const((1, LANES))],
        out_specs=out_specs,
        out_shape=out_shape,
        compiler_params=_params(1),
        name="proj",
    )(x2, norm_g, w_pad, cos_t, sin_t, gains, kgain)


SUB_KEYS = 256


def _online_update(ss, ms, acc_ref, vts, col_oks=None):
    ps, out = [], []
    for c, s in enumerate(ss):
        m = ms[c]
        m_new = jnp.maximum(m, jnp.max(s, axis=0, keepdims=True))
        m_eff = jnp.maximum(m_new, M_FLOOR)
        if col_oks is not None:
            m_new = jnp.where(col_oks[c], m_new, m)
            m_eff = jnp.where(col_oks[c], m_eff, -NEG_BIG)
        out.append(m_new)
        ps.append((jnp.exp2(m - m_new), jnp.exp2(s - m_eff).astype(BF16)))
    for c, (alpha, p) in enumerate(ps):
        acc_ref[c] = alpha * acc_ref[c] + _dot(vts[c], p)
    return out


def _init_max(n_chains, tq):
    return tuple(jnp.full((1, tq), NEG_BIG, F32) for _ in range(n_chains))


def _normalized(acc_ref, c, dv):
    acc = acc_ref[c]
    return acc[:dv] / acc[dv:dv + 1]


def _head_slice(h, width=HEAD_DIM):
    return slice(h * width, (h + 1) * width)


def _moba_kernel(qt_ref, k_ref, vt_ref, kmean_ref, o_ref, acc_ref, *, n_sel):
    i = pl.program_id(1)
    blk = MOBA_BLOCK
    nb = kmean_ref.shape[1]
    n_part = blk // SUB_KEYS
    km = kmean_ref[0]
    brow = lax.broadcasted_iota(jnp.int32, (nb, blk), 0)
    browf = brow.astype(F32)
    causal = (lax.broadcasted_iota(jnp.int32, (blk, 1), 0)
              <= lax.broadcasted_iota(jnp.int32, (1, blk), 1))
    qts = [qt_ref[0, _head_slice(h), :] for h in range(N_HEADS)]
    sels = []
    for h in range(N_HEADS):
        km_hi, km_lo = _split_bf16(km[:, _head_slice(h)])
        gate = _dot(km_hi, qts[h]) + _dot(km_lo, qts[h])
        gate = jnp.where(brow < i, gate, -jnp.inf)
        sel = jnp.zeros((nb, blk), F32)
        for _ in range(n_sel):
            gm = jnp.max(gate, axis=0, keepdims=True)
            is_m = (gate == gm) & (gm > -jnp.inf)
            first = jnp.min(jnp.where(is_m, browf, float(nb)), axis=0, keepdims=True)
            pick = browf == first
            sel = jnp.where(pick, 1.0, sel)
            gate = jnp.where(pick, -jnp.inf, gate)
        sels.append(sel)
    acc_ref[...] = jnp.zeros(acc_ref.shape, F32)

    def scores(j, part):
        rows = pl.ds(pl.multiple_of(j * blk + part * SUB_KEYS, SUB_KEYS), SUB_KEYS)
        return [_dot(k_ref[rows, _head_slice(h)], qts[h]) for h in range(N_HEADS)]

    def values(j, part):
        cols = slice(part * SUB_KEYS, (part + 1) * SUB_KEYS)
        return [vt_ref[j, _head_slice(h, HEAD_DIM + V_PAD), cols] for h in range(N_HEADS)]

    def body(j, stats):
        oks = [jnp.sum(jnp.where(brow == j, sels[h], 0.0), axis=0, keepdims=True) > 0.0
               for h in range(N_HEADS)]
        for part in range(n_part):
            stats = _online_update(scores(j, part), stats, acc_ref, values(j, part), oks)
        return tuple(stats)

    stats = lax.fori_loop(0, i, body, _init_max(N_HEADS, blk))
    for part in range(n_part):
        cmask = causal[part * SUB_KEYS:(part + 1) * SUB_KEYS, :]
        ss = [jnp.where(cmask, s, NEG_BIG) for s in scores(i, part)]
        stats = _online_update(ss, stats, acc_ref, values(i, part))
    for h in range(N_HEADS):
        o_ref[0, _head_slice(h), :] = _normalized(acc_ref, h, HEAD_DIM).astype(o_ref.dtype)


def _moba(qt, k, vt, kmean, batch, seq):
    blk = MOBA_BLOCK
    nb = seq // blk
    n_sel = min(MOBA_TOPK, nb - 1)
    tspec = pl.BlockSpec((1, HW, blk), lambda b, i: (b, 0, i))
    return pl.pallas_call(
        functools.partial(_moba_kernel, n_sel=n_sel),
        grid=(batch, nb),
        in_specs=[tspec,
                  pl.BlockSpec((seq, HW), lambda b, i: (b, 0)),
                  pl.BlockSpec((nb, vt.shape[1], blk), lambda b, i: (b, 0, 0)),
                  pl.BlockSpec((1, nb, HW), lambda b, i: (b, 0, 0))],
        out_specs=tspec,
        out_shape=jax.ShapeDtypeStruct((batch, HW, seq), BF16),
        scratch_shapes=[pltpu.VMEM((N_HEADS, HEAD_DIM + V_PAD, blk), F32)],
        compiler_params=_params(2),
        name="moba",
    )(qt, k, vt, kmean)


def _dsa_kernel(qit_ref, wi_ref, ki_ref, qt_ref, k_ref, vt_ref, o_ref, key_ref, acc_ref, *, n_keep,
                idx_bits):
    i = pl.program_id(1)
    blk = ROW_TILE
    n_chunk = i + 1
    n_part = blk // SUB_KEYS
    w_t = wi_ref[...].T
    qpos = i * blk + lax.broadcasted_iota(jnp.int32, (1, blk), 1)
    krow = lax.broadcasted_iota(jnp.int32, (blk, 1), 0)

    def score_chunk(c, carry):
        kc = ki_ref[pl.ds(pl.multiple_of(c * blk, blk), blk), :]
        lgs = [_dot(kc, qit_ref[0, _head_slice(h, IDX_DIM), :]) for h in range(IDX_HEADS)]
        sc = jnp.zeros((blk, blk), F32)
        for h in range(IDX_HEADS):
            sc = sc + w_t[h:h + 1, :] * jnp.maximum(lgs[h], 0.0)
        sc = sc + 0.0
        bits = pltpu.bitcast(sc, jnp.int32)
        key = jnp.where(bits < 0, bits ^ 0x7FFFFFFF, bits)
        key_ref[c] = jnp.where(c * blk + krow <= qpos, key, INT_MIN)
        return carry

    lax.fori_loop(0, n_chunk, score_chunk, 0)

    @pl.when(n_chunk % 2 == 1)
    def _():
        key_ref[n_chunk] = jnp.full((blk, blk), INT_MIN, jnp.int32)

    def count(pred):
        def body(c2, acc):
            for c in (2 * c2, 2 * c2 + 1):
                hit = jnp.where(pred(c, key_ref[c]), 1.0, 0.0)
                acc = acc + jnp.sum(hit.reshape(blk // 8, 8, blk), axis=0)
            return acc
        acc = lax.fori_loop(0, (n_chunk + 1) // 2, body, jnp.zeros((8, blk), F32))
        return jnp.sum(acc, axis=0, keepdims=True)

    keep = float(n_keep)
    thr = jnp.full((1, blk), INT_MIN, jnp.int32)
    zero = jnp.zeros((1, blk), jnp.int32)
    thr = jnp.where(count(lambda c, k: k >= zero) >= keep, zero, thr)

    def bit_step(t, thr):
        cand = thr + jnp.left_shift(jnp.int32(1), 30 - t)
        return jnp.where(count(lambda c, k: k >= cand) >= keep, cand, thr)

    thr = lax.fori_loop(0, 31, bit_step, thr)

    tied = (count(lambda c, k: k >= thr) > keep) & (thr > INT_MIN)

    @pl.when(jnp.max(jnp.where(tied, 1.0, 0.0)) > 0.0)
    def _():
        need = keep - count(lambda c, k: k > thr)

        def idx_step(t, last):
            cand = last + jnp.left_shift(jnp.int32(1), idx_bits - 1 - t)
            n = count(lambda c, k: (k == thr) & (c * blk + krow < cand))
            return jnp.where(n < need, cand, last)

        last = lax.fori_loop(0, idx_bits, idx_step, zero)

        def demote(c, carry):
            k = key_ref[c]
            hit = (k == thr) & (c * blk + krow > last) & tied
            key_ref[c] = jnp.where(hit, thr - 1, k)
            return carry

        lax.fori_loop(0, n_chunk, demote, 0)

    thr_eff = jnp.maximum(thr, INT_MIN + 1)
    qts = [qt_ref[0, _head_slice(h), :] for h in range(N_HEADS)]
    acc_ref[...] = jnp.zeros(acc_ref.shape, F32)

    def body(c, stats):
        for part in range(n_part):
            lo = part * SUB_KEYS
            rows = pl.ds(pl.multiple_of(c * blk + lo, SUB_KEYS), SUB_KEYS)
            allowed = key_ref[c, lo:lo + SUB_KEYS, :] >= thr_eff
            ss = [jnp.where(allowed, _dot(k_ref[rows, _head_slice(h)], qts[h]), NEG_BIG)
                  for h in range(N_HEADS)]
            vts = [vt_ref[c, _head_slice(h, HEAD_DIM + V_PAD), lo:lo + SUB_KEYS] for h in range(N_HEADS)]
            stats = _online_update(ss, stats, acc_ref, vts)
        return tuple(stats)

    lax.fori_loop(0, n_chunk, body, _init_max(N_HEADS, blk))
    for h in range(N_HEADS):
        o_ref[0, _head_slice(h), :] = _normalized(acc_ref, h, HEAD_DIM).astype(o_ref.dtype)


def _dsa(qit, wi, ki, qt, k, vt, batch, seq):
    blk = ROW_TILE
    nq = seq // blk
    n_keep = min(DSA_TOPK_MAX, seq // 4)
    tspec = lambda w: pl.BlockSpec((1, w, blk), lambda b, i: (b, 0, i))
    full = lambda w: pl.BlockSpec((seq, w), lambda b, i: (b, 0))
    return pl.pallas_call(
        functools.partial(_dsa_kernel, n_keep=n_keep, idx_bits=(seq - 1).bit_length()),
        grid=(batch, nq),
        in_specs=[tspec(IDX_HEADS * IDX_DIM), pl.BlockSpec((blk, LANES), lambda b, i: (b * nq + i, 0)),
                  full(IDX_DIM), tspec(HW), full(HW),
                  pl.BlockSpec((nq, vt.shape[1], blk), lambda b, i: (b, 0, 0))],
        out_specs=tspec(HW),
        out_shape=jax.ShapeDtypeStruct((batch, HW, seq), BF16),
        scratch_shapes=[pltpu.VMEM((nq + nq % 2, blk, blk), jnp.int32),
                        pltpu.VMEM((N_HEADS, HEAD_DIM + V_PAD, blk), F32)],
        compiler_params=_params(2),
        name="dsa",
    )(qit, wi, ki, qt, k, vt)


DIFF_GROUP = 2


def _diff_kernel(q1t_ref, q2t_ref, k1_ref, k2_ref, vt_ref, dl_ref, gain_ref, o_ref, acc_ref, *, lam_init):
    i = pl.program_id(1)
    blk = ROW_TILE
    n_part = blk // SUB_KEYS
    dl = dl_ref[...]
    lam = (jnp.exp(jnp.sum(dl[0:1] * dl[1:2], axis=-1, keepdims=True))
           - jnp.exp(jnp.sum(dl[2:3] * dl[3:4], axis=-1, keepdims=True)) + lam_init)
    causal = (lax.broadcasted_iota(jnp.int32, (blk, 1), 0)
              <= lax.broadcasted_iota(jnp.int32, (1, blk), 1))
    maps = ((q1t_ref, k1_ref), (q2t_ref, k2_ref))

    for h0 in range(0, N_HEADS, DIFF_GROUP):
        chains = [(h, mp) for h in range(h0, h0 + DIFF_GROUP) for mp in range(2)]
        qts = [maps[mp][0][0, _head_slice(h), :] for h, mp in chains]
        acc_ref[...] = jnp.zeros(acc_ref.shape, F32)

        def step(j, stats, diag):
            for part in range(n_part):
                lo = part * SUB_KEYS
                rows = pl.ds(pl.multiple_of(j * blk + lo, SUB_KEYS), SUB_KEYS)
                ss = [_dot(maps[mp][1][rows, _head_slice(h)], qts[c]) for c, (h, mp) in enumerate(chains)]
                if diag:
                    ss = [jnp.where(causal[lo:lo + SUB_KEYS, :], s, NEG_BIG) for s in ss]
                vts = [vt_ref[j, _head_slice(h, C_VDIM + V_PAD), lo:lo + SUB_KEYS] for h, _ in chains]
                stats = _online_update(ss, stats, acc_ref, vts)
            return tuple(stats)

        stats = lax.fori_loop(0, i, lambda j, st: step(j, st, False), _init_max(len(chains), blk))
        step(i, stats, True)
        for g in range(DIFF_GROUP):
            h = h0 + g
            o = _normalized(acc_ref, 2 * g, C_VDIM) - lam * _normalized(acc_ref, 2 * g + 1, C_VDIM)
            ms = jnp.mean(o * o, axis=0, keepdims=True)
            o = o * lax.rsqrt(ms + EPS) * gain_ref[...] * (1.0 - lam_init)
            o_ref[0, _head_slice(h, C_VDIM), :] = o.astype(o_ref.dtype)


def _diff(q1t, q2t, k1, k2, vt, dl, gain, lam_init, batch, seq):
    blk = ROW_TILE
    nq = seq // blk
    tspec = lambda w: pl.BlockSpec((1, w, blk), lambda b, i: (b, 0, i))
    kspec = pl.BlockSpec((seq, HW), lambda b, i: (b, 0))
    vw = N_HEADS * C_VDIM
    return pl.pallas_call(
        functools.partial(_diff_kernel, lam_init=lam_init),
        grid=(batch, nq),
        in_specs=[tspec(HW), tspec(HW), kspec, kspec,
                  pl.BlockSpec((nq, vt.shape[1], blk), lambda b, i: (b, 0, 0)),
                  pl.BlockSpec((4, HEAD_DIM), lambda b, i: (0, 0)),
                  pl.BlockSpec((C_VDIM, 1), lambda b, i: (0, 0))],
        out_specs=tspec(vw),
        out_shape=jax.ShapeDtypeStruct((batch, vw, seq), BF16),
        scratch_shapes=[pltpu.VMEM((2 * DIFF_GROUP, C_VDIM + V_PAD, blk), F32)],
        compiler_params=_params(2),
        name="diff",
    )(q1t, q2t, k1, k2, vt, dl, gain)


def _merge_kernel(x_ref, oa_ref, ob_ref, oc_ref, sg_ref, wa_ref, wb_ref, wc_ref, wo_ref, g_ref,
                  wr_hi_ref, wr_lo_ref, br_ref, x1_ref, hn_ref, route_ref):
    merged = (sg_ref[:, 0:D_MODEL].astype(F32) * _dot(oa_ref[...], wa_ref[...])
              + sg_ref[:, D_MODEL:2 * D_MODEL].astype(F32) * _dot(ob_ref[...], wb_ref[...])
              + sg_ref[:, 2 * D_MODEL:3 * D_MODEL].astype(F32) * _dot(oc_ref[...], wc_ref[...]))
    x1 = x_ref[...] + _dot(merged.astype(BF16), wo_ref[...])
    x1_ref[...] = x1
    ms = jnp.mean(x1 * x1, axis=-1, keepdims=True)
    hn = x1 * lax.rsqrt(ms + EPS) * g_ref[...]
    hn_ref[...] = hn

    hi, lo = _split_bf16(hn)
    lg = (_dot(hi, wr_hi_ref[...]) + _dot(lo, wr_hi_ref[...]) + _dot(hi, wr_lo_ref[...])
          + br_ref[...])
    lane = lax.broadcasted_iota(jnp.int32, lg.shape, 1)
    lanef = lane.astype(F32)
    far = float(LANES)
    is_g = (lane >= N_EXPERTS) & (lane < N_EXPERTS + N_GROUPS)
    gl = jnp.where(is_g, lg, -jnp.inf)
    gmax = jnp.max(gl, axis=-1, keepdims=True)
    gidx = jnp.min(jnp.where(gl == gmax, lanef, far), axis=-1, keepdims=True) - float(N_EXPERTS)
    g_w = 1.0 / jnp.sum(jnp.where(is_g, jnp.exp(gl - gmax), 0.0), axis=-1, keepdims=True)
    in_group = (lane < N_EXPERTS) & ((lane // EXPERTS_PER_GROUP).astype(F32) == gidx)
    el = jnp.where(in_group, lg, -jnp.inf)
    e1 = jnp.max(el, axis=-1, keepdims=True)
    i1 = jnp.min(jnp.where(el == e1, lanef, far), axis=-1, keepdims=True)
    el2 = jnp.where(lanef == i1, -jnp.inf, el)
    e2 = jnp.max(el2, axis=-1, keepdims=True)
    i2 = jnp.min(jnp.where(el2 == e2, lanef, far), axis=-1, keepdims=True)
    t = jnp.exp(e2 - e1)
    w1 = g_w / (1.0 + t)
    w2 = g_w * t / (1.0 + t)
    route_ref[...] = jnp.where(lane == 0, i1, jnp.where(lane == 1, i2, jnp.where(lane == 2, w1, jnp.where(lane == 3, w2, 0.0))))


def _merge(x2, oa, ob, oc, sg, wa, wb, wc, wo, norm_g, wr_hi, wr_lo, br):
    n = x2.shape[0]
    tm = ROW_TILE
    row = lambda w: pl.BlockSpec((tm, w), lambda i: (i, 0))
    const = lambda a: pl.BlockSpec(a.shape, lambda i: (0, 0))
    return pl.pallas_call(
        _merge_kernel,
        grid=(n // tm,),
        in_specs=[row(D_MODEL), row(HW), row(HW), row(N_HEADS * C_VDIM), row(3 * D_MODEL),
                  const(wa), const(wb), const(wc), const(wo), const(norm_g), const(wr_hi),
                  const(wr_lo), const(br)],
        out_specs=[row(D_MODEL), row(D_MODEL), row(LANES)],
        out_shape=[jax.ShapeDtypeStruct((n, D_MODEL), F32), jax.ShapeDtypeStruct((n, D_MODEL), F32),
                   jax.ShapeDtypeStruct((n, LANES), F32)],
        compiler_params=_params(1),
        name="merge",
    )(x2, oa, ob, oc, sg, wa, wb, wc, wo, norm_g, wr_hi, wr_lo, br)


MOE_BLOCK = 256
META_ROWS = 8
META_USED = 3 * LANES
META_END = 4 * LANES
META_PADDED = 5 * LANES


def _lane_prefix_sum(x):
    lane = lax.broadcasted_iota(jnp.int32, x.shape, 1)
    shift = 1
    while shift < LANES:
        x = x + jnp.where(lane >= shift, pltpu.roll(x, shift, 1), 0.0)
        shift *= 2
    return x


def _positions_kernel(route_ref, dest_ref, meta_ref, cnt_ref, base_ref):
    phase = pl.program_id(0)
    t = pl.program_id(1)
    tm = route_ref.shape[0]
    route = route_ref[...]
    lane = lax.broadcasted_iota(jnp.int32, route.shape, 1)
    lanef = lane.astype(F32)
    e1 = route[:, 0:1]
    e2 = route[:, 1:2]
    uses = jnp.where((lanef == e1) | (lanef == e2), 1.0, 0.0)
    tile_cnt = jnp.sum(uses, axis=0, keepdims=True)

    @pl.when((phase == 0) & (t == 0))
    def _():
        cnt_ref[...] = jnp.zeros(cnt_ref.shape, F32)

    @pl.when(phase == 0)
    def _():
        cnt_ref[...] += tile_cnt

    @pl.when((phase == 1) & (t == 0))
    def _():
        cnt = cnt_ref[...]
        padded = jnp.floor((cnt + (MOE_BLOCK - 1)) * (1.0 / MOE_BLOCK)) * MOE_BLOCK
        end = _lane_prefix_sum(padded)
        base_ref[...] = end - padded
        cnt_ref[...] = jnp.zeros(cnt_ref.shape, F32)
        lane1 = lax.broadcasted_iota(jnp.int32, (1, LANES), 1)
        row = lax.broadcasted_iota(jnp.int32, (META_ROWS, LANES), 0)
        col = lax.broadcasted_iota(jnp.int32, (META_ROWS, LANES), 1)
        first_row = ((row * LANES + col) * MOE_BLOCK).astype(F32)
        owner = jnp.zeros((META_ROWS, LANES), F32)
        for e in range(N_EXPERTS):
            end_e = jnp.sum(jnp.where(lane1 == e, end, 0.0), axis=-1, keepdims=True)
            owner = owner + jnp.where(end_e <= first_row, 1.0, 0.0)
        owner = jnp.minimum(owner, float(N_EXPERTS - 1))
        used = jnp.sum(jnp.where(lane1 == N_EXPERTS - 1, end, 0.0), axis=-1, keepdims=True) * (1.0 / MOE_BLOCK)
        meta = jnp.where(row == META_USED // LANES, used,
                         jnp.where(row == META_END // LANES, end,
                                   jnp.where(row == META_PADDED // LANES, padded, owner)))
        meta_ref[...] = meta.astype(jnp.int32)

    @pl.when(phase == 1)
    def _():
        before = (lax.broadcasted_iota(jnp.int32, (tm, tm), 1)
                  < lax.broadcasted_iota(jnp.int32, (tm, tm), 0)).astype(BF16)
        rank = _dot(before, uses.astype(BF16))
        pos = base_ref[...] + cnt_ref[...] + rank
        d1 = jnp.sum(jnp.where(lanef == e1, pos, 0.0), axis=-1, keepdims=True)
        d2 = jnp.sum(jnp.where(lanef == e2, pos, 0.0), axis=-1, keepdims=True)
        dest_ref[...] = jnp.where(lane == 0, d1, jnp.where(lane == 1, d2, 0.0)).astype(jnp.int32)
        cnt_ref[...] += tile_cnt


def _positions(route):
    n = route.shape[0]
    tm = min(ROW_TILE, n)
    return pl.pallas_call(
        _positions_kernel,
        grid=(2, n // tm),
        in_specs=[pl.BlockSpec((tm, LANES), lambda p, t: (t, 0))],
        out_specs=[pl.BlockSpec((tm, LANES), lambda p, t: (t * p, 0)),
                   pl.BlockSpec((META_ROWS, LANES), lambda p, t: (0, 0))],
        out_shape=[jax.ShapeDtypeStruct((n, LANES), jnp.int32),
                   jax.ShapeDtypeStruct((META_ROWS, LANES), jnp.int32)],
        scratch_shapes=[pltpu.VMEM((1, LANES), F32), pltpu.VMEM((1, LANES), F32)],
        compiler_params=_params(2),
        name="moe_positions",
    )(route)


def _row_copy(src_ref, src_row, dst_ref, dst_row, sem):
    return pltpu.make_async_copy(src_ref.at[pl.ds(src_row, 1), :], dst_ref.at[pl.ds(dst_row, 1), :], sem)


def _dispatch_kernel(meta_ref, dest_ref, hn_ref, xs_ref, zero_ref, sem):
    tm = hn_ref.shape[0]

    @pl.when(pl.program_id(0) == 0)
    def _():
        zero_ref[...] = jnp.zeros(zero_ref.shape, F32)

        def fill(e):
            end = pl.multiple_of(meta_ref[META_END + e], MOE_BLOCK)
            return pltpu.make_async_copy(zero_ref, xs_ref.at[pl.ds(end - MOE_BLOCK, MOE_BLOCK), :], sem)

        for e in range(N_EXPERTS):
            @pl.when(meta_ref[META_PADDED + e] > 0)
            def _():
                fill(e).start()
        for e in range(N_EXPERTS):
            @pl.when(meta_ref[META_PADDED + e] > 0)
            def _():
                fill(e).wait()

        def spare(b):
            return pltpu.make_async_copy(
                zero_ref, xs_ref.at[pl.ds(pl.multiple_of(b * MOE_BLOCK, MOE_BLOCK), MOE_BLOCK), :], sem)

        n_blocks = xs_ref.shape[0] // MOE_BLOCK
        lax.fori_loop(meta_ref[META_USED], n_blocks, lambda b, c: (spare(b).start(), c)[1], 0)
        lax.fori_loop(meta_ref[META_USED], n_blocks, lambda b, c: (spare(b).wait(), c)[1], 0)

    def copies(r):
        return (_row_copy(hn_ref, r, xs_ref, dest_ref[0, 0, 2 * r], sem),
                _row_copy(hn_ref, r, xs_ref, dest_ref[0, 0, 2 * r + 1], sem))

    def start(r, carry):
        for cp in copies(r):
            cp.start()
        return carry

    def wait(r, carry):
        for cp in copies(r):
            cp.wait()
        return carry

    lax.fori_loop(0, tm, start, 0)
    lax.fori_loop(0, tm, wait, 0)


def _dispatch(meta, dest3, hn, n_rows):
    n = hn.shape[0]
    tm = dest3.shape[2] // 2
    return pl.pallas_call(
        _dispatch_kernel,
        grid_spec=pltpu.PrefetchScalarGridSpec(
            num_scalar_prefetch=1,
            grid=(n // tm,),
            in_specs=[pl.BlockSpec((1, 1, 2 * tm), lambda t, m: (t, 0, 0), memory_space=pltpu.SMEM),
                      pl.BlockSpec((tm, D_MODEL), lambda t, m: (t, 0))],
            out_specs=pl.BlockSpec(memory_space=pl.ANY),
            scratch_shapes=[pltpu.VMEM((MOE_BLOCK, D_MODEL), F32), pltpu.SemaphoreType.DMA(())]),
        out_shape=jax.ShapeDtypeStruct((n_rows, D_MODEL), F32),
        compiler_params=_params(1),
        name="moe_dispatch",
    )(meta, dest3, hn)


def _expert_kernel(meta_ref, xs_ref, wg_ref, wu_ref, wd_ref, y_ref):
    holds_rows = pl.program_id(0) < meta_ref[META_USED]

    @pl.when(holds_rows)
    def _():
        x = xs_ref[...].astype(BF16)
        g = _dot(x, wg_ref[0])
        u = _dot(x, wu_ref[0])
        hid = g * (1.0 / (1.0 + jnp.exp(-g))) * u
        y_ref[...] = _dot(hid.astype(BF16), wd_ref[0])

    @pl.when(jnp.logical_not(holds_rows))
    def _():
        y_ref[...] = jnp.zeros(y_ref.shape, F32)


def _experts(meta, xs, wg, wu, wd):
    n_blocks = xs.shape[0] // MOE_BLOCK
    rows = pl.BlockSpec((MOE_BLOCK, D_MODEL), lambda b, m: (b, 0))
    weight = lambda shape: pl.BlockSpec(
        (1,) + shape, lambda b, m: (m[jnp.minimum(b, m[META_USED] - 1)], 0, 0))
    return pl.pallas_call(
        _expert_kernel,
        grid_spec=pltpu.PrefetchScalarGridSpec(
            num_scalar_prefetch=1,
            grid=(n_blocks,),
            in_specs=[rows, weight((D_MODEL, D_EXPERT)), weight((D_MODEL, D_EXPERT)),
                      weight((D_EXPERT, D_MODEL))],
            out_specs=rows),
        out_shape=jax.ShapeDtypeStruct(xs.shape, F32),
        compiler_params=_params(1),
        name="moe_experts",
    )(meta, xs, wg, wu, wd)


def _combine_kernel(dest_ref, x1_ref, route_ref, y_ref, o_ref, buf_ref, sem):
    tm = x1_ref.shape[0]

    def copies(r):
        return (_row_copy(y_ref, dest_ref[0, 0, 2 * r], buf_ref.at[0], r, sem),
                _row_copy(y_ref, dest_ref[0, 0, 2 * r + 1], buf_ref.at[1], r, sem))

    def start(r, carry):
        for cp in copies(r):
            cp.start()
        return carry

    def wait(r, carry):
        for cp in copies(r):
            cp.wait()
        return carry

    lax.fori_loop(0, tm, start, 0)
    lax.fori_loop(0, tm, wait, 0)
    route = route_ref[...]
    o_ref[...] = x1_ref[...] + route[:, 2:3] * buf_ref[0] + route[:, 3:4] * buf_ref[1]


def _combine(dest3, x1, route, y):
    n = x1.shape[0]
    tm = dest3.shape[2] // 2
    row = lambda w: pl.BlockSpec((tm, w), lambda t: (t, 0))
    return pl.pallas_call(
        _combine_kernel,
        grid=(n // tm,),
        in_specs=[pl.BlockSpec((1, 1, 2 * tm), lambda t: (t, 0, 0), memory_space=pltpu.SMEM),
                  row(D_MODEL), row(LANES), pl.BlockSpec(memory_space=pl.ANY)],
        out_specs=row(D_MODEL),
        out_shape=jax.ShapeDtypeStruct((n, D_MODEL), F32),
        scratch_shapes=[pltpu.VMEM((2, tm, D_MODEL), F32), pltpu.SemaphoreType.DMA(())],
        compiler_params=_params(1),
        name="moe_combine",
    )(dest3, x1, route, y)


def _moe(x1, hn, route, wg, wu, wd):
    n = x1.shape[0]
    tm = min(ROW_TILE, n)
    n_blocks = -(-(2 * n + N_EXPERTS * (MOE_BLOCK - 1)) // MOE_BLOCK)
    assert n_blocks <= META_USED
    dest, meta = _positions(route)
    meta = meta.reshape(-1)
    dest3 = dest[:, :2].reshape(n // tm, 1, 2 * tm)
    xs = _dispatch(meta, dest3, hn, n_blocks * MOE_BLOCK)
    y = _experts(meta, xs, wg, wu, wd)
    return _combine(dest3, x1, route, y)


def _rope_tables(seq):
    inv_freq = 1.0 / (ROPE_THETA ** (jnp.arange(0, HEAD_DIM, 2, dtype=F32) / HEAD_DIM))
    ang = jnp.arange(seq, dtype=F32)[:, None] * inv_freq[None, :]
    cos, sin = jnp.cos(ang), jnp.sin(ang)
    cos_t = jnp.tile(jnp.concatenate([cos, cos], axis=-1), (1, N_HEADS))
    sin_t = jnp.tile(jnp.concatenate([-sin, sin], axis=-1), (1, N_HEADS))
    return cos_t, sin_t


def _to_t(v, batch, n_chunk):
    tokens = v.shape[0]
    v = v.reshape(tokens, N_HEADS, -1)
    pad = jnp.zeros((tokens, N_HEADS, V_PAD), v.dtype).at[:, :, 0].set(1)
    v = jnp.concatenate([v, pad], axis=2).reshape(tokens, -1)
    feat = v.shape[1]
    return (v.reshape(batch, n_chunk, ROW_TILE, feat).transpose(0, 1, 3, 2)
            .reshape(batch * n_chunk, feat, ROW_TILE))


def _from_t(o_t):
    b, feat, seq = o_t.shape
    return o_t.transpose(0, 2, 1).reshape(b * seq, feat)


def kernel(x, norm_attn, w_in, qk_gain, idx_k_gain, diff_lambda, diff_subln_gain, w_proj_a, w_proj_b, w_proj_c, w_out, norm_ffn, w_group, b_group, w_router, b_router, w_e_gate, w_e_up, w_e_down):
    batch, seq, d = x.shape
    assert d == D_MODEL and seq % ROW_TILE == 0 and ROW_TILE == MOBA_BLOCK
    n = batch * seq
    nq = seq // ROW_TILE
    depth = w_in.shape[0]
    cos_t, sin_t = _rope_tables(seq)
    x2 = x.reshape(n, d)
    for l in range(depth):
        w_pad = jnp.concatenate(
            [w_in[l][:, :KW_SRC], jnp.zeros((d, LANES - IDX_DIM - IDX_HEADS), F32), w_in[l][:, KW_SRC:]],
            axis=1).astype(BF16)
        gains = jnp.tile(qk_gain[l][jnp.array([0, 1, 2, 3, 4, 4, 5, 5])], (1, N_HEADS))
        kgain = jnp.pad(idx_k_gain[l], (0, LANES - IDX_DIM))[None, :]
        (qa, ka, va, qb, kb, vb, qi, ki, wi, q1, q2, k1, k2, vc, sg, kmean) = _project(
            x2, norm_attn[l][None, :], w_pad, cos_t, sin_t, gains, kgain, seq)

        feat_major = lambda t: t.reshape(batch, seq, t.shape[1]).transpose(0, 2, 1)
        o_a = _from_t(_moba(feat_major(qa), ka, _to_t(va, batch, nq), kmean.reshape(batch, nq, HW),
                            batch, seq))
        o_b = _from_t(_dsa(feat_major(qi), wi, ki, feat_major(qb), kb, _to_t(vb, batch, nq), batch, seq))
        lam_init = 0.8 - 0.6 * math.exp(-0.3 * l)
        o_c = _from_t(_diff(feat_major(q1), feat_major(q2), k1, k2, _to_t(vc, batch, nq), diff_lambda[l],
                            diff_subln_gain[l][:, None], lam_init, batch, seq))

        w_r = jnp.concatenate([w_router[l], w_group[l],
                               jnp.zeros((d, LANES - N_EXPERTS - N_GROUPS), F32)], axis=1)
        wr_hi = w_r.astype(BF16)
        wr_lo = (w_r - wr_hi.astype(F32)).astype(BF16)
        b_r = jnp.concatenate([b_router[l], b_group[l],
                               jnp.zeros((LANES - N_EXPERTS - N_GROUPS,), F32)])[None, :]
        x1, hn, route = _merge(x2, o_a, o_b, o_c, sg, w_proj_a[l].astype(BF16), w_proj_b[l].astype(BF16),
                            w_proj_c[l].astype(BF16), w_out[l].astype(BF16), norm_ffn[l][None, :],
                            wr_hi, wr_lo, b_r)
        x2 = _moe(x1, hn, route, w_e_gate[l].astype(BF16), w_e_up[l].astype(BF16),
                  w_e_down[l].astype(BF16))
    return x2.reshape(batch, seq, d)
```

```python
import functools
import math

import jax
import jax.numpy as jnp
from jax import lax
from jax.experimental import pallas as pl
from jax.experimental.pallas import tpu as pltpu

F32 = jnp.float32
BF16 = jnp.bfloat16

D_MODEL = 1024
HEAD_DIM = 64
ROPE_THETA = 10000.0
EPS = 1e-6
N_HEADS = 4
MOBA_BLOCK = 256
MOBA_TOPK = 3
IDX_HEADS = 8
IDX_DIM = 64
DSA_TOPK_MAX = 256
C_VDIM = 2 * HEAD_DIM
N_GROUPS = 4
EXPERTS_PER_GROUP = 8
N_EXPERTS = N_GROUPS * EXPERTS_PER_GROUP
D_EXPERT = 512

HW = N_HEADS * HEAD_DIM
LANES = 128
ROW_TILE = 256
DENSE_TILE = 512
VMEM_LIMIT = 56 * 1024 * 1024

_SEG = {}
_off = 0
for _name, _w in (("qa", HW), ("ka", HW), ("va", HW), ("qb", HW), ("kb", HW), ("vb", HW),
                  ("qi", IDX_HEADS * IDX_DIM), ("kw", LANES), ("q1", HW), ("q2", HW), ("k1", HW),
                  ("k2", HW), ("vc", N_HEADS * C_VDIM), ("ga", D_MODEL), ("gb", D_MODEL),
                  ("gc", D_MODEL)):
    _SEG[_name] = (_off, _w)
    _off += _w
D_IN_PAD = _off
KW_SRC = 6 * HW + IDX_HEADS * IDX_DIM + IDX_DIM + IDX_HEADS

NEG_BIG = -1e30
M_FLOOR = -1e20
INT_MIN = -(2 ** 31)
LOG2E = math.log2(math.e)
Q_SCALE = HEAD_DIM ** -0.5 * LOG2E
V_PAD = 16
NT_DIMS = (((1,), (1,)), ((), ()))


def _params(n_axes):
    return pltpu.CompilerParams(dimension_semantics=("arbitrary",) * n_axes,
                                vmem_limit_bytes=VMEM_LIMIT)


def _dot(a, b):
    return jnp.dot(a, b, preferred_element_type=F32)


def _dot_nt(a, b):
    return lax.dot_general(a, b, NT_DIMS, preferred_element_type=F32)


def _split_bf16(a):
    hi = a.astype(BF16)
    return hi, (a - hi.astype(F32)).astype(BF16)


def _swap_halves(y, width):
    lane = lax.broadcasted_iota(jnp.int32, y.shape, 1)
    first = (lane % HEAD_DIM) < (HEAD_DIM // 2)
    return jnp.where(first, pltpu.roll(y, width - HEAD_DIM // 2, 1), pltpu.roll(y, HEAD_DIM // 2, 1))


def _proj_kernel(x_ref, g_ref, w_ref, cos_ref, sin_ref, gain_ref, kgain_ref,
                 qa_ref, ka_ref, va_ref, qb_ref, kb_ref, vb_ref, qi_ref, ki_ref, wi_ref,
                 q1_ref, q2_ref, k1_ref, k2_ref, vc_ref, sg_ref, kmean_ref):
    x = x_ref[...]
    ms = jnp.mean(x * x, axis=-1, keepdims=True)
    h = (x * lax.rsqrt(ms + EPS) * g_ref[...]).astype(BF16)
    cos = cos_ref[...]
    sin = sin_ref[...]
    r = lax.broadcasted_iota(jnp.int32, (HW, HW), 0) // HEAD_DIM
    c = lax.broadcasted_iota(jnp.int32, (HW, HW), 1) // HEAD_DIM
    head_ones = (r == c).astype(BF16)

    def seg(name, lo=0, width=None):
        off, w = _SEG[name]
        width = w if width is None else width
        return _dot(h, w_ref[:, off + lo:off + lo + width])

    def rope(y):
        return y * cos + _swap_halves(y, HW) * sin

    def norm_rope(t, gain_row):
        hi, lo = _split_bf16(t * t)
        ss = _dot(hi, head_ones) + _dot(lo, head_ones)
        yn = t * lax.rsqrt(ss * (1.0 / HEAD_DIM) + EPS) * gain_ref[gain_row:gain_row + 1, :]
        return rope(yn)

    qa_ref[...] = (norm_rope(seg("qa"), 0) * Q_SCALE).astype(BF16)
    ka = norm_rope(seg("ka"), 1)
    ka_ref[...] = ka.astype(BF16)
    for blk in range(ka.shape[0] // MOBA_BLOCK):
        kmean_ref[blk] = jnp.mean(ka[blk * MOBA_BLOCK:(blk + 1) * MOBA_BLOCK], axis=0, keepdims=True)
    va_ref[...] = seg("va").astype(BF16)
    qb_ref[...] = (norm_rope(seg("qb"), 2) * Q_SCALE).astype(BF16)
    kb_ref[...] = norm_rope(seg("kb"), 3).astype(BF16)
    vb_ref[...] = seg("vb").astype(BF16)
    for half in range(2):
        qi_ref[:, half * HW:(half + 1) * HW] = rope(seg("qi", half * HW, HW)).astype(BF16)

    t = seg("kw")
    lane = lax.broadcasted_iota(jnp.int32, t.shape, 1)
    is_k = lane < IDX_DIM
    kms = jnp.sum(jnp.where(is_k, t * t, 0.0), axis=-1, keepdims=True) * (1.0 / IDX_DIM)
    kn = t * lax.rsqrt(kms + EPS) * kgain_ref[...]
    kr = kn * cos[:, :LANES] + _swap_halves(kn, LANES) * sin[:, :LANES]
    ki_ref[...] = kr[:, :IDX_DIM].astype(BF16)
    w_scale = (IDX_HEADS ** -0.5) * (IDX_DIM ** -0.5)
    wi_ref[...] = jnp.where(lane < IDX_HEADS, pltpu.roll(t, LANES - IDX_DIM, 1) * w_scale, 0.0)

    q1_ref[...] = (norm_rope(seg("q1"), 4) * Q_SCALE).astype(BF16)
    q2_ref[...] = (norm_rope(seg("q2"), 5) * Q_SCALE).astype(BF16)
    k1_ref[...] = norm_rope(seg("k1"), 6).astype(BF16)
    k2_ref[...] = norm_rope(seg("k2"), 7).astype(BF16)
    for half in range(2):
        vc_ref[:, half * HW:(half + 1) * HW] = seg("vc", half * HW, HW).astype(BF16)
    for gi, name in enumerate(("ga", "gb", "gc")):
        for part in range(D_MODEL // 512):
            g = seg(name, part * 512, 512)
            lo = gi * D_MODEL + part * 512
            sg_ref[:, lo:lo + 512] = (1.0 / (1.0 + jnp.exp(-g))).astype(BF16)


def _project(x2, norm_g, w_pad, cos_t, sin_t, gains, kgain, seq):
    n = x2.shape[0]
    tm = min(DENSE_TILE, seq)
    assert seq % tm == 0 and tm % MOBA_BLOCK == 0
    n_pos = seq // tm
    row = lambda w: pl.BlockSpec((tm, w), lambda i: (i, 0))
    const = lambda shape: pl.BlockSpec(shape, lambda i: (0,) * len(shape))
    out_widths = [HW] * 6 + [IDX_HEADS * IDX_DIM, IDX_DIM, LANES] + [HW] * 4 + [N_HEADS * C_VDIM, 3 * D_MODEL]
    out_dtypes = [BF16] * 8 + [F32] + [BF16] * 6
    out_shape = [jax.ShapeDtypeStruct((n, w), dt) for w, dt in zip(out_widths, out_dtypes)]
    out_shape.append(jax.ShapeDtypeStruct((n // MOBA_BLOCK, 1, HW), F32))
    out_specs = [row(w) for w in out_widths] + [pl.BlockSpec((tm // MOBA_BLOCK, 1, HW), lambda i: (i, 0, 0))]
    return pl.pallas_call(
        _proj_kernel,
        grid=(n // tm,),
        in_specs=[row(D_MODEL), const((1, D_MODEL)),
                  pl.BlockSpec((D_MODEL, D_IN_PAD), lambda i: (0, 0), pipeline_mode=pl.Buffered(1)),
                  pl.BlockSpec((tm, HW), lambda i: (i % n_pos, 0)),
                  pl.BlockSpec((tm, HW), lambda i: (i % n_pos, 0)),
                  const((8, HW)), const((1, LANES))],
        out_specs=out_specs,
        out_shape=out_shape,
        compiler_params=_params(1),
        name="proj",
    )(x2, norm_g, w_pad, cos_t, sin_t, gains, kgain)


def _online_update(parts, ms, acc_ref):
    ps, out = [], []
    for c, tiles in enumerate(parts):
        m_new = ms[c]
        for s, _, ok in tiles:
            smax = jnp.max(s, axis=0, keepdims=True)
            m_new = jnp.maximum(m_new, smax if ok is None else jnp.where(ok, smax, NEG_BIG))
        m_eff = jnp.maximum(m_new, M_FLOOR)
        out.append(m_new)
        probs = [jnp.exp2(s - (m_eff if ok is None else jnp.where(ok, m_eff, -NEG_BIG))).astype(BF16)
                 for s, _, ok in tiles]
        ps.append((jnp.exp2(ms[c] - m_new), probs))
    for c, (alpha, probs) in enumerate(ps):
        acc = alpha * acc_ref[c]
        for (_, vt, _), p in zip(parts[c], probs):
            acc = acc + _dot(vt, p)
        acc_ref[c] = acc
    return out


def _init_max(n_chains, tq):
    return tuple(jnp.full((1, tq), NEG_BIG, F32) for _ in range(n_chains))


def _normalized(acc_ref, c, dv):
    acc = acc_ref[c]
    return acc[:dv] / acc[dv:dv + 1]


def _head_slice(h, width=HEAD_DIM):
    return slice(h * width, (h + 1) * width)


def _moba_kernel(qt_ref, k_ref, vt_ref, kmean_ref, o_ref, acc_ref, *, n_sel):
    i = pl.program_id(1)
    blk = MOBA_BLOCK
    nb = kmean_ref.shape[1]
    km = kmean_ref[0]
    brow = lax.broadcasted_iota(jnp.int32, (nb, blk), 0)
    browf = brow.astype(F32)
    causal = (lax.broadcasted_iota(jnp.int32, (blk, 1), 0)
              <= lax.broadcasted_iota(jnp.int32, (1, blk), 1))
    qts = [qt_ref[0, _head_slice(h), :] for h in range(N_HEADS)]
    sels = []
    for h in range(N_HEADS):
        km_hi, km_lo = _split_bf16(km[:, _head_slice(h)])
        gate = _dot(km_hi, qts[h]) + _dot(km_lo, qts[h])
        gate = jnp.where(brow < i, gate, -jnp.inf)
        sel = jnp.zeros((nb, blk), F32)
        for _ in range(n_sel):
            gm = jnp.max(gate, axis=0, keepdims=True)
            is_m = (gate == gm) & (gm > -jnp.inf)
            first = jnp.min(jnp.where(is_m, browf, float(nb)), axis=0, keepdims=True)
            pick = browf == first
            sel = jnp.where(pick, 1.0, sel)
            gate = jnp.where(pick, -jnp.inf, gate)
        sels.append(sel)
    acc_ref[...] = jnp.zeros(acc_ref.shape, F32)

    def tile(j, h, mask=None, seen=False):
        rows = pl.ds(pl.multiple_of(j * blk, blk), blk)
        s = _dot(k_ref[rows, _head_slice(h)], qts[h])
        if mask is not None:
            s = jnp.where(mask, s, NEG_BIG)
        ok = jnp.sum(jnp.where(brow == j, sels[h], 0.0), axis=0, keepdims=True) > 0.0 if seen else None
        return s, vt_ref[j, _head_slice(h, HEAD_DIM + V_PAD), :], ok

    def pair(j2, ms):
        parts = [[tile(2 * j2, h, seen=True), tile(2 * j2 + 1, h, seen=True)] for h in range(N_HEADS)]
        return tuple(_online_update(parts, ms, acc_ref))

    ms = lax.fori_loop(0, i // 2, pair, _init_max(N_HEADS, blk))

    @pl.when(i % 2 == 1)
    def _():
        _online_update([[tile(i - 1, h, seen=True), tile(i, h, mask=causal)] for h in range(N_HEADS)],
                       ms, acc_ref)

    @pl.when(i % 2 == 0)
    def _():
        _online_update([[tile(i, h, mask=causal)] for h in range(N_HEADS)], ms, acc_ref)

    for h in range(N_HEADS):
        o_ref[0, _head_slice(h), :] = _normalized(acc_ref, h, HEAD_DIM).astype(o_ref.dtype)


def _moba(qt, k, vt, kmean, batch, seq):
    blk = MOBA_BLOCK
    nb = seq // blk
    n_sel = min(MOBA_TOPK, nb - 1)
    tspec = pl.BlockSpec((1, HW, blk), lambda b, i: (b, 0, i))
    return pl.pallas_call(
        functools.partial(_moba_kernel, n_sel=n_sel),
        grid=(batch, nb),
        in_specs=[tspec,
                  pl.BlockSpec((seq, HW), lambda b, i: (b, 0)),
                  pl.BlockSpec((nb, vt.shape[1], blk), lambda b, i: (b, 0, 0)),
                  pl.BlockSpec((1, nb, HW), lambda b, i: (b, 0, 0))],
        out_specs=tspec,
        out_shape=jax.ShapeDtypeStruct((batch, HW, seq), BF16),
        scratch_shapes=[pltpu.VMEM((N_HEADS, HEAD_DIM + V_PAD, blk), F32)],
        compiler_params=_params(2),
        name="moba",
    )(qt, k, vt, kmean)


def _dsa_kernel(qit_ref, wi_ref, ki_ref, qt_ref, k_ref, vt_ref, o_ref, key_ref, acc_ref, *, n_keep,
                idx_bits):
    i = pl.program_id(1)
    blk = ROW_TILE
    n_chunk = i + 1
    w_t = wi_ref[...].T
    qpos = i * blk + lax.broadcasted_iota(jnp.int32, (1, blk), 1)
    krow = lax.broadcasted_iota(jnp.int32, (blk, 1), 0)

    def score_chunk(c, carry):
        kc = ki_ref[pl.ds(pl.multiple_of(c * blk, blk), blk), :]
        lgs = [_dot(kc, qit_ref[0, _head_slice(h, IDX_DIM), :]) for h in range(IDX_HEADS)]
        sc = jnp.zeros((blk, blk), F32)
        for h in range(IDX_HEADS):
            sc = sc + w_t[h:h + 1, :] * jnp.maximum(lgs[h], 0.0)
        sc = sc + 0.0
        bits = pltpu.bitcast(sc, jnp.int32)
        key = jnp.where(bits < 0, bits ^ 0x7FFFFFFF, bits)
        key_ref[c] = jnp.where(c * blk + krow <= qpos, key, INT_MIN)
        return carry

    lax.fori_loop(0, n_chunk, score_chunk, 0)

    @pl.when(n_chunk % 2 == 1)
    def _():
        key_ref[n_chunk] = jnp.full((blk, blk), INT_MIN, jnp.int32)

    def count(pred):
        def body(c2, acc):
            for c in (2 * c2, 2 * c2 + 1):
                hit = jnp.where(pred(c, key_ref[c]), 1.0, 0.0)
                acc = acc + jnp.sum(hit.reshape(blk // 8, 8, blk), axis=0)
            return acc
        acc = lax.fori_loop(0, (n_chunk + 1) // 2, body, jnp.zeros((8, blk), F32))
        return jnp.sum(acc, axis=0, keepdims=True)

    keep = float(n_keep)
    thr = jnp.full((1, blk), INT_MIN, jnp.int32)
    zero = jnp.zeros((1, blk), jnp.int32)
    thr = jnp.where(count(lambda c, k: k >= zero) >= keep, zero, thr)

    def bit_step(t, thr):
        cand = thr + jnp.left_shift(jnp.int32(1), 30 - t)
        return jnp.where(count(lambda c, k: k >= cand) >= keep, cand, thr)

    thr = lax.fori_loop(0, 31, bit_step, thr)

    tied = (count(lambda c, k: k >= thr) > keep) & (thr > INT_MIN)

    @pl.when(jnp.max(jnp.where(tied, 1.0, 0.0)) > 0.0)
    def _():
        need = keep - count(lambda c, k: k > thr)

        def idx_step(t, last):
            cand = last + jnp.left_shift(jnp.int32(1), idx_bits - 1 - t)
            n = count(lambda c, k: (k == thr) & (c * blk + krow < cand))
            return jnp.where(n < need, cand, last)

        last = lax.fori_loop(0, idx_bits, idx_step, zero)

        def demote(c, carry):
            k = key_ref[c]
            hit = (k == thr) & (c * blk + krow > last) & tied
            key_ref[c] = jnp.where(hit, thr - 1, k)
            return carry

        lax.fori_loop(0, n_chunk, demote, 0)

    thr_eff = jnp.maximum(thr, INT_MIN + 1)
    qts = [qt_ref[0, _head_slice(h), :] for h in range(N_HEADS)]
    acc_ref[...] = jnp.zeros(acc_ref.shape, F32)

    def pair(c2, ms):
        parts = [[] for _ in range(N_HEADS)]
        for c in (2 * c2, 2 * c2 + 1):
            rows = pl.ds(pl.multiple_of(c * blk, blk), blk)
            allowed = key_ref[c] >= thr_eff
            for h in range(N_HEADS):
                s = jnp.where(allowed, _dot(k_ref[rows, _head_slice(h)], qts[h]), NEG_BIG)
                parts[h].append((s, vt_ref[c, _head_slice(h, HEAD_DIM + V_PAD), :], None))
        return tuple(_online_update(parts, ms, acc_ref))

    lax.fori_loop(0, (n_chunk + 1) // 2, pair, _init_max(N_HEADS, blk))
    for h in range(N_HEADS):
        o_ref[0, _head_slice(h), :] = _normalized(acc_ref, h, HEAD_DIM).astype(o_ref.dtype)


def _dsa(qit, wi, ki, qt, k, vt, batch, seq):
    blk = ROW_TILE
    nq = seq // blk
    n_keep = min(DSA_TOPK_MAX, seq // 4)
    tspec = lambda w: pl.BlockSpec((1, w, blk), lambda b, i: (b, 0, i))
    full = lambda w: pl.BlockSpec((seq, w), lambda b, i: (b, 0))
    return pl.pallas_call(
        functools.partial(_dsa_kernel, n_keep=n_keep, idx_bits=(seq - 1).bit_length()),
        grid=(batch, nq),
        in_specs=[tspec(IDX_HEADS * IDX_DIM), pl.BlockSpec((blk, LANES), lambda b, i: (b * nq + i, 0)),
                  full(IDX_DIM), tspec(HW), full(HW),
                  pl.BlockSpec((nq, vt.shape[1], blk), lambda b, i: (b, 0, 0))],
        out_specs=tspec(HW),
        out_shape=jax.ShapeDtypeStruct((batch, HW, seq), BF16),
        scratch_shapes=[pltpu.VMEM((nq + nq % 2, blk, blk), jnp.int32),
                        pltpu.VMEM((N_HEADS, HEAD_DIM + V_PAD, blk), F32)],
        compiler_params=_params(2),
        name="dsa",
    )(qit, wi, ki, qt, k, vt)


DIFF_GROUP = 4


def _diff_kernel(q1t_ref, q2t_ref, k1_ref, k2_ref, vt_ref, dl_ref, gain_ref, o_ref, acc_ref, *, lam_init):
    i = pl.program_id(1)
    blk = ROW_TILE
    dl = dl_ref[...]
    lam = (jnp.exp(jnp.sum(dl[0:1] * dl[1:2], axis=-1, keepdims=True))
           - jnp.exp(jnp.sum(dl[2:3] * dl[3:4], axis=-1, keepdims=True)) + lam_init)
    causal = (lax.broadcasted_iota(jnp.int32, (blk, 1), 0)
              <= lax.broadcasted_iota(jnp.int32, (1, blk), 1))
    maps = ((q1t_ref, k1_ref), (q2t_ref, k2_ref))

    for h0 in range(0, N_HEADS, DIFF_GROUP):
        chains = [(h, mp) for h in range(h0, h0 + DIFF_GROUP) for mp in range(2)]
        qts = [maps[mp][0][0, _head_slice(h), :] for h, mp in chains]
        acc_ref[...] = jnp.zeros(acc_ref.shape, F32)

        def step(j, ms, diag):
            rows = pl.ds(pl.multiple_of(j * blk, blk), blk)
            ss = [_dot(maps[mp][1][rows, _head_slice(h)], qts[c]) for c, (h, mp) in enumerate(chains)]
            if diag:
                ss = [jnp.where(causal, s, NEG_BIG) for s in ss]
            parts = [[(s, vt_ref[j, _head_slice(h, C_VDIM + V_PAD), :], None)]
                     for s, (h, _) in zip(ss, chains)]
            return tuple(_online_update(parts, ms, acc_ref))

        stats = lax.fori_loop(0, i, lambda j, st: step(j, st, False), _init_max(len(chains), blk))
        step(i, stats, True)
        for g in range(DIFF_GROUP):
            h = h0 + g
            o = _normalized(acc_ref, 2 * g, C_VDIM) - lam * _normalized(acc_ref, 2 * g + 1, C_VDIM)
            ms = jnp.mean(o * o, axis=0, keepdims=True)
            o = o * lax.rsqrt(ms + EPS) * gain_ref[...] * (1.0 - lam_init)
            o_ref[0, _head_slice(h, C_VDIM), :] = o.astype(o_ref.dtype)


def _diff(q1t, q2t, k1, k2, vt, dl, gain, lam_init, batch, seq):
    blk = ROW_TILE
    nq = seq // blk
    tspec = lambda w: pl.BlockSpec((1, w, blk), lambda b, i: (b, 0, i))
    kspec = pl.BlockSpec((seq, HW), lambda b, i: (b, 0))
    vw = N_HEADS * C_VDIM
    return pl.pallas_call(
        functools.partial(_diff_kernel, lam_init=lam_init),
        grid=(batch, nq),
        in_specs=[tspec(HW), tspec(HW), kspec, kspec,
                  pl.BlockSpec((nq, vt.shape[1], blk), lambda b, i: (b, 0, 0)),
                  pl.BlockSpec((4, HEAD_DIM), lambda b, i: (0, 0)),
                  pl.BlockSpec((C_VDIM, 1), lambda b, i: (0, 0))],
        out_specs=tspec(vw),
        out_shape=jax.ShapeDtypeStruct((batch, vw, seq), BF16),
        scratch_shapes=[pltpu.VMEM((2 * DIFF_GROUP, C_VDIM + V_PAD, blk), F32)],
        compiler_params=_params(2),
        name="diff",
    )(q1t, q2t, k1, k2, vt, dl, gain)


def _merge_kernel(x_ref, oa_ref, ob_ref, oc_ref, sg_ref, wa_ref, wb_ref, wc_ref, wo_ref, g_ref,
                  wr_hi_ref, wr_lo_ref, br_ref, x1_ref, hn_ref, route_ref):
    merged = (sg_ref[:, 0:D_MODEL].astype(F32) * _dot(oa_ref[...], wa_ref[...])
              + sg_ref[:, D_MODEL:2 * D_MODEL].astype(F32) * _dot(ob_ref[...], wb_ref[...])
              + sg_ref[:, 2 * D_MODEL:3 * D_MODEL].astype(F32) * _dot(oc_ref[...], wc_ref[...]))
    x1 = x_ref[...] + _dot(merged.astype(BF16), wo_ref[...])
    x1_ref[...] = x1
    ms = jnp.mean(x1 * x1, axis=-1, keepdims=True)
    hn = x1 * lax.rsqrt(ms + EPS) * g_ref[...]
    hn_ref[...] = hn

    hi, lo = _split_bf16(hn)
    lg = (_dot(hi, wr_hi_ref[...]) + _dot(lo, wr_hi_ref[...]) + _dot(hi, wr_lo_ref[...])
          + br_ref[...])
    lane = lax.broadcasted_iota(jnp.int32, lg.shape, 1)
    lanef = lane.astype(F32)
    far = float(LANES)
    is_g = (lane >= N_EXPERTS) & (lane < N_EXPERTS + N_GROUPS)
    gl = jnp.where(is_g, lg, -jnp.inf)
    gmax = jnp.max(gl, axis=-1, keepdims=True)
    gidx = jnp.min(jnp.where(gl == gmax, lanef, far), axis=-1, keepdims=True) - float(N_EXPERTS)
    g_w = 1.0 / jnp.sum(jnp.where(is_g, jnp.exp(gl - gmax), 0.0), axis=-1, keepdims=True)
    in_group = (lane < N_EXPERTS) & ((lane // EXPERTS_PER_GROUP).astype(F32) == gidx)
    el = jnp.where(in_group, lg, -jnp.inf)
    e1 = jnp.max(el, axis=-1, keepdims=True)
    i1 = jnp.min(jnp.where(el == e1, lanef, far), axis=-1, keepdims=True)
    el2 = jnp.where(lanef == i1, -jnp.inf, el)
    e2 = jnp.max(el2, axis=-1, keepdims=True)
    i2 = jnp.min(jnp.where(el2 == e2, lanef, far), axis=-1, keepdims=True)
    t = jnp.exp(e2 - e1)
    w1 = g_w / (1.0 + t)
    w2 = g_w * t / (1.0 + t)
    route_ref[...] = jnp.where(lane == 0, i1, jnp.where(lane == 1, i2, jnp.where(lane == 2, w1, jnp.where(lane == 3, w2, 0.0))))


def _merge(x2, oa, ob, oc, sg, wa, wb, wc, wo, norm_g, wr_hi, wr_lo, br):
    n = x2.shape[0]
    tm = min(DENSE_TILE, n)
    row = lambda w: pl.BlockSpec((tm, w), lambda i: (i, 0))
    const = lambda a: pl.BlockSpec(a.shape, lambda i: (0, 0))
    return pl.pallas_call(
        _merge_kernel,
        grid=(n // tm,),
        in_specs=[row(D_MODEL), row(HW), row(HW), row(N_HEADS * C_VDIM), row(3 * D_MODEL),
                  const(wa), const(wb), const(wc), const(wo), const(norm_g), const(wr_hi),
                  const(wr_lo), const(br)],
        out_specs=[row(D_MODEL), row(D_MODEL), row(LANES)],
        out_shape=[jax.ShapeDtypeStruct((n, D_MODEL), F32), jax.ShapeDtypeStruct((n, D_MODEL), F32),
                   jax.ShapeDtypeStruct((n, LANES), F32)],
        compiler_params=_params(1),
        name="merge",
    )(x2, oa, ob, oc, sg, wa, wb, wc, wo, norm_g, wr_hi, wr_lo, br)


MOE_BLOCK = 256
META_ROWS = 8
META_USED = 3 * LANES
META_END = 4 * LANES
META_PADDED = 5 * LANES


def _lane_prefix_sum(x):
    lane = lax.broadcasted_iota(jnp.int32, x.shape, 1)
    shift = 1
    while shift < LANES:
        x = x + jnp.where(lane >= shift, pltpu.roll(x, shift, 1), 0.0)
        shift *= 2
    return x


def _positions_kernel(route_ref, dest_ref, meta_ref, cnt_ref, base_ref):
    phase = pl.program_id(0)
    t = pl.program_id(1)
    tm = route_ref.shape[0]
    route = route_ref[...]
    lane = lax.broadcasted_iota(jnp.int32, route.shape, 1)
    lanef = lane.astype(F32)
    e1 = route[:, 0:1]
    e2 = route[:, 1:2]
    uses = jnp.where((lanef == e1) | (lanef == e2), 1.0, 0.0)
    tile_cnt = jnp.sum(uses, axis=0, keepdims=True)

    @pl.when((phase == 0) & (t == 0))
    def _():
        cnt_ref[...] = jnp.zeros(cnt_ref.shape, F32)

    @pl.when(phase == 0)
    def _():
        cnt_ref[...] += tile_cnt

    @pl.when((phase == 1) & (t == 0))
    def _():
        cnt = cnt_ref[...]
        padded = jnp.floor((cnt + (MOE_BLOCK - 1)) * (1.0 / MOE_BLOCK)) * MOE_BLOCK
        end = _lane_prefix_sum(padded)
        base_ref[...] = end - padded
        cnt_ref[...] = jnp.zeros(cnt_ref.shape, F32)
        lane1 = lax.broadcasted_iota(jnp.int32, (1, LANES), 1)
        row = lax.broadcasted_iota(jnp.int32, (META_ROWS, LANES), 0)
        col = lax.broadcasted_iota(jnp.int32, (META_ROWS, LANES), 1)
        first_row = ((row * LANES + col) * MOE_BLOCK).astype(F32)
        owner = jnp.zeros((META_ROWS, LANES), F32)
        for e in range(N_EXPERTS):
            end_e = jnp.sum(jnp.where(lane1 == e, end, 0.0), axis=-1, keepdims=True)
            owner = owner + jnp.where(end_e <= first_row, 1.0, 0.0)
        owner = jnp.minimum(owner, float(N_EXPERTS - 1))
        used = jnp.sum(jnp.where(lane1 == N_EXPERTS - 1, end, 0.0), axis=-1, keepdims=True) * (1.0 / MOE_BLOCK)
        meta = jnp.where(row == META_USED // LANES, used,
                         jnp.where(row == META_END // LANES, end,
                                   jnp.where(row == META_PADDED // LANES, padded, owner)))
        meta_ref[...] = meta.astype(jnp.int32)

    @pl.when(phase == 1)
    def _():
        before = (lax.broadcasted_iota(jnp.int32, (tm, tm), 1)
                  < lax.broadcasted_iota(jnp.int32, (tm, tm), 0)).astype(BF16)
        rank = _dot(before, uses.astype(BF16))
        pos = base_ref[...] + cnt_ref[...] + rank
        d1 = jnp.sum(jnp.where(lanef == e1, pos, 0.0), axis=-1, keepdims=True)
        d2 = jnp.sum(jnp.where(lanef == e2, pos, 0.0), axis=-1, keepdims=True)
        dest_ref[...] = jnp.where(lane == 0, d1, jnp.where(lane == 1, d2, 0.0)).astype(jnp.int32)
        cnt_ref[...] += tile_cnt


def _positions(route):
    n = route.shape[0]
    tm = min(4 * ROW_TILE, n)
    return pl.pallas_call(
        _positions_kernel,
        grid=(2, n // tm),
        in_specs=[pl.BlockSpec((tm, LANES), lambda p, t: (t, 0))],
        out_specs=[pl.BlockSpec((tm, LANES), lambda p, t: (t * p, 0)),
                   pl.BlockSpec((META_ROWS, LANES), lambda p, t: (0, 0))],
        out_shape=[jax.ShapeDtypeStruct((n, LANES), jnp.int32),
                   jax.ShapeDtypeStruct((META_ROWS, LANES), jnp.int32)],
        scratch_shapes=[pltpu.VMEM((1, LANES), F32), pltpu.VMEM((1, LANES), F32)],
        compiler_params=_params(2),
        name="moe_positions",
    )(route)


def _row_copy(src_ref, src_row, dst_ref, dst_row, sem):
    return pltpu.make_async_copy(src_ref.at[pl.ds(src_row, 1), :], dst_ref.at[pl.ds(dst_row, 1), :], sem)


def _dispatch_kernel(meta_ref, dest_ref, hn_ref, xs_ref, zero_ref, sem):
    tm = hn_ref.shape[0]

    @pl.when(pl.program_id(0) == 0)
    def _():
        zero_ref[...] = jnp.zeros(zero_ref.shape, F32)

        def fill(e):
            end = pl.multiple_of(meta_ref[META_END + e], MOE_BLOCK)
            return pltpu.make_async_copy(zero_ref, xs_ref.at[pl.ds(end - MOE_BLOCK, MOE_BLOCK), :], sem)

        for e in range(N_EXPERTS):
            @pl.when(meta_ref[META_PADDED + e] > 0)
            def _():
                fill(e).start()
        for e in range(N_EXPERTS):
            @pl.when(meta_ref[META_PADDED + e] > 0)
            def _():
                fill(e).wait()

        def spare(b):
            return pltpu.make_async_copy(
                zero_ref, xs_ref.at[pl.ds(pl.multiple_of(b * MOE_BLOCK, MOE_BLOCK), MOE_BLOCK), :], sem)

        n_blocks = xs_ref.shape[0] // MOE_BLOCK
        lax.fori_loop(meta_ref[META_USED], n_blocks, lambda b, c: (spare(b).start(), c)[1], 0)
        lax.fori_loop(meta_ref[META_USED], n_blocks, lambda b, c: (spare(b).wait(), c)[1], 0)

    def copies(r):
        return (_row_copy(hn_ref, r, xs_ref, dest_ref[0, 0, 2 * r], sem),
                _row_copy(hn_ref, r, xs_ref, dest_ref[0, 0, 2 * r + 1], sem))

    def start(r, carry):
        for cp in copies(r):
            cp.start()
        return carry

    lax.fori_loop(0, tm, start, 0)
    for _ in range(2):
        pltpu.make_async_copy(hn_ref, xs_ref.at[pl.ds(0, tm), :], sem).wait()


def _dispatch(meta, dest3, hn, n_rows):
    n = hn.shape[0]
    tm = dest3.shape[2] // 2
    return pl.pallas_call(
        _dispatch_kernel,
        grid_spec=pltpu.PrefetchScalarGridSpec(
            num_scalar_prefetch=1,
            grid=(n // tm,),
            in_specs=[pl.BlockSpec((1, 1, 2 * tm), lambda t, m: (t, 0, 0), memory_space=pltpu.SMEM),
                      pl.BlockSpec((tm, D_MODEL), lambda t, m: (t, 0))],
            out_specs=pl.BlockSpec(memory_space=pl.ANY),
            scratch_shapes=[pltpu.VMEM((MOE_BLOCK, D_MODEL), F32), pltpu.SemaphoreType.DMA(())]),
        out_shape=jax.ShapeDtypeStruct((n_rows, D_MODEL), F32),
        compiler_params=_params(1),
        name="moe_dispatch",
    )(meta, dest3, hn)


def _expert_kernel(meta_ref, xs_ref, wg_ref, wu_ref, wd_ref, y_ref, wg_bf, wu_bf, wd_bf):
    b = pl.program_id(0)
    holds_rows = b < meta_ref[META_USED]
    new_expert = (b == 0) | (meta_ref[b] != meta_ref[jnp.maximum(b - 1, 0)])

    @pl.when(holds_rows & new_expert)
    def _():
        wg_bf[...] = wg_ref[0].astype(BF16)
        wu_bf[...] = wu_ref[0].astype(BF16)
        wd_bf[...] = wd_ref[0].astype(BF16)

    @pl.when(holds_rows)
    def _():
        x = xs_ref[...].astype(BF16)
        g = _dot(x, wg_bf[...])
        u = _dot(x, wu_bf[...])
        hid = g * (1.0 / (1.0 + jnp.exp(-g))) * u
        y_ref[...] = _dot(hid.astype(BF16), wd_bf[...])

    @pl.when(jnp.logical_not(holds_rows))
    def _():
        y_ref[...] = jnp.zeros(y_ref.shape, F32)


def _experts(meta, xs, wg, wu, wd, layer):
    n_blocks = xs.shape[0] // MOE_BLOCK
    rows = pl.BlockSpec((MOE_BLOCK, D_MODEL), lambda b, m: (b, 0))
    weight = lambda shape: pl.BlockSpec(
        (None, 1) + shape, lambda b, m: (layer, m[jnp.minimum(b, m[META_USED] - 1)], 0, 0))
    return pl.pallas_call(
        _expert_kernel,
        grid_spec=pltpu.PrefetchScalarGridSpec(
            num_scalar_prefetch=1,
            grid=(n_blocks,),
            in_specs=[rows, weight((D_MODEL, D_EXPERT)), weight((D_MODEL, D_EXPERT)),
                      weight((D_EXPERT, D_MODEL))],
            out_specs=rows,
            scratch_shapes=[pltpu.VMEM((D_MODEL, D_EXPERT), BF16), pltpu.VMEM((D_MODEL, D_EXPERT), BF16),
                            pltpu.VMEM((D_EXPERT, D_MODEL), BF16)]),
        out_shape=jax.ShapeDtypeStruct(xs.shape, F32),
        compiler_params=_params(1),
        name="moe_experts",
    )(meta, xs, wg, wu, wd)


def _combine_kernel(dest_ref, x1_ref, route_ref, y_ref, o_ref, buf_ref, sem):
    tm = x1_ref.shape[0]

    def copies(r):
        return (_row_copy(y_ref, dest_ref[0, 0, 2 * r], buf_ref.at[0], r, sem),
                _row_copy(y_ref, dest_ref[0, 0, 2 * r + 1], buf_ref.at[1], r, sem))

    def start(r, carry):
        for cp in copies(r):
            cp.start()
        return carry

    lax.fori_loop(0, tm, start, 0)
    for slot in range(2):
        pltpu.make_async_copy(y_ref.at[pl.ds(0, tm), :], buf_ref.at[slot], sem).wait()
    route = route_ref[...]
    o_ref[...] = x1_ref[...] + route[:, 2:3] * buf_ref[0] + route[:, 3:4] * buf_ref[1]


def _combine(dest3, x1, route, y):
    n = x1.shape[0]
    tm = dest3.shape[2] // 2
    row = lambda w: pl.BlockSpec((tm, w), lambda t: (t, 0))
    return pl.pallas_call(
        _combine_kernel,
        grid=(n // tm,),
        in_specs=[pl.BlockSpec((1, 1, 2 * tm), lambda t: (t, 0, 0), memory_space=pltpu.SMEM),
                  row(D_MODEL), row(LANES), pl.BlockSpec(memory_space=pl.ANY)],
        out_specs=row(D_MODEL),
        out_shape=jax.ShapeDtypeStruct((n, D_MODEL), F32),
        scratch_shapes=[pltpu.VMEM((2, tm, D_MODEL), F32), pltpu.SemaphoreType.DMA(())],
        compiler_params=_params(1),
        name="moe_combine",
    )(dest3, x1, route, y)


def _moe(x1, hn, route, wg, wu, wd, layer):
    n = x1.shape[0]
    tm = min(ROW_TILE, n)
    n_blocks = -(-(2 * n + N_EXPERTS * (MOE_BLOCK - 1)) // MOE_BLOCK)
    assert n_blocks <= META_USED
    dest, meta = _positions(route)
    meta = meta.reshape(-1)
    dest3 = dest[:, :2].reshape(n // tm, 1, 2 * tm)
    xs = _dispatch(meta, dest3, hn, n_blocks * MOE_BLOCK)
    y = _experts(meta, xs, wg, wu, wd, layer)
    return _combine(dest3, x1, route, y)


def _rope_tables(seq):
    inv_freq = 1.0 / (ROPE_THETA ** (jnp.arange(0, HEAD_DIM, 2, dtype=F32) / HEAD_DIM))
    ang = jnp.arange(seq, dtype=F32)[:, None] * inv_freq[None, :]
    cos, sin = jnp.cos(ang), jnp.sin(ang)
    cos_t = jnp.tile(jnp.concatenate([cos, cos], axis=-1), (1, N_HEADS))
    sin_t = jnp.tile(jnp.concatenate([-sin, sin], axis=-1), (1, N_HEADS))
    return cos_t, sin_t


def _to_t(v, batch, n_chunk):
    tokens = v.shape[0]
    v = v.reshape(tokens, N_HEADS, -1)
    pad = jnp.zeros((tokens, N_HEADS, V_PAD), v.dtype).at[:, :, 0].set(1)
    v = jnp.concatenate([v, pad], axis=2).reshape(tokens, -1)
    feat = v.shape[1]
    return (v.reshape(batch, n_chunk, ROW_TILE, feat).transpose(0, 1, 3, 2)
            .reshape(batch * n_chunk, feat, ROW_TILE))


def _from_t(o_t):
    b, feat, seq = o_t.shape
    return o_t.transpose(0, 2, 1).reshape(b * seq, feat)


def kernel(x, norm_attn, w_in, qk_gain, idx_k_gain, diff_lambda, diff_subln_gain, w_proj_a, w_proj_b, w_proj_c, w_out, norm_ffn, w_group, b_group, w_router, b_router, w_e_gate, w_e_up, w_e_down):
    batch, seq, d = x.shape
    assert d == D_MODEL and seq % (2 * ROW_TILE) == 0 and ROW_TILE == MOBA_BLOCK
    n = batch * seq
    nq = seq // ROW_TILE
    depth = w_in.shape[0]
    cos_t, sin_t = _rope_tables(seq)
    x2 = x.reshape(n, d)
    for l in range(depth):
        w_pad = jnp.concatenate(
            [w_in[l][:, :KW_SRC], jnp.zeros((d, LANES - IDX_DIM - IDX_HEADS), F32), w_in[l][:, KW_SRC:]],
            axis=1).astype(BF16)
        gains = jnp.tile(qk_gain[l][jnp.array([0, 1, 2, 3, 4, 4, 5, 5])], (1, N_HEADS))
        kgain = jnp.pad(idx_k_gain[l], (0, LANES - IDX_DIM))[None, :]
        (qa, ka, va, qb, kb, vb, qi, ki, wi, q1, q2, k1, k2, vc, sg, kmean) = _project(
            x2, norm_attn[l][None, :], w_pad, cos_t, sin_t, gains, kgain, seq)

        feat_major = lambda t: t.reshape(batch, seq, t.shape[1]).transpose(0, 2, 1)
        o_a = _from_t(_moba(feat_major(qa), ka, _to_t(va, batch, nq), kmean.reshape(batch, nq, HW),
                            batch, seq))
        o_b = _from_t(_dsa(feat_major(qi), wi, ki, feat_major(qb), kb, _to_t(vb, batch, nq), batch, seq))
        lam_init = 0.8 - 0.6 * math.exp(-0.3 * l)
        o_c = _from_t(_diff(feat_major(q1), feat_major(q2), k1, k2, _to_t(vc, batch, nq), diff_lambda[l],
                            diff_subln_gain[l][:, None], lam_init, batch, seq))

        w_r = jnp.concatenate([w_router[l], w_group[l],
                               jnp.zeros((d, LANES - N_EXPERTS - N_GROUPS), F32)], axis=1)
        wr_hi = w_r.astype(BF16)
        wr_lo = (w_r - wr_hi.astype(F32)).astype(BF16)
        b_r = jnp.concatenate([b_router[l], b_group[l],
                               jnp.zeros((LANES - N_EXPERTS - N_GROUPS,), F32)])[None, :]
        x1, hn, route = _merge(x2, o_a, o_b, o_c, sg, w_proj_a[l].astype(BF16), w_proj_b[l].astype(BF16),
                            w_proj_c[l].astype(BF16), w_out[l].astype(BF16), norm_ffn[l][None, :],
                            wr_hi, wr_lo, b_r)
        x2 = _moe(x1, hn, route, w_e_gate, w_e_up, w_e_down, l)
    return x2.reshape(batch, seq, d)
```

```python
import functools
import math

import jax
import jax.numpy as jnp
from jax import lax
from jax.experimental import pallas as pl
from jax.experimental.pallas import tpu as pltpu

F32 = jnp.float32
BF16 = jnp.bfloat16

D_MODEL = 1024
HEAD_DIM = 64
ROPE_THETA = 10000.0
EPS = 1e-6
N_HEADS = 4
MOBA_BLOCK = 256
MOBA_TOPK = 3
IDX_HEADS = 8
IDX_DIM = 64
DSA_TOPK_MAX = 256
C_VDIM = 2 * HEAD_DIM
N_GROUPS = 4
EXPERTS_PER_GROUP = 8
N_EXPERTS = N_GROUPS * EXPERTS_PER_GROUP
D_EXPERT = 512

HW = N_HEADS * HEAD_DIM
LANES = 128
ROW_TILE = 256
DENSE_TILE = 512
VMEM_LIMIT = 56 * 1024 * 1024

_SEG = {}
_off = 0
for _name, _w in (("qa", HW), ("ka", HW), ("va", HW), ("qb", HW), ("kb", HW), ("vb", HW),
                  ("qi", IDX_HEADS * IDX_DIM), ("kw", LANES), ("q1", HW), ("q2", HW), ("k1", HW),
                  ("k2", HW), ("vc", N_HEADS * C_VDIM), ("ga", D_MODEL), ("gb", D_MODEL),
                  ("gc", D_MODEL)):
    _SEG[_name] = (_off, _w)
    _off += _w
D_IN_PAD = _off
KW_SRC = 6 * HW + IDX_HEADS * IDX_DIM + IDX_DIM + IDX_HEADS

NEG_BIG = -1e30
M_FLOOR = -1e20
INT_MIN = -(2 ** 31)
LOG2E = math.log2(math.e)
Q_SCALE = HEAD_DIM ** -0.5 * LOG2E
V_PAD = 16
NT_DIMS = (((1,), (1,)), ((), ()))


def _params(n_axes):
    return pltpu.CompilerParams(dimension_semantics=("arbitrary",) * n_axes,
                                vmem_limit_bytes=VMEM_LIMIT)


def _dot(a, b):
    return jnp.dot(a, b, preferred_element_type=F32)


def _dot_nt(a, b):
    return lax.dot_general(a, b, NT_DIMS, preferred_element_type=F32)


def _split_bf16(a):
    hi = a.astype(BF16)
    return hi, (a - hi.astype(F32)).astype(BF16)


def _swap_halves(y, width):
    lane = lax.broadcasted_iota(jnp.int32, y.shape, 1)
    first = (lane % HEAD_DIM) < (HEAD_DIM // 2)
    return jnp.where(first, pltpu.roll(y, width - HEAD_DIM // 2, 1), pltpu.roll(y, HEAD_DIM // 2, 1))


def _proj_kernel(x_ref, g_ref, w_ref, cos_ref, sin_ref, gain_ref, kgain_ref,
                 qa_ref, ka_ref, va_ref, qb_ref, kb_ref, vb_ref, qi_ref, ki_ref, wi_ref,
                 q1_ref, q2_ref, k1_ref, k2_ref, vc_ref, sg_ref, kmean_ref):
    x = x_ref[...]
    ms = jnp.mean(x * x, axis=-1, keepdims=True)
    h = (x * lax.rsqrt(ms + EPS) * g_ref[...]).astype(BF16)
    cos = cos_ref[...]
    sin = sin_ref[...]
    r = lax.broadcasted_iota(jnp.int32, (HW, HW), 0) // HEAD_DIM
    c = lax.broadcasted_iota(jnp.int32, (HW, HW), 1) // HEAD_DIM
    head_ones = (r == c).astype(BF16)

    def seg(name, lo=0, width=None):
        off, w = _SEG[name]
        width = w if width is None else width
        return _dot(h, w_ref[:, off + lo:off + lo + width])

    def rope(y):
        return y * cos + _swap_halves(y, HW) * sin

    def norm_rope(t, gain_row):
        hi, lo = _split_bf16(t * t)
        ss = _dot(hi, head_ones) + _dot(lo, head_ones)
        yn = t * lax.rsqrt(ss * (1.0 / HEAD_DIM) + EPS) * gain_ref[gain_row:gain_row + 1, :]
        return rope(yn)

    qa_ref[...] = (norm_rope(seg("qa"), 0) * Q_SCALE).astype(BF16)
    ka = norm_rope(seg("ka"), 1)
    ka_ref[...] = ka.astype(BF16)
    for blk in range(ka.shape[0] // MOBA_BLOCK):
        kmean_ref[blk] = jnp.mean(ka[blk * MOBA_BLOCK:(blk + 1) * MOBA_BLOCK], axis=0, keepdims=True)
    va_ref[...] = seg("va").astype(BF16)
    qb_ref[...] = (norm_rope(seg("qb"), 2) * Q_SCALE).astype(BF16)
    kb_ref[...] = norm_rope(seg("kb"), 3).astype(BF16)
    vb_ref[...] = seg("vb").astype(BF16)
    for half in range(2):
        qi_ref[:, half * HW:(half + 1) * HW] = rope(seg("qi", half * HW, HW)).astype(BF16)

    t = seg("kw")
    lane = lax.broadcasted_iota(jnp.int32, t.shape, 1)
    is_k = lane < IDX_DIM
    kms = jnp.sum(jnp.where(is_k, t * t, 0.0), axis=-1, keepdims=True) * (1.0 / IDX_DIM)
    kn = t * lax.rsqrt(kms + EPS) * kgain_ref[...]
    kr = kn * cos[:, :LANES] + _swap_halves(kn, LANES) * sin[:, :LANES]
    ki_ref[...] = kr[:, :IDX_DIM].astype(BF16)
    w_scale = (IDX_HEADS ** -0.5) * (IDX_DIM ** -0.5)
    wi_ref[...] = jnp.where(lane < IDX_HEADS, pltpu.roll(t, LANES - IDX_DIM, 1) * w_scale, 0.0)

    q1_ref[...] = (norm_rope(seg("q1"), 4) * Q_SCALE).astype(BF16)
    q2_ref[...] = (norm_rope(seg("q2"), 5) * Q_SCALE).astype(BF16)
    k1_ref[...] = norm_rope(seg("k1"), 6).astype(BF16)
    k2_ref[...] = norm_rope(seg("k2"), 7).astype(BF16)
    for half in range(2):
        vc_ref[:, half * HW:(half + 1) * HW] = seg("vc", half * HW, HW).astype(BF16)
    for gi, name in enumerate(("ga", "gb", "gc")):
        for part in range(D_MODEL // 512):
            g = seg(name, part * 512, 512)
            lo = gi * D_MODEL + part * 512
            sg_ref[:, lo:lo + 512] = (1.0 / (1.0 + jnp.exp(-g))).astype(BF16)


def _project(x2, norm_g, w_pad, cos_t, sin_t, gains, kgain, seq):
    n = x2.shape[0]
    tm = min(DENSE_TILE, seq)
    assert seq % tm == 0 and tm % MOBA_BLOCK == 0
    n_pos = seq // tm
    row = lambda w: pl.BlockSpec((tm, w), lambda i: (i, 0))
    const = lambda shape: pl.BlockSpec(shape, lambda i: (0,) * len(shape))
    out_widths = [HW] * 6 + [IDX_HEADS * IDX_DIM, IDX_DIM, LANES] + [HW] * 4 + [N_HEADS * C_VDIM, 3 * D_MODEL]
    out_dtypes = [BF16] * 8 + [F32] + [BF16] * 6
    out_shape = [jax.ShapeDtypeStruct((n, w), dt) for w, dt in zip(out_widths, out_dtypes)]
    out_shape.append(jax.ShapeDtypeStruct((n // MOBA_BLOCK, 1, HW), F32))
    out_specs = [row(w) for w in out_widths] + [pl.BlockSpec((tm // MOBA_BLOCK, 1, HW), lambda i: (i, 0, 0))]
    return pl.pallas_call(
        _proj_kernel,
        grid=(n // tm,),
        in_specs=[row(D_MODEL), const((1, D_MODEL)),
                  pl.BlockSpec((D_MODEL, D_IN_PAD), lambda i: (0, 0), pipeline_mode=pl.Buffered(1)),
                  pl.BlockSpec((tm, HW), lambda i: (i % n_pos, 0)),
                  pl.BlockSpec((tm, HW), lambda i: (i % n_pos, 0)),
                  const((8, HW)), const((1, LANES))],
        out_specs=out_specs,
        out_shape=out_shape,
        compiler_params=_params(1),
        name="proj",
    )(x2, norm_g, w_pad, cos_t, sin_t, gains, kgain)


def _online_update(parts, ms, acc_ref):
    ps, out = [], []
    for c, tiles in enumerate(parts):
        m_new = ms[c]
        for s, _, ok in tiles:
            smax = jnp.max(s, axis=0, keepdims=True)
            m_new = jnp.maximum(m_new, smax if ok is None else jnp.where(ok, smax, NEG_BIG))
        m_eff = jnp.maximum(m_new, M_FLOOR)
        out.append(m_new)
        probs = [jnp.exp2(s - (m_eff if ok is None else jnp.where(ok, m_eff, -NEG_BIG))).astype(BF16)
                 for s, _, ok in tiles]
        ps.append((jnp.exp2(ms[c] - m_new), probs))
    for c, (alpha, probs) in enumerate(ps):
        acc = alpha * acc_ref[c]
        for (_, vt, _), p in zip(parts[c], probs):
            acc = acc + _dot(vt, p)
        acc_ref[c] = acc
    return out


def _init_max(n_chains, tq):
    return tuple(jnp.full((1, tq), NEG_BIG, F32) for _ in range(n_chains))


def _normalized(acc_ref, c, dv):
    acc = acc_ref[c]
    return acc[:dv] / acc[dv:dv + 1]


def _head_slice(h, width=HEAD_DIM):
    return slice(h * width, (h + 1) * width)


def _moba_kernel(qt_ref, k_ref, vt_ref, kmean_ref, o_ref, acc_ref, *, n_sel):
    i = pl.program_id(1)
    blk = MOBA_BLOCK
    nb = kmean_ref.shape[1]
    km = kmean_ref[0]
    brow = lax.broadcasted_iota(jnp.int32, (nb, blk), 0)
    browf = brow.astype(F32)
    causal = (lax.broadcasted_iota(jnp.int32, (blk, 1), 0)
              <= lax.broadcasted_iota(jnp.int32, (1, blk), 1))
    qts = [qt_ref[0, _head_slice(h), :] for h in range(N_HEADS)]
    sels = []
    for h in range(N_HEADS):
        km_hi, km_lo = _split_bf16(km[:, _head_slice(h)])
        gate = _dot(km_hi, qts[h]) + _dot(km_lo, qts[h])
        gate = jnp.where(brow < i, gate, -jnp.inf)
        sel = jnp.zeros((nb, blk), F32)
        for _ in range(n_sel):
            gm = jnp.max(gate, axis=0, keepdims=True)
            is_m = (gate == gm) & (gm > -jnp.inf)
            first = jnp.min(jnp.where(is_m, browf, float(nb)), axis=0, keepdims=True)
            pick = browf == first
            sel = jnp.where(pick, 1.0, sel)
            gate = jnp.where(pick, -jnp.inf, gate)
        sels.append(sel)
    acc_ref[...] = jnp.zeros(acc_ref.shape, F32)

    def tile(j, h, mask=None, seen=False):
        rows = pl.ds(pl.multiple_of(j * blk, blk), blk)
        s = _dot(k_ref[rows, _head_slice(h)], qts[h])
        if mask is not None:
            s = jnp.where(mask, s, NEG_BIG)
        ok = jnp.sum(jnp.where(brow == j, sels[h], 0.0), axis=0, keepdims=True) > 0.0 if seen else None
        return s, vt_ref[j, _head_slice(h, HEAD_DIM + V_PAD), :], ok

    def pair(j2, ms):
        parts = [[tile(2 * j2, h, seen=True), tile(2 * j2 + 1, h, seen=True)] for h in range(N_HEADS)]
        return tuple(_online_update(parts, ms, acc_ref))

    ms = lax.fori_loop(0, i // 2, pair, _init_max(N_HEADS, blk))

    @pl.when(i % 2 == 1)
    def _():
        _online_update([[tile(i - 1, h, seen=True), tile(i, h, mask=causal)] for h in range(N_HEADS)],
                       ms, acc_ref)

    @pl.when(i % 2 == 0)
    def _():
        _online_update([[tile(i, h, mask=causal)] for h in range(N_HEADS)], ms, acc_ref)

    for h in range(N_HEADS):
        o_ref[0, _head_slice(h), :] = _normalized(acc_ref, h, HEAD_DIM).astype(o_ref.dtype)


def _moba(qt, k, vt, kmean, batch, seq):
    blk = MOBA_BLOCK
    nb = seq // blk
    n_sel = min(MOBA_TOPK, nb - 1)
    tspec = pl.BlockSpec((1, HW, blk), lambda b, i: (b, 0, i))
    return pl.pallas_call(
        functools.partial(_moba_kernel, n_sel=n_sel),
        grid=(batch, nb),
        in_specs=[tspec,
                  pl.BlockSpec((seq, HW), lambda b, i: (b, 0)),
                  pl.BlockSpec((nb, vt.shape[1], blk), lambda b, i: (b, 0, 0)),
                  pl.BlockSpec((1, nb, HW), lambda b, i: (b, 0, 0))],
        out_specs=tspec,
        out_shape=jax.ShapeDtypeStruct((batch, HW, seq), BF16),
        scratch_shapes=[pltpu.VMEM((N_HEADS, HEAD_DIM + V_PAD, blk), F32)],
        compiler_params=_params(2),
        name="moba",
    )(qt, k, vt, kmean)


def _bit_planes(words):
    a = list(words)
    assert len(a) == 32
    mask, j = 0x0000FFFF, 16
    while j:
        k = 0
        while k < 32:
            t = (a[k] ^ (a[k + j] >> j)) & mask
            a[k] = a[k] ^ t
            a[k + j] = a[k + j] ^ (t << j)
            k = (k + j + 1) & ~j
        j >>= 1
        mask = (mask ^ (mask << j)) & 0xFFFFFFFF
    return a


def _dsa_kernel(qit_ref, wi_ref, ki_ref, qt_ref, k_ref, vt_ref, o_ref, key_ref, plane_ref, acc_ref, *,
                n_keep, idx_bits):
    i = pl.program_id(1)
    blk = ROW_TILE
    n_chunk = i + 1
    n_planes, n_slots = plane_ref.shape[:2]

    @pl.when(i == 0)
    def _():
        plane_ref[:, 1:] = jnp.zeros((n_planes, n_slots - 1) + plane_ref.shape[2:], jnp.int32)

    w_t = wi_ref[...].T
    qpos = i * blk + lax.broadcasted_iota(jnp.int32, (1, blk), 1)
    krow = lax.broadcasted_iota(jnp.int32, (blk, 1), 0)

    def score_chunk(c, carry):
        kc = ki_ref[pl.ds(pl.multiple_of(c * blk, blk), blk), :]
        lgs = [_dot(kc, qit_ref[0, _head_slice(h, IDX_DIM), :]) for h in range(IDX_HEADS)]
        sc = jnp.zeros((blk, blk), F32)
        for h in range(IDX_HEADS):
            sc = sc + w_t[h:h + 1, :] * jnp.maximum(lgs[h], 0.0)
        sc = sc + 0.0
        bits = pltpu.bitcast(sc, jnp.int32)
        key = jnp.where(bits < 0, bits ^ 0x7FFFFFFF, bits)
        key = jnp.where(c * blk + krow <= qpos, key, INT_MIN)
        key_ref[c] = key
        planes = _bit_planes(list((key ^ INT_MIN).reshape(n_planes, blk // n_planes, blk)))
        for b in range(n_planes):
            plane_ref[b, c] = planes[n_planes - 1 - b]
        return carry

    lax.fori_loop(0, n_chunk, score_chunk, 0)

    @pl.when(n_chunk % 2 == 1)
    def _():
        key_ref[n_chunk] = jnp.full((blk, blk), INT_MIN, jnp.int32)

    def count(pred):
        def body(c2, acc):
            for c in (2 * c2, 2 * c2 + 1):
                hit = jnp.where(pred(c, key_ref[c]), 1.0, 0.0)
                acc = acc + jnp.sum(hit.reshape(blk // 8, 8, blk), axis=0)
            return acc
        acc = lax.fori_loop(0, (n_chunk + 1) // 2, body, jnp.zeros((8, blk), F32))
        return jnp.sum(acc, axis=0, keepdims=True)

    keep = float(n_keep)

    def n_set(words):
        per_row = jnp.sum(lax.population_count(words), axis=0)
        return jnp.sum(per_row.astype(F32), axis=0, keepdims=True)

    def select_bit(t, carry):
        eq, gt, prefix = carry
        b = n_planes - 1 - t
        plane = plane_ref[b]
        cand = gt | (eq & plane)
        take = n_set(cand) >= keep
        eq = eq & jnp.where(take, plane, ~plane)
        gt = jnp.where(take, gt, cand)
        prefix = prefix | jnp.where(take, jnp.left_shift(jnp.int32(1), b), 0)
        return eq, gt, prefix

    state = (jnp.full(plane_ref.shape[1:], -1, jnp.int32), jnp.zeros(plane_ref.shape[1:], jnp.int32),
             jnp.zeros((1, blk), jnp.int32))
    eq, gt, prefix = lax.fori_loop(0, n_planes, select_bit, state)
    thr = prefix ^ INT_MIN
    zero = jnp.zeros((1, blk), jnp.int32)

    tied = (n_set(eq | gt) > keep) & (thr > INT_MIN)

    @pl.when(jnp.max(jnp.where(tied, 1.0, 0.0)) > 0.0)
    def _():
        need = keep - count(lambda c, k: k > thr)

        def idx_step(t, last):
            cand = last + jnp.left_shift(jnp.int32(1), idx_bits - 1 - t)
            n = count(lambda c, k: (k == thr) & (c * blk + krow < cand))
            return jnp.where(n < need, cand, last)

        last = lax.fori_loop(0, idx_bits, idx_step, zero)

        def demote(c, carry):
            k = key_ref[c]
            hit = (k == thr) & (c * blk + krow > last) & tied
            key_ref[c] = jnp.where(hit, thr - 1, k)
            return carry

        lax.fori_loop(0, n_chunk, demote, 0)

    thr_eff = jnp.maximum(thr, INT_MIN + 1)
    qts = [qt_ref[0, _head_slice(h), :] for h in range(N_HEADS)]
    acc_ref[...] = jnp.zeros(acc_ref.shape, F32)

    def pair(c2, ms):
        parts = [[] for _ in range(N_HEADS)]
        for c in (2 * c2, 2 * c2 + 1):
            rows = pl.ds(pl.multiple_of(c * blk, blk), blk)
            allowed = key_ref[c] >= thr_eff
            for h in range(N_HEADS):
                s = jnp.where(allowed, _dot(k_ref[rows, _head_slice(h)], qts[h]), NEG_BIG)
                parts[h].append((s, vt_ref[c, _head_slice(h, HEAD_DIM + V_PAD), :], None))
        return tuple(_online_update(parts, ms, acc_ref))

    lax.fori_loop(0, (n_chunk + 1) // 2, pair, _init_max(N_HEADS, blk))
    for h in range(N_HEADS):
        o_ref[0, _head_slice(h), :] = _normalized(acc_ref, h, HEAD_DIM).astype(o_ref.dtype)


def _dsa(qit, wi, ki, qt, k, vt, batch, seq):
    blk = ROW_TILE
    nq = seq // blk
    n_keep = min(DSA_TOPK_MAX, seq // 4)
    tspec = lambda w: pl.BlockSpec((1, w, blk), lambda b, i: (b, 0, i))
    full = lambda w: pl.BlockSpec((seq, w), lambda b, i: (b, 0))
    return pl.pallas_call(
        functools.partial(_dsa_kernel, n_keep=n_keep, idx_bits=(seq - 1).bit_length()),
        grid=(batch, nq),
        in_specs=[tspec(IDX_HEADS * IDX_DIM), pl.BlockSpec((blk, LANES), lambda b, i: (b * nq + i, 0)),
                  full(IDX_DIM), tspec(HW), full(HW),
                  pl.BlockSpec((nq, vt.shape[1], blk), lambda b, i: (b, 0, 0))],
        out_specs=tspec(HW),
        out_shape=jax.ShapeDtypeStruct((batch, HW, seq), BF16),
        scratch_shapes=[pltpu.VMEM((nq + nq % 2, blk, blk), jnp.int32),
                        pltpu.VMEM((32, nq, blk // 32, blk), jnp.int32),
                        pltpu.VMEM((N_HEADS, HEAD_DIM + V_PAD, blk), F32)],
        compiler_params=_params(2),
        name="dsa",
    )(qit, wi, ki, qt, k, vt)


DIFF_GROUP = 4


def _diff_kernel(q1t_ref, q2t_ref, k1_ref, k2_ref, vt_ref, dl_ref, gain_ref, o_ref, acc_ref, *, lam_init):
    i = pl.program_id(1)
    blk = ROW_TILE
    dl = dl_ref[...]
    lam = (jnp.exp(jnp.sum(dl[0:1] * dl[1:2], axis=-1, keepdims=True))
           - jnp.exp(jnp.sum(dl[2:3] * dl[3:4], axis=-1, keepdims=True)) + lam_init)
    causal = (lax.broadcasted_iota(jnp.int32, (blk, 1), 0)
              <= lax.broadcasted_iota(jnp.int32, (1, blk), 1))
    maps = ((q1t_ref, k1_ref), (q2t_ref, k2_ref))

    for h0 in range(0, N_HEADS, DIFF_GROUP):
        chains = [(h, mp) for h in range(h0, h0 + DIFF_GROUP) for mp in range(2)]
        qts = [maps[mp][0][0, _head_slice(h), :] for h, mp in chains]
        acc_ref[...] = jnp.zeros(acc_ref.shape, F32)

        def step(j, ms, diag):
            rows = pl.ds(pl.multiple_of(j * blk, blk), blk)
            ss = [_dot(maps[mp][1][rows, _head_slice(h)], qts[c]) for c, (h, mp) in enumerate(chains)]
            if diag:
                ss = [jnp.where(causal, s, NEG_BIG) for s in ss]
            parts = [[(s, vt_ref[j, _head_slice(h, C_VDIM + V_PAD), :], None)]
                     for s, (h, _) in zip(ss, chains)]
            return tuple(_online_update(parts, ms, acc_ref))

        stats = lax.fori_loop(0, i, lambda j, st: step(j, st, False), _init_max(len(chains), blk))
        step(i, stats, True)
        for g in range(DIFF_GROUP):
            h = h0 + g
            o = _normalized(acc_ref, 2 * g, C_VDIM) - lam * _normalized(acc_ref, 2 * g + 1, C_VDIM)
            ms = jnp.mean(o * o, axis=0, keepdims=True)
            o = o * lax.rsqrt(ms + EPS) * gain_ref[...] * (1.0 - lam_init)
            o_ref[0, _head_slice(h, C_VDIM), :] = o.astype(o_ref.dtype)


def _diff(q1t, q2t, k1, k2, vt, dl, gain, lam_init, batch, seq):
    blk = ROW_TILE
    nq = seq // blk
    tspec = lambda w: pl.BlockSpec((1, w, blk), lambda b, i: (b, 0, i))
    kspec = pl.BlockSpec((seq, HW), lambda b, i: (b, 0))
    vw = N_HEADS * C_VDIM
    return pl.pallas_call(
        functools.partial(_diff_kernel, lam_init=lam_init),
        grid=(batch, nq),
        in_specs=[tspec(HW), tspec(HW), kspec, kspec,
                  pl.BlockSpec((nq, vt.shape[1], blk), lambda b, i: (b, 0, 0)),
                  pl.BlockSpec((4, HEAD_DIM), lambda b, i: (0, 0)),
                  pl.BlockSpec((C_VDIM, 1), lambda b, i: (0, 0))],
        out_specs=tspec(vw),
        out_shape=jax.ShapeDtypeStruct((batch, vw, seq), BF16),
        scratch_shapes=[pltpu.VMEM((2 * DIFF_GROUP, C_VDIM + V_PAD, blk), F32)],
        compiler_params=_params(2),
        name="diff",
    )(q1t, q2t, k1, k2, vt, dl, gain)


def _merge_kernel(x_ref, oa_ref, ob_ref, oc_ref, sg_ref, wa_ref, wb_ref, wc_ref, wo_ref, g_ref,
                  wr_hi_ref, wr_lo_ref, br_ref, x1_ref, hn_ref, route_ref):
    merged = (sg_ref[:, 0:D_MODEL].astype(F32) * _dot(oa_ref[...], wa_ref[...])
              + sg_ref[:, D_MODEL:2 * D_MODEL].astype(F32) * _dot(ob_ref[...], wb_ref[...])
              + sg_ref[:, 2 * D_MODEL:3 * D_MODEL].astype(F32) * _dot(oc_ref[...], wc_ref[...]))
    x1 = x_ref[...] + _dot(merged.astype(BF16), wo_ref[...])
    x1_ref[...] = x1
    ms = jnp.mean(x1 * x1, axis=-1, keepdims=True)
    hn = x1 * lax.rsqrt(ms + EPS) * g_ref[...]
    hn_ref[...] = hn

    hi, lo = _split_bf16(hn)
    lg = (_dot(hi, wr_hi_ref[...]) + _dot(lo, wr_hi_ref[...]) + _dot(hi, wr_lo_ref[...])
          + br_ref[...])
    lane = lax.broadcasted_iota(jnp.int32, lg.shape, 1)
    lanef = lane.astype(F32)
    far = float(LANES)
    is_g = (lane >= N_EXPERTS) & (lane < N_EXPERTS + N_GROUPS)
    gl = jnp.where(is_g, lg, -jnp.inf)
    gmax = jnp.max(gl, axis=-1, keepdims=True)
    gidx = jnp.min(jnp.where(gl == gmax, lanef, far), axis=-1, keepdims=True) - float(N_EXPERTS)
    g_w = 1.0 / jnp.sum(jnp.where(is_g, jnp.exp(gl - gmax), 0.0), axis=-1, keepdims=True)
    in_group = (lane < N_EXPERTS) & ((lane // EXPERTS_PER_GROUP).astype(F32) == gidx)
    el = jnp.where(in_group, lg, -jnp.inf)
    e1 = jnp.max(el, axis=-1, keepdims=True)
    i1 = jnp.min(jnp.where(el == e1, lanef, far), axis=-1, keepdims=True)
    el2 = jnp.where(lanef == i1, -jnp.inf, el)
    e2 = jnp.max(el2, axis=-1, keepdims=True)
    i2 = jnp.min(jnp.where(el2 == e2, lanef, far), axis=-1, keepdims=True)
    t = jnp.exp(e2 - e1)
    w1 = g_w / (1.0 + t)
    w2 = g_w * t / (1.0 + t)
    route_ref[...] = jnp.where(lane == 0, i1, jnp.where(lane == 1, i2, jnp.where(lane == 2, w1, jnp.where(lane == 3, w2, 0.0))))


def _merge(x2, oa, ob, oc, sg, wa, wb, wc, wo, norm_g, wr_hi, wr_lo, br):
    n = x2.shape[0]
    tm = min(DENSE_TILE, n)
    row = lambda w: pl.BlockSpec((tm, w), lambda i: (i, 0))
    const = lambda a: pl.BlockSpec(a.shape, lambda i: (0, 0))
    return pl.pallas_call(
        _merge_kernel,
        grid=(n // tm,),
        in_specs=[row(D_MODEL), row(HW), row(HW), row(N_HEADS * C_VDIM), row(3 * D_MODEL),
                  const(wa), const(wb), const(wc), const(wo), const(norm_g), const(wr_hi),
                  const(wr_lo), const(br)],
        out_specs=[row(D_MODEL), row(D_MODEL), row(LANES)],
        out_shape=[jax.ShapeDtypeStruct((n, D_MODEL), F32), jax.ShapeDtypeStruct((n, D_MODEL), F32),
                   jax.ShapeDtypeStruct((n, LANES), F32)],
        compiler_params=_params(1),
        name="merge",
    )(x2, oa, ob, oc, sg, wa, wb, wc, wo, norm_g, wr_hi, wr_lo, br)


MOE_BLOCK = 256
META_ROWS = 8
META_USED = 3 * LANES
META_END = 4 * LANES
META_PADDED = 5 * LANES


def _lane_prefix_sum(x):
    lane = lax.broadcasted_iota(jnp.int32, x.shape, 1)
    shift = 1
    while shift < LANES:
        x = x + jnp.where(lane >= shift, pltpu.roll(x, shift, 1), 0.0)
        shift *= 2
    return x


def _positions_kernel(route_ref, dest_ref, meta_ref, cnt_ref, base_ref):
    phase = pl.program_id(0)
    t = pl.program_id(1)
    tm = route_ref.shape[0]
    route = route_ref[...]
    lane = lax.broadcasted_iota(jnp.int32, route.shape, 1)
    lanef = lane.astype(F32)
    e1 = route[:, 0:1]
    e2 = route[:, 1:2]
    uses = jnp.where((lanef == e1) | (lanef == e2), 1.0, 0.0)
    tile_cnt = jnp.sum(uses, axis=0, keepdims=True)

    @pl.when((phase == 0) & (t == 0))
    def _():
        cnt_ref[...] = jnp.zeros(cnt_ref.shape, F32)

    @pl.when(phase == 0)
    def _():
        cnt_ref[...] += tile_cnt

    @pl.when((phase == 1) & (t == 0))
    def _():
        cnt = cnt_ref[...]
        padded = jnp.floor((cnt + (MOE_BLOCK - 1)) * (1.0 / MOE_BLOCK)) * MOE_BLOCK
        end = _lane_prefix_sum(padded)
        base_ref[...] = end - padded
        cnt_ref[...] = jnp.zeros(cnt_ref.shape, F32)
        lane1 = lax.broadcasted_iota(jnp.int32, (1, LANES), 1)
        row = lax.broadcasted_iota(jnp.int32, (META_ROWS, LANES), 0)
        col = lax.broadcasted_iota(jnp.int32, (META_ROWS, LANES), 1)
        first_row = ((row * LANES + col) * MOE_BLOCK).astype(F32)
        owner = jnp.zeros((META_ROWS, LANES), F32)
        for e in range(N_EXPERTS):
            end_e = jnp.sum(jnp.where(lane1 == e, end, 0.0), axis=-1, keepdims=True)
            owner = owner + jnp.where(end_e <= first_row, 1.0, 0.0)
        owner = jnp.minimum(owner, float(N_EXPERTS - 1))
        used = jnp.sum(jnp.where(lane1 == N_EXPERTS - 1, end, 0.0), axis=-1, keepdims=True) * (1.0 / MOE_BLOCK)
        meta = jnp.where(row == META_USED // LANES, used,
                         jnp.where(row == META_END // LANES, end,
                                   jnp.where(row == META_PADDED // LANES, padded, owner)))
        meta_ref[...] = meta.astype(jnp.int32)

    @pl.when(phase == 1)
    def _():
        before = (lax.broadcasted_iota(jnp.int32, (tm, tm), 1)
                  < lax.broadcasted_iota(jnp.int32, (tm, tm), 0)).astype(BF16)
        rank = _dot(before, uses.astype(BF16))
        pos = base_ref[...] + cnt_ref[...] + rank
        d1 = jnp.sum(jnp.where(lanef == e1, pos, 0.0), axis=-1, keepdims=True)
        d2 = jnp.sum(jnp.where(lanef == e2, pos, 0.0), axis=-1, keepdims=True)
        dest_ref[...] = jnp.where(lane == 0, d1, jnp.where(lane == 1, d2, 0.0)).astype(jnp.int32)
        cnt_ref[...] += tile_cnt


def _positions(route):
    n = route.shape[0]
    tm = min(4 * ROW_TILE, n)
    return pl.pallas_call(
        _positions_kernel,
        grid=(2, n // tm),
        in_specs=[pl.BlockSpec((tm, LANES), lambda p, t: (t, 0))],
        out_specs=[pl.BlockSpec((tm, LANES), lambda p, t: (t * p, 0)),
                   pl.BlockSpec((META_ROWS, LANES), lambda p, t: (0, 0))],
        out_shape=[jax.ShapeDtypeStruct((n, LANES), jnp.int32),
                   jax.ShapeDtypeStruct((META_ROWS, LANES), jnp.int32)],
        scratch_shapes=[pltpu.VMEM((1, LANES), F32), pltpu.VMEM((1, LANES), F32)],
        compiler_params=_params(2),
        name="moe_positions",
    )(route)


def _row_copy(src_ref, src_row, dst_ref, dst_row, sem):
    return pltpu.make_async_copy(src_ref.at[pl.ds(src_row, 1), :], dst_ref.at[pl.ds(dst_row, 1), :], sem)


def _dispatch_kernel(meta_ref, dest_ref, hn_ref, xs_ref, zero_ref, sem):
    tm = hn_ref.shape[0]

    @pl.when(pl.program_id(0) == 0)
    def _():
        zero_ref[...] = jnp.zeros(zero_ref.shape, F32)

        def fill(e):
            end = pl.multiple_of(meta_ref[META_END + e], MOE_BLOCK)
            return pltpu.make_async_copy(zero_ref, xs_ref.at[pl.ds(end - MOE_BLOCK, MOE_BLOCK), :], sem)

        for e in range(N_EXPERTS):
            @pl.when(meta_ref[META_PADDED + e] > 0)
            def _():
                fill(e).start()
        for e in range(N_EXPERTS):
            @pl.when(meta_ref[META_PADDED + e] > 0)
            def _():
                fill(e).wait()

        def spare(b):
            return pltpu.make_async_copy(
                zero_ref, xs_ref.at[pl.ds(pl.multiple_of(b * MOE_BLOCK, MOE_BLOCK), MOE_BLOCK), :], sem)

        n_blocks = xs_ref.shape[0] // MOE_BLOCK
        lax.fori_loop(meta_ref[META_USED], n_blocks, lambda b, c: (spare(b).start(), c)[1], 0)
        lax.fori_loop(meta_ref[META_USED], n_blocks, lambda b, c: (spare(b).wait(), c)[1], 0)

    def copies(r):
        return (_row_copy(hn_ref, r, xs_ref, dest_ref[0, 0, 2 * r], sem),
                _row_copy(hn_ref, r, xs_ref, dest_ref[0, 0, 2 * r + 1], sem))

    def start(r, carry):
        for cp in copies(r):
            cp.start()
        return carry

    lax.fori_loop(0, tm, start, 0)
    for _ in range(2):
        pltpu.make_async_copy(hn_ref, xs_ref.at[pl.ds(0, tm), :], sem).wait()


def _dispatch(meta, dest3, hn, n_rows):
    n = hn.shape[0]
    tm = dest3.shape[2] // 2
    return pl.pallas_call(
        _dispatch_kernel,
        grid_spec=pltpu.PrefetchScalarGridSpec(
            num_scalar_prefetch=1,
            grid=(n // tm,),
            in_specs=[pl.BlockSpec((1, 1, 2 * tm), lambda t, m: (t, 0, 0), memory_space=pltpu.SMEM),
                      pl.BlockSpec((tm, D_MODEL), lambda t, m: (t, 0))],
            out_specs=pl.BlockSpec(memory_space=pl.ANY),
            scratch_shapes=[pltpu.VMEM((MOE_BLOCK, D_MODEL), F32), pltpu.SemaphoreType.DMA(())]),
        out_shape=jax.ShapeDtypeStruct((n_rows, D_MODEL), F32),
        compiler_params=_params(1),
        name="moe_dispatch",
    )(meta, dest3, hn)


def _expert_kernel(meta_ref, xs_ref, wg_ref, wu_ref, wd_ref, y_ref, wg_bf, wu_bf, wd_bf):
    b = pl.program_id(0)
    holds_rows = b < meta_ref[META_USED]
    new_expert = (b == 0) | (meta_ref[b] != meta_ref[jnp.maximum(b - 1, 0)])

    @pl.when(holds_rows & new_expert)
    def _():
        wg_bf[...] = wg_ref[0].astype(BF16)
        wu_bf[...] = wu_ref[0].astype(BF16)
        wd_bf[...] = wd_ref[0].astype(BF16)

    @pl.when(holds_rows)
    def _():
        x = xs_ref[...].astype(BF16)
        g = _dot(x, wg_bf[...])
        u = _dot(x, wu_bf[...])
        hid = g * (1.0 / (1.0 + jnp.exp(-g))) * u
        y_ref[...] = _dot(hid.astype(BF16), wd_bf[...])

    @pl.when(jnp.logical_not(holds_rows))
    def _():
        y_ref[...] = jnp.zeros(y_ref.shape, F32)


def _experts(meta, xs, wg, wu, wd, layer):
    n_blocks = xs.shape[0] // MOE_BLOCK
    rows = pl.BlockSpec((MOE_BLOCK, D_MODEL), lambda b, m: (b, 0))
    weight = lambda shape: pl.BlockSpec(
        (None, 1) + shape, lambda b, m: (layer, m[jnp.minimum(b, m[META_USED] - 1)], 0, 0))
    return pl.pallas_call(
        _expert_kernel,
        grid_spec=pltpu.PrefetchScalarGridSpec(
            num_scalar_prefetch=1,
            grid=(n_blocks,),
            in_specs=[rows, weight((D_MODEL, D_EXPERT)), weight((D_MODEL, D_EXPERT)),
                      weight((D_EXPERT, D_MODEL))],
            out_specs=rows,
            scratch_shapes=[pltpu.VMEM((D_MODEL, D_EXPERT), BF16), pltpu.VMEM((D_MODEL, D_EXPERT), BF16),
                            pltpu.VMEM((D_EXPERT, D_MODEL), BF16)]),
        out_shape=jax.ShapeDtypeStruct(xs.shape, F32),
        compiler_params=_params(1),
        name="moe_experts",
    )(meta, xs, wg, wu, wd)


def _combine_kernel(dest_ref, x1_ref, route_ref, y_ref, o_ref, buf_ref, sem):
    tm = x1_ref.shape[0]

    def copies(r):
        return (_row_copy(y_ref, dest_ref[0, 0, 2 * r], buf_ref.at[0], r, sem),
                _row_copy(y_ref, dest_ref[0, 0, 2 * r + 1], buf_ref.at[1], r, sem))

    def start(r, carry):
        for cp in copies(r):
            cp.start()
        return carry

    lax.fori_loop(0, tm, start, 0)
    for slot in range(2):
        pltpu.make_async_copy(y_ref.at[pl.ds(0, tm), :], buf_ref.at[slot], sem).wait()
    route = route_ref[...]
    o_ref[...] = x1_ref[...] + route[:, 2:3] * buf_ref[0] + route[:, 3:4] * buf_ref[1]


def _combine(dest3, x1, route, y):
    n = x1.shape[0]
    tm = dest3.shape[2] // 2
    row = lambda w: pl.BlockSpec((tm, w), lambda t: (t, 0))
    return pl.pallas_call(
        _combine_kernel,
        grid=(n // tm,),
        in_specs=[pl.BlockSpec((1, 1, 2 * tm), lambda t: (t, 0, 0), memory_space=pltpu.SMEM),
                  row(D_MODEL), row(LANES), pl.BlockSpec(memory_space=pl.ANY)],
        out_specs=row(D_MODEL),
        out_shape=jax.ShapeDtypeStruct((n, D_MODEL), F32),
        scratch_shapes=[pltpu.VMEM((2, tm, D_MODEL), F32), pltpu.SemaphoreType.DMA(())],
        compiler_params=_params(1),
        name="moe_combine",
    )(dest3, x1, route, y)


def _moe(x1, hn, route, wg, wu, wd, layer):
    n = x1.shape[0]
    tm = min(ROW_TILE, n)
    n_blocks = -(-(2 * n + N_EXPERTS * (MOE_BLOCK - 1)) // MOE_BLOCK)
    assert n_blocks <= META_USED
    dest, meta = _positions(route)
    meta = meta.reshape(-1)
    dest3 = dest[:, :2].reshape(n // tm, 1, 2 * tm)
    xs = _dispatch(meta, dest3, hn, n_blocks * MOE_BLOCK)
    y = _experts(meta, xs, wg, wu, wd, layer)
    return _combine(dest3, x1, route, y)


def _rope_tables(seq):
    inv_freq = 1.0 / (ROPE_THETA ** (jnp.arange(0, HEAD_DIM, 2, dtype=F32) / HEAD_DIM))
    ang = jnp.arange(seq, dtype=F32)[:, None] * inv_freq[None, :]
    cos, sin = jnp.cos(ang), jnp.sin(ang)
    cos_t = jnp.tile(jnp.concatenate([cos, cos], axis=-1), (1, N_HEADS))
    sin_t = jnp.tile(jnp.concatenate([-sin, sin], axis=-1), (1, N_HEADS))
    return cos_t, sin_t


def _to_t(v, batch, n_chunk):
    tokens = v.shape[0]
    v = v.reshape(tokens, N_HEADS, -1)
    pad = jnp.zeros((tokens, N_HEADS, V_PAD), v.dtype).at[:, :, 0].set(1)
    v = jnp.concatenate([v, pad], axis=2).reshape(tokens, -1)
    feat = v.shape[1]
    return (v.reshape(batch, n_chunk, ROW_TILE, feat).transpose(0, 1, 3, 2)
            .reshape(batch * n_chunk, feat, ROW_TILE))


def _from_t(o_t):
    b, feat, seq = o_t.shape
    return o_t.transpose(0, 2, 1).reshape(b * seq, feat)


def kernel(x, norm_attn, w_in, qk_gain, idx_k_gain, diff_lambda, diff_subln_gain, w_proj_a, w_proj_b, w_proj_c, w_out, norm_ffn, w_group, b_group, w_router, b_router, w_e_gate, w_e_up, w_e_down):
    batch, seq, d = x.shape
    assert d == D_MODEL and seq % (2 * ROW_TILE) == 0 and ROW_TILE == MOBA_BLOCK
    n = batch * seq
    nq = seq // ROW_TILE
    depth = w_in.shape[0]
    cos_t, sin_t = _rope_tables(seq)
    x2 = x.reshape(n, d)
    for l in range(depth):
        w_pad = jnp.concatenate(
            [w_in[l][:, :KW_SRC], jnp.zeros((d, LANES - IDX_DIM - IDX_HEADS), F32), w_in[l][:, KW_SRC:]],
            axis=1).astype(BF16)
        gains = jnp.tile(qk_gain[l][jnp.array([0, 1, 2, 3, 4, 4, 5, 5])], (1, N_HEADS))
        kgain = jnp.pad(idx_k_gain[l], (0, LANES - IDX_DIM))[None, :]
        (qa, ka, va, qb, kb, vb, qi, ki, wi, q1, q2, k1, k2, vc, sg, kmean) = _project(
            x2, norm_attn[l][None, :], w_pad, cos_t, sin_t, gains, kgain, seq)

        feat_major = lambda t: t.reshape(batch, seq, t.shape[1]).transpose(0, 2, 1)
        o_a = _from_t(_moba(feat_major(qa), ka, _to_t(va, batch, nq), kmean.reshape(batch, nq, HW),
                            batch, seq))
        o_b = _from_t(_dsa(feat_major(qi), wi, ki, feat_major(qb), kb, _to_t(vb, batch, nq), batch, seq))
        lam_init = 0.8 - 0.6 * math.exp(-0.3 * l)
        o_c = _from_t(_diff(feat_major(q1), feat_major(q2), k1, k2, _to_t(vc, batch, nq), diff_lambda[l],
                            diff_subln_gain[l][:, None], lam_init, batch, seq))

        w_r = jnp.concatenate([w_router[l], w_group[l],
                               jnp.zeros((d, LANES - N_EXPERTS - N_GROUPS), F32)], axis=1)
        wr_hi = w_r.astype(BF16)
        wr_lo = (w_r - wr_hi.astype(F32)).astype(BF16)
        b_r = jnp.concatenate([b_router[l], b_group[l],
                               jnp.zeros((LANES - N_EXPERTS - N_GROUPS,), F32)])[None, :]
        x1, hn, route = _merge(x2, o_a, o_b, o_c, sg, w_proj_a[l].astype(BF16), w_proj_b[l].astype(BF16),
                            w_proj_c[l].astype(BF16), w_out[l].astype(BF16), norm_ffn[l][None, :],
                            wr_hi, wr_lo, b_r)
        x2 = _moe(x1, hn, route, w_e_gate, w_e_up, w_e_down, l)
    return x2.reshape(batch, seq, d)
```

```python
import functools
import math

import jax
import jax.numpy as jnp
from jax import lax
from jax.experimental import pallas as pl
from jax.experimental.pallas import tpu as pltpu

F32 = jnp.float32
BF16 = jnp.bfloat16

D_MODEL = 1024
HEAD_DIM = 64
ROPE_THETA = 10000.0
EPS = 1e-6
N_HEADS = 4
MOBA_BLOCK = 256
MOBA_TOPK = 3
IDX_HEADS = 8
IDX_DIM = 64
DSA_TOPK_MAX = 256
C_VDIM = 2 * HEAD_DIM
N_GROUPS = 4
EXPERTS_PER_GROUP = 8
N_EXPERTS = N_GROUPS * EXPERTS_PER_GROUP
D_EXPERT = 512

HW = N_HEADS * HEAD_DIM
LANES = 128
ROW_TILE = 256
DENSE_TILE = 512
VMEM_LIMIT = 56 * 1024 * 1024

_SEG = {}
_off = 0
for _name, _w in (("qa", HW), ("ka", HW), ("va", HW), ("qb", HW), ("kb", HW), ("vb", HW),
                  ("qi", IDX_HEADS * IDX_DIM), ("kw", LANES), ("q1", HW), ("q2", HW), ("k1", HW),
                  ("k2", HW), ("vc", N_HEADS * C_VDIM), ("ga", D_MODEL), ("gb", D_MODEL),
                  ("gc", D_MODEL)):
    _SEG[_name] = (_off, _w)
    _off += _w
D_IN_PAD = _off
KW_SRC = 6 * HW + IDX_HEADS * IDX_DIM + IDX_DIM + IDX_HEADS

NEG_BIG = -1e30
M_FLOOR = -1e20
INT_MIN = -(2 ** 31)
LOG2E = math.log2(math.e)
Q_SCALE = HEAD_DIM ** -0.5 * LOG2E
V_PAD = 16
NT_DIMS = (((1,), (1,)), ((), ()))


def _params(n_axes):
    return pltpu.CompilerParams(dimension_semantics=("arbitrary",) * n_axes,
                                vmem_limit_bytes=VMEM_LIMIT)


def _dot(a, b):
    return jnp.dot(a, b, preferred_element_type=F32)


def _dot_nt(a, b):
    return lax.dot_general(a, b, NT_DIMS, preferred_element_type=F32)


def _split_bf16(a):
    hi = a.astype(BF16)
    return hi, (a - hi.astype(F32)).astype(BF16)


def _swap_halves(y, width):
    lane = lax.broadcasted_iota(jnp.int32, y.shape, 1)
    first = (lane % HEAD_DIM) < (HEAD_DIM // 2)
    return jnp.where(first, pltpu.roll(y, width - HEAD_DIM // 2, 1), pltpu.roll(y, HEAD_DIM // 2, 1))


def _proj_kernel(x_ref, g_ref, w_ref, cos_ref, sin_ref, gain_ref, kgain_ref,
                 qa_ref, ka_ref, va_ref, qb_ref, kb_ref, vb_ref, qi_ref, ki_ref, wi_ref,
                 q1_ref, q2_ref, k1_ref, k2_ref, vc_ref, sg_ref, kmean_ref):
    x = x_ref[...]
    ms = jnp.mean(x * x, axis=-1, keepdims=True)
    h = (x * lax.rsqrt(ms + EPS) * g_ref[...]).astype(BF16)
    cos = cos_ref[...]
    sin = sin_ref[...]
    r = lax.broadcasted_iota(jnp.int32, (HW, HW), 0) // HEAD_DIM
    c = lax.broadcasted_iota(jnp.int32, (HW, HW), 1) // HEAD_DIM
    head_ones = (r == c).astype(BF16)

    def seg(name, lo=0, width=None):
        off, w = _SEG[name]
        width = w if width is None else width
        return _dot(h, w_ref[:, off + lo:off + lo + width])

    def rope(y):
        return y * cos + _swap_halves(y, HW) * sin

    def norm_rope(t, gain_row):
        hi, lo = _split_bf16(t * t)
        ss = _dot(hi, head_ones) + _dot(lo, head_ones)
        yn = t * lax.rsqrt(ss * (1.0 / HEAD_DIM) + EPS) * gain_ref[gain_row:gain_row + 1, :]
        return rope(yn)

    qa_ref[...] = (norm_rope(seg("qa"), 0) * Q_SCALE).astype(BF16)
    ka = norm_rope(seg("ka"), 1)
    ka_ref[...] = ka.astype(BF16)
    for blk in range(ka.shape[0] // MOBA_BLOCK):
        kmean_ref[blk] = jnp.mean(ka[blk * MOBA_BLOCK:(blk + 1) * MOBA_BLOCK], axis=0, keepdims=True)
    va_ref[...] = seg("va").astype(BF16)
    qb_ref[...] = (norm_rope(seg("qb"), 2) * Q_SCALE).astype(BF16)
    kb_ref[...] = norm_rope(seg("kb"), 3).astype(BF16)
    vb_ref[...] = seg("vb").astype(BF16)
    for half in range(2):
        qi_ref[:, half * HW:(half + 1) * HW] = rope(seg("qi", half * HW, HW)).astype(BF16)

    t = seg("kw")
    lane = lax.broadcasted_iota(jnp.int32, t.shape, 1)
    is_k = lane < IDX_DIM
    kms = jnp.sum(jnp.where(is_k, t * t, 0.0), axis=-1, keepdims=True) * (1.0 / IDX_DIM)
    kn = t * lax.rsqrt(kms + EPS) * kgain_ref[...]
    kr = kn * cos[:, :LANES] + _swap_halves(kn, LANES) * sin[:, :LANES]
    ki_ref[...] = kr[:, :IDX_DIM].astype(BF16)
    w_scale = (IDX_HEADS ** -0.5) * (IDX_DIM ** -0.5)
    wi_ref[...] = jnp.where(lane < IDX_HEADS, pltpu.roll(t, LANES - IDX_DIM, 1) * w_scale, 0.0)

    q1_ref[...] = (norm_rope(seg("q1"), 4) * Q_SCALE).astype(BF16)
    q2_ref[...] = (norm_rope(seg("q2"), 5) * Q_SCALE).astype(BF16)
    k1_ref[...] = norm_rope(seg("k1"), 6).astype(BF16)
    k2_ref[...] = norm_rope(seg("k2"), 7).astype(BF16)
    for half in range(2):
        vc_ref[:, half * HW:(half + 1) * HW] = seg("vc", half * HW, HW).astype(BF16)
    for gi, name in enumerate(("ga", "gb", "gc")):
        for part in range(D_MODEL // 512):
            g = seg(name, part * 512, 512)
            lo = gi * D_MODEL + part * 512
            sg_ref[:, lo:lo + 512] = (1.0 / (1.0 + jnp.exp(-g))).astype(BF16)


def _project(x2, norm_g, w_pad, cos_t, sin_t, gains, kgain, seq):
    n = x2.shape[0]
    tm = min(DENSE_TILE, seq)
    assert seq % tm == 0 and tm % MOBA_BLOCK == 0
    n_pos = seq // tm
    row = lambda w: pl.BlockSpec((tm, w), lambda i: (i, 0))
    const = lambda shape: pl.BlockSpec(shape, lambda i: (0,) * len(shape))
    out_widths = [HW] * 6 + [IDX_HEADS * IDX_DIM, IDX_DIM, LANES] + [HW] * 4 + [N_HEADS * C_VDIM, 3 * D_MODEL]
    out_dtypes = [BF16] * 8 + [F32] + [BF16] * 6
    out_shape = [jax.ShapeDtypeStruct((n, w), dt) for w, dt in zip(out_widths, out_dtypes)]
    out_shape.append(jax.ShapeDtypeStruct((n // MOBA_BLOCK, 1, HW), F32))
    out_specs = [row(w) for w in out_widths] + [pl.BlockSpec((tm // MOBA_BLOCK, 1, HW), lambda i: (i, 0, 0))]
    return pl.pallas_call(
        _proj_kernel,
        grid=(n // tm,),
        in_specs=[row(D_MODEL), const((1, D_MODEL)),
                  pl.BlockSpec((D_MODEL, D_IN_PAD), lambda i: (0, 0), pipeline_mode=pl.Buffered(1)),
                  pl.BlockSpec((tm, HW), lambda i: (i % n_pos, 0)),
                  pl.BlockSpec((tm, HW), lambda i: (i % n_pos, 0)),
                  const((8, HW)), const((1, LANES))],
        out_specs=out_specs,
        out_shape=out_shape,
        compiler_params=_params(1),
        name="proj",
    )(x2, norm_g, w_pad, cos_t, sin_t, gains, kgain)


def _online_update(parts, ms, acc_ref):
    ps, out = [], []
    for c, tiles in enumerate(parts):
        m_new = ms[c]
        for s, _, ok in tiles:
            smax = jnp.max(s, axis=0, keepdims=True)
            m_new = jnp.maximum(m_new, smax if ok is None else jnp.where(ok, smax, NEG_BIG))
        m_eff = jnp.maximum(m_new, M_FLOOR)
        out.append(m_new)
        probs = [jnp.exp2(s - (m_eff if ok is None else jnp.where(ok, m_eff, -NEG_BIG))).astype(BF16)
                 for s, _, ok in tiles]
        ps.append((jnp.exp2(ms[c] - m_new), probs))
    for c, (alpha, probs) in enumerate(ps):
        acc = alpha * acc_ref[c]
        for (_, vt, _), p in zip(parts[c], probs):
            acc = acc + _dot(vt, p)
        acc_ref[c] = acc
    return out


def _init_max(n_chains, tq):
    return tuple(jnp.full((1, tq), NEG_BIG, F32) for _ in range(n_chains))


def _with_ones_row(vt):
    row = lax.broadcasted_iota(jnp.int32, (V_PAD, vt.shape[1]), 0)
    return jnp.concatenate([vt, jnp.where(row == 0, 1.0, 0.0).astype(vt.dtype)], axis=0)


def _normalized(acc_ref, c, dv):
    acc = acc_ref[c]
    return acc[:dv] / acc[dv:dv + 1]


def _head_slice(h, width=HEAD_DIM):
    return slice(h * width, (h + 1) * width)


def _moba_kernel(qt_ref, k_ref, vt_ref, kmean_ref, o_ref, acc_ref, *, n_sel):
    i = pl.program_id(1)
    blk = MOBA_BLOCK
    nb = kmean_ref.shape[1]
    km = kmean_ref[0]
    brow = lax.broadcasted_iota(jnp.int32, (nb, blk), 0)
    browf = brow.astype(F32)
    causal = (lax.broadcasted_iota(jnp.int32, (blk, 1), 0)
              <= lax.broadcasted_iota(jnp.int32, (1, blk), 1))
    qts = [qt_ref[0, _head_slice(h), :] for h in range(N_HEADS)]
    sels = []
    for h in range(N_HEADS):
        km_hi, km_lo = _split_bf16(km[:, _head_slice(h)])
        gate = _dot(km_hi, qts[h]) + _dot(km_lo, qts[h])
        gate = jnp.where(brow < i, gate, -jnp.inf)
        sel = jnp.zeros((nb, blk), F32)
        for _ in range(n_sel):
            gm = jnp.max(gate, axis=0, keepdims=True)
            is_m = (gate == gm) & (gm > -jnp.inf)
            first = jnp.min(jnp.where(is_m, browf, float(nb)), axis=0, keepdims=True)
            pick = browf == first
            sel = jnp.where(pick, 1.0, sel)
            gate = jnp.where(pick, -jnp.inf, gate)
        sels.append(sel)
    acc_ref[...] = jnp.zeros(acc_ref.shape, F32)

    def tile(j, h, mask=None, seen=False):
        rows = pl.ds(pl.multiple_of(j * blk, blk), blk)
        s = _dot(k_ref[rows, _head_slice(h)], qts[h])
        if mask is not None:
            s = jnp.where(mask, s, NEG_BIG)
        ok = jnp.sum(jnp.where(brow == j, sels[h], 0.0), axis=0, keepdims=True) > 0.0 if seen else None
        return s, _with_ones_row(vt_ref[j, _head_slice(h), :]), ok

    def pair(j2, ms):
        parts = [[tile(2 * j2, h, seen=True), tile(2 * j2 + 1, h, seen=True)] for h in range(N_HEADS)]
        return tuple(_online_update(parts, ms, acc_ref))

    ms = lax.fori_loop(0, i // 2, pair, _init_max(N_HEADS, blk))

    @pl.when(i % 2 == 1)
    def _():
        _online_update([[tile(i - 1, h, seen=True), tile(i, h, mask=causal)] for h in range(N_HEADS)],
                       ms, acc_ref)

    @pl.when(i % 2 == 0)
    def _():
        _online_update([[tile(i, h, mask=causal)] for h in range(N_HEADS)], ms, acc_ref)

    for h in range(N_HEADS):
        o_ref[0, _head_slice(h), :] = _normalized(acc_ref, h, HEAD_DIM).astype(o_ref.dtype)


def _moba(qt, k, vt, kmean, batch, seq):
    blk = MOBA_BLOCK
    nb = seq // blk
    n_sel = min(MOBA_TOPK, nb - 1)
    tspec = pl.BlockSpec((1, HW, blk), lambda b, i: (b, 0, i))
    return pl.pallas_call(
        functools.partial(_moba_kernel, n_sel=n_sel),
        grid=(batch, nb),
        in_specs=[tspec,
                  pl.BlockSpec((seq, HW), lambda b, i: (b, 0)),
                  pl.BlockSpec((nb, vt.shape[1], blk), lambda b, i: (b, 0, 0)),
                  pl.BlockSpec((1, nb, HW), lambda b, i: (b, 0, 0))],
        out_specs=tspec,
        out_shape=jax.ShapeDtypeStruct((batch, HW, seq), BF16),
        scratch_shapes=[pltpu.VMEM((N_HEADS, HEAD_DIM + V_PAD, blk), F32)],
        compiler_params=_params(2),
        name="moba",
    )(qt, k, vt, kmean)


def _bit_planes(words):
    a = list(words)
    assert len(a) == 32
    mask, j = 0x0000FFFF, 16
    while j:
        k = 0
        while k < 32:
            t = (a[k] ^ (a[k + j] >> j)) & mask
            a[k] = a[k] ^ t
            a[k + j] = a[k + j] ^ (t << j)
            k = (k + j + 1) & ~j
        j >>= 1
        mask = (mask ^ (mask << j)) & 0xFFFFFFFF
    return a


def _dsa_kernel(qit_ref, wi_ref, ki_ref, qt_ref, k_ref, vt_ref, o_ref, key_ref, plane_ref, thr_ref, nge_ref,
                acc_ref, *, n_keep, idx_bits):
    i = pl.program_id(1)
    blk = ROW_TILE
    n_chunk = i + 1
    n_planes, n_slots = plane_ref.shape[:2]

    @pl.when(i == 0)
    def _():
        plane_ref[:, 1:] = jnp.zeros((n_planes, n_slots - 1) + plane_ref.shape[2:], jnp.int32)

    w_t = wi_ref[...].T
    qpos = i * blk + lax.broadcasted_iota(jnp.int32, (1, blk), 1)
    krow = lax.broadcasted_iota(jnp.int32, (blk, 1), 0)

    def score_chunk(c, carry):
        kc = ki_ref[pl.ds(pl.multiple_of(c * blk, blk), blk), :]
        lgs = [_dot(kc, qit_ref[0, _head_slice(h, IDX_DIM), :]) for h in range(IDX_HEADS)]
        sc = jnp.zeros((blk, blk), F32)
        for h in range(IDX_HEADS):
            sc = sc + w_t[h:h + 1, :] * jnp.maximum(lgs[h], 0.0)
        sc = sc + 0.0
        bits = pltpu.bitcast(sc, jnp.int32)
        key = jnp.where(bits < 0, bits ^ 0x7FFFFFFF, bits)
        key = jnp.where(c * blk + krow <= qpos, key, INT_MIN)
        key_ref[c] = key
        words = (key ^ INT_MIN).reshape(n_planes, blk // n_planes, blk)
        planes = _bit_planes([words[v] for v in range(n_planes)])
        for b in range(n_planes):
            plane_ref[b, c] = planes[n_planes - 1 - b]
        return carry

    lax.fori_loop(0, n_chunk, score_chunk, 0)

    @pl.when(n_chunk % 2 == 1)
    def _():
        key_ref[n_chunk] = jnp.full((blk, blk), INT_MIN, jnp.int32)

    def count(pred):
        def body(c2, acc):
            for c in (2 * c2, 2 * c2 + 1):
                hit = jnp.where(pred(c, key_ref[c]), 1.0, 0.0)
                acc = acc + jnp.sum(hit.reshape(blk // 8, 8, blk), axis=0)
            return acc
        acc = lax.fori_loop(0, (n_chunk + 1) // 2, body, jnp.zeros((8, blk), F32))
        return jnp.sum(acc, axis=0, keepdims=True)

    keep = float(n_keep)

    def n_set(words):
        per_row = jnp.sum(lax.population_count(words), axis=0)
        return jnp.sum(per_row.astype(F32), axis=0, keepdims=True)

    def select(n_used, lanes):
        def select_bit(t, carry):
            eq, gt, prefix = carry
            b = n_planes - 1 - t
            plane = plane_ref[b, 0:n_used, :, lanes]
            cand = gt | (eq & plane)
            take = n_set(cand) >= keep
            eq = eq & jnp.where(take, plane, ~plane)
            gt = jnp.where(take, gt, cand)
            prefix = prefix | jnp.where(take, jnp.left_shift(jnp.int32(1), b), 0)
            return eq, gt, prefix

        shape = (n_used, plane_ref.shape[2], LANES)
        state = (jnp.full(shape, -1, jnp.int32), jnp.zeros(shape, jnp.int32), jnp.zeros((1, LANES), jnp.int32))
        eq, gt, prefix = lax.fori_loop(0, n_planes, select_bit, state)
        return prefix ^ INT_MIN, n_set(eq | gt)

    half_slots = n_slots // 2
    for n_used, wanted in ((half_slots, n_chunk <= half_slots), (n_slots, n_chunk > half_slots)):
        @pl.when(wanted)
        def _():
            for lo in range(0, blk, LANES):
                thr_ref[:, lo:lo + LANES], nge_ref[:, lo:lo + LANES] = select(n_used, slice(lo, lo + LANES))

    thr = thr_ref[...]
    zero = jnp.zeros((1, blk), jnp.int32)

    tied = (nge_ref[...] > keep) & (thr > INT_MIN)

    @pl.when(jnp.max(jnp.where(tied, 1.0, 0.0)) > 0.0)
    def _():
        need = keep - count(lambda c, k: k > thr)

        def idx_step(t, last):
            cand = last + jnp.left_shift(jnp.int32(1), idx_bits - 1 - t)
            n = count(lambda c, k: (k == thr) & (c * blk + krow < cand))
            return jnp.where(n < need, cand, last)

        last = lax.fori_loop(0, idx_bits, idx_step, zero)

        def demote(c, carry):
            k = key_ref[c]
            hit = (k == thr) & (c * blk + krow > last) & tied
            key_ref[c] = jnp.where(hit, thr - 1, k)
            return carry

        lax.fori_loop(0, n_chunk, demote, 0)

    thr_eff = jnp.maximum(thr, INT_MIN + 1)
    qts = [qt_ref[0, _head_slice(h), :] for h in range(N_HEADS)]
    acc_ref[...] = jnp.zeros(acc_ref.shape, F32)

    def pair(c2, ms):
        parts = [[] for _ in range(N_HEADS)]
        for c in (2 * c2, 2 * c2 + 1):
            rows = pl.ds(pl.multiple_of(c * blk, blk), blk)
            allowed = key_ref[c] >= thr_eff
            for h in range(N_HEADS):
                s = jnp.where(allowed, _dot(k_ref[rows, _head_slice(h)], qts[h]), NEG_BIG)
                parts[h].append((s, _with_ones_row(vt_ref[c, _head_slice(h), :]), None))
        return tuple(_online_update(parts, ms, acc_ref))

    lax.fori_loop(0, (n_chunk + 1) // 2, pair, _init_max(N_HEADS, blk))
    for h in range(N_HEADS):
        o_ref[0, _head_slice(h), :] = _normalized(acc_ref, h, HEAD_DIM).astype(o_ref.dtype)


def _dsa(qit, wi, ki, qt, k, vt, batch, seq):
    blk = ROW_TILE
    nq = seq // blk
    n_keep = min(DSA_TOPK_MAX, seq // 4)
    tspec = lambda w: pl.BlockSpec((1, w, blk), lambda b, i: (b, 0, i))
    full = lambda w: pl.BlockSpec((seq, w), lambda b, i: (b, 0))
    return pl.pallas_call(
        functools.partial(_dsa_kernel, n_keep=n_keep, idx_bits=(seq - 1).bit_length()),
        grid=(batch, nq),
        in_specs=[tspec(IDX_HEADS * IDX_DIM), pl.BlockSpec((blk, LANES), lambda b, i: (b * nq + i, 0)),
                  full(IDX_DIM), tspec(HW), full(HW),
                  pl.BlockSpec((nq, vt.shape[1], blk), lambda b, i: (b, 0, 0))],
        out_specs=tspec(HW),
        out_shape=jax.ShapeDtypeStruct((batch, HW, seq), BF16),
        scratch_shapes=[pltpu.VMEM((nq + nq % 2, blk, blk), jnp.int32),
                        pltpu.VMEM((32, nq, blk // 32, blk), jnp.int32),
                        pltpu.VMEM((1, blk), jnp.int32), pltpu.VMEM((1, blk), F32),
                        pltpu.VMEM((N_HEADS, HEAD_DIM + V_PAD, blk), F32)],
        compiler_params=_params(2),
        name="dsa",
    )(qit, wi, ki, qt, k, vt)


DIFF_GROUP = 4


def _diff_kernel(q1t_ref, q2t_ref, k1_ref, k2_ref, vt_ref, dl_ref, gain_ref, o_ref, acc_ref, *, lam_init):
    i = pl.program_id(1)
    blk = ROW_TILE
    dl = dl_ref[...]
    lam = (jnp.exp(jnp.sum(dl[0:1] * dl[1:2], axis=-1, keepdims=True))
           - jnp.exp(jnp.sum(dl[2:3] * dl[3:4], axis=-1, keepdims=True)) + lam_init)
    causal = (lax.broadcasted_iota(jnp.int32, (blk, 1), 0)
              <= lax.broadcasted_iota(jnp.int32, (1, blk), 1))
    maps = ((q1t_ref, k1_ref), (q2t_ref, k2_ref))

    for h0 in range(0, N_HEADS, DIFF_GROUP):
        chains = [(h, mp) for h in range(h0, h0 + DIFF_GROUP) for mp in range(2)]
        qts = [maps[mp][0][0, _head_slice(h), :] for h, mp in chains]
        acc_ref[...] = jnp.zeros(acc_ref.shape, F32)

        def step(j, ms, diag):
            rows = pl.ds(pl.multiple_of(j * blk, blk), blk)
            ss = [_dot(maps[mp][1][rows, _head_slice(h)], qts[c]) for c, (h, mp) in enumerate(chains)]
            if diag:
                ss = [jnp.where(causal, s, NEG_BIG) for s in ss]
            parts = [[(s, _with_ones_row(vt_ref[j, _head_slice(h, C_VDIM), :]), None)]
                     for s, (h, _) in zip(ss, chains)]
            return tuple(_online_update(parts, ms, acc_ref))

        stats = lax.fori_loop(0, i, lambda j, st: step(j, st, False), _init_max(len(chains), blk))
        step(i, stats, True)
        for g in range(DIFF_GROUP):
            h = h0 + g
            o = _normalized(acc_ref, 2 * g, C_VDIM) - lam * _normalized(acc_ref, 2 * g + 1, C_VDIM)
            ms = jnp.mean(o * o, axis=0, keepdims=True)
            o = o * lax.rsqrt(ms + EPS) * gain_ref[...] * (1.0 - lam_init)
            o_ref[0, _head_slice(h, C_VDIM), :] = o.astype(o_ref.dtype)


def _diff(q1t, q2t, k1, k2, vt, dl, gain, lam_init, batch, seq):
    blk = ROW_TILE
    nq = seq // blk
    tspec = lambda w: pl.BlockSpec((1, w, blk), lambda b, i: (b, 0, i))
    kspec = pl.BlockSpec((seq, HW), lambda b, i: (b, 0))
    vw = N_HEADS * C_VDIM
    return pl.pallas_call(
        functools.partial(_diff_kernel, lam_init=lam_init),
        grid=(batch, nq),
        in_specs=[tspec(HW), tspec(HW), kspec, kspec,
                  pl.BlockSpec((nq, vt.shape[1], blk), lambda b, i: (b, 0, 0)),
                  pl.BlockSpec((4, HEAD_DIM), lambda b, i: (0, 0)),
                  pl.BlockSpec((C_VDIM, 1), lambda b, i: (0, 0))],
        out_specs=tspec(vw),
        out_shape=jax.ShapeDtypeStruct((batch, vw, seq), BF16),
        scratch_shapes=[pltpu.VMEM((2 * DIFF_GROUP, C_VDIM + V_PAD, blk), F32)],
        compiler_params=_params(2),
        name="diff",
    )(q1t, q2t, k1, k2, vt, dl, gain)


def _merge_kernel(x_ref, oa_ref, ob_ref, oc_ref, sg_ref, wa_ref, wb_ref, wc_ref, wo_ref, g_ref,
                  wr_hi_ref, wr_lo_ref, br_ref, x1_ref, hn_ref, route_ref):
    merged = (sg_ref[:, 0:D_MODEL].astype(F32) * _dot(oa_ref[...], wa_ref[...])
              + sg_ref[:, D_MODEL:2 * D_MODEL].astype(F32) * _dot(ob_ref[...], wb_ref[...])
              + sg_ref[:, 2 * D_MODEL:3 * D_MODEL].astype(F32) * _dot(oc_ref[...], wc_ref[...]))
    x1 = x_ref[...] + _dot(merged.astype(BF16), wo_ref[...])
    x1_ref[...] = x1
    ms = jnp.mean(x1 * x1, axis=-1, keepdims=True)
    hn = x1 * lax.rsqrt(ms + EPS) * g_ref[...]
    hn_ref[...] = hn

    hi, lo = _split_bf16(hn)
    lg = (_dot(hi, wr_hi_ref[...]) + _dot(lo, wr_hi_ref[...]) + _dot(hi, wr_lo_ref[...])
          + br_ref[...])
    lane = lax.broadcasted_iota(jnp.int32, lg.shape, 1)
    lanef = lane.astype(F32)
    far = float(LANES)
    is_g = (lane >= N_EXPERTS) & (lane < N_EXPERTS + N_GROUPS)
    gl = jnp.where(is_g, lg, -jnp.inf)
    gmax = jnp.max(gl, axis=-1, keepdims=True)
    gidx = jnp.min(jnp.where(gl == gmax, lanef, far), axis=-1, keepdims=True) - float(N_EXPERTS)
    g_w = 1.0 / jnp.sum(jnp.where(is_g, jnp.exp(gl - gmax), 0.0), axis=-1, keepdims=True)
    in_group = (lane < N_EXPERTS) & ((lane // EXPERTS_PER_GROUP).astype(F32) == gidx)
    el = jnp.where(in_group, lg, -jnp.inf)
    e1 = jnp.max(el, axis=-1, keepdims=True)
    i1 = jnp.min(jnp.where(el == e1, lanef, far), axis=-1, keepdims=True)
    el2 = jnp.where(lanef == i1, -jnp.inf, el)
    e2 = jnp.max(el2, axis=-1, keepdims=True)
    i2 = jnp.min(jnp.where(el2 == e2, lanef, far), axis=-1, keepdims=True)
    t = jnp.exp(e2 - e1)
    w1 = g_w / (1.0 + t)
    w2 = g_w * t / (1.0 + t)
    route_ref[...] = jnp.where(lane == 0, i1, jnp.where(lane == 1, i2, jnp.where(lane == 2, w1, jnp.where(lane == 3, w2, 0.0))))


def _merge(x2, oa, ob, oc, sg, wa, wb, wc, wo, norm_g, wr_hi, wr_lo, br):
    n = x2.shape[0]
    tm = min(DENSE_TILE, n)
    row = lambda w: pl.BlockSpec((tm, w), lambda i: (i, 0))
    const = lambda a: pl.BlockSpec(a.shape, lambda i: (0, 0))
    return pl.pallas_call(
        _merge_kernel,
        grid=(n // tm,),
        in_specs=[row(D_MODEL), row(HW), row(HW), row(N_HEADS * C_VDIM), row(3 * D_MODEL),
                  const(wa), const(wb), const(wc), const(wo), const(norm_g), const(wr_hi),
                  const(wr_lo), const(br)],
        out_specs=[row(D_MODEL), row(D_MODEL), row(LANES)],
        out_shape=[jax.ShapeDtypeStruct((n, D_MODEL), F32), jax.ShapeDtypeStruct((n, D_MODEL), F32),
                   jax.ShapeDtypeStruct((n, LANES), F32)],
        compiler_params=_params(1),
        name="merge",
    )(x2, oa, ob, oc, sg, wa, wb, wc, wo, norm_g, wr_hi, wr_lo, br)


MOE_BLOCK = 256
META_ROWS = 8
META_USED = 3 * LANES
META_END = 4 * LANES
META_PADDED = 5 * LANES


def _lane_prefix_sum(x):
    lane = lax.broadcasted_iota(jnp.int32, x.shape, 1)
    shift = 1
    while shift < LANES:
        x = x + jnp.where(lane >= shift, pltpu.roll(x, shift, 1), 0.0)
        shift *= 2
    return x


def _positions_kernel(route_ref, dest_ref, meta_ref, cnt_ref, base_ref):
    phase = pl.program_id(0)
    t = pl.program_id(1)
    tm = route_ref.shape[0]
    route = route_ref[...]
    lane = lax.broadcasted_iota(jnp.int32, route.shape, 1)
    lanef = lane.astype(F32)
    e1 = route[:, 0:1]
    e2 = route[:, 1:2]
    uses = jnp.where((lanef == e1) | (lanef == e2), 1.0, 0.0)
    tile_cnt = jnp.sum(uses, axis=0, keepdims=True)

    @pl.when((phase == 0) & (t == 0))
    def _():
        cnt_ref[...] = jnp.zeros(cnt_ref.shape, F32)

    @pl.when(phase == 0)
    def _():
        cnt_ref[...] += tile_cnt

    @pl.when((phase == 1) & (t == 0))
    def _():
        cnt = cnt_ref[...]
        padded = jnp.floor((cnt + (MOE_BLOCK - 1)) * (1.0 / MOE_BLOCK)) * MOE_BLOCK
        end = _lane_prefix_sum(padded)
        base_ref[...] = end - padded
        cnt_ref[...] = jnp.zeros(cnt_ref.shape, F32)
        lane1 = lax.broadcasted_iota(jnp.int32, (1, LANES), 1)
        row = lax.broadcasted_iota(jnp.int32, (META_ROWS, LANES), 0)
        col = lax.broadcasted_iota(jnp.int32, (META_ROWS, LANES), 1)
        first_row = ((row * LANES + col) * MOE_BLOCK).astype(F32)
        owner = jnp.zeros((META_ROWS, LANES), F32)
        for e in range(N_EXPERTS):
            end_e = jnp.sum(jnp.where(lane1 == e, end, 0.0), axis=-1, keepdims=True)
            owner = owner + jnp.where(end_e <= first_row, 1.0, 0.0)
        owner = jnp.minimum(owner, float(N_EXPERTS - 1))
        used = jnp.sum(jnp.where(lane1 == N_EXPERTS - 1, end, 0.0), axis=-1, keepdims=True) * (1.0 / MOE_BLOCK)
        meta = jnp.where(row == META_USED // LANES, used,
                         jnp.where(row == META_END // LANES, end,
                                   jnp.where(row == META_PADDED // LANES, padded, owner)))
        meta_ref[...] = meta.astype(jnp.int32)

    @pl.when(phase == 1)
    def _():
        before = (lax.broadcasted_iota(jnp.int32, (tm, tm), 1)
                  < lax.broadcasted_iota(jnp.int32, (tm, tm), 0)).astype(BF16)
        rank = _dot(before, uses.astype(BF16))
        pos = base_ref[...] + cnt_ref[...] + rank
        d1 = jnp.sum(jnp.where(lanef == e1, pos, 0.0), axis=-1, keepdims=True)
        d2 = jnp.sum(jnp.where(lanef == e2, pos, 0.0), axis=-1, keepdims=True)
        dest_ref[...] = jnp.where(lane == 0, d1, jnp.where(lane == 1, d2, 0.0)).astype(jnp.int32)
        cnt_ref[...] += tile_cnt


def _positions(route):
    n = route.shape[0]
    tm = min(4 * ROW_TILE, n)
    return pl.pallas_call(
        _positions_kernel,
        grid=(2, n // tm),
        in_specs=[pl.BlockSpec((tm, LANES), lambda p, t: (t, 0))],
        out_specs=[pl.BlockSpec((tm, LANES), lambda p, t: (t * p, 0)),
                   pl.BlockSpec((META_ROWS, LANES), lambda p, t: (0, 0))],
        out_shape=[jax.ShapeDtypeStruct((n, LANES), jnp.int32),
                   jax.ShapeDtypeStruct((META_ROWS, LANES), jnp.int32)],
        scratch_shapes=[pltpu.VMEM((1, LANES), F32), pltpu.VMEM((1, LANES), F32)],
        compiler_params=_params(2),
        name="moe_positions",
    )(route)


def _row_copy(src_ref, src_row, dst_ref, dst_row, sem):
    return pltpu.make_async_copy(src_ref.at[pl.ds(src_row, 1), :], dst_ref.at[pl.ds(dst_row, 1), :], sem)


def _dispatch_kernel(meta_ref, dest_ref, hn_ref, xs_ref, zero_ref, sem):
    tm = hn_ref.shape[0]

    @pl.when(pl.program_id(0) == 0)
    def _():
        zero_ref[...] = jnp.zeros(zero_ref.shape, F32)

        def fill(e):
            end = pl.multiple_of(meta_ref[META_END + e], MOE_BLOCK)
            return pltpu.make_async_copy(zero_ref, xs_ref.at[pl.ds(end - MOE_BLOCK, MOE_BLOCK), :], sem)

        for e in range(N_EXPERTS):
            @pl.when(meta_ref[META_PADDED + e] > 0)
            def _():
                fill(e).start()
        for e in range(N_EXPERTS):
            @pl.when(meta_ref[META_PADDED + e] > 0)
            def _():
                fill(e).wait()

        def spare(b):
            return pltpu.make_async_copy(
                zero_ref, xs_ref.at[pl.ds(pl.multiple_of(b * MOE_BLOCK, MOE_BLOCK), MOE_BLOCK), :], sem)

        n_blocks = xs_ref.shape[0] // MOE_BLOCK
        lax.fori_loop(meta_ref[META_USED], n_blocks, lambda b, c: (spare(b).start(), c)[1], 0)
        lax.fori_loop(meta_ref[META_USED], n_blocks, lambda b, c: (spare(b).wait(), c)[1], 0)

    def copies(r):
        return (_row_copy(hn_ref, r, xs_ref, dest_ref[0, 0, 2 * r], sem),
                _row_copy(hn_ref, r, xs_ref, dest_ref[0, 0, 2 * r + 1], sem))

    def start(r, carry):
        for cp in copies(r):
            cp.start()
        return carry

    lax.fori_loop(0, tm, start, 0, unroll=8)
    for _ in range(2):
        pltpu.make_async_copy(hn_ref, xs_ref.at[pl.ds(0, tm), :], sem).wait()


def _dispatch(meta, dest3, hn, n_rows):
    n = hn.shape[0]
    tm = dest3.shape[2] // 2
    return pl.pallas_call(
        _dispatch_kernel,
        grid_spec=pltpu.PrefetchScalarGridSpec(
            num_scalar_prefetch=1,
            grid=(n // tm,),
            in_specs=[pl.BlockSpec((1, 1, 2 * tm), lambda t, m: (t, 0, 0), memory_space=pltpu.SMEM),
                      pl.BlockSpec((tm, D_MODEL), lambda t, m: (t, 0))],
            out_specs=pl.BlockSpec(memory_space=pl.ANY),
            scratch_shapes=[pltpu.VMEM((MOE_BLOCK, D_MODEL), F32), pltpu.SemaphoreType.DMA(())]),
        out_shape=jax.ShapeDtypeStruct((n_rows, D_MODEL), F32),
        compiler_params=_params(1),
        name="moe_dispatch",
    )(meta, dest3, hn)


def _expert_kernel(meta_ref, xs_ref, wg_ref, wu_ref, wd_ref, y_ref, wg_bf, wu_bf, wd_bf):
    b = pl.program_id(0)
    holds_rows = b < meta_ref[META_USED]
    new_expert = (b == 0) | (meta_ref[b] != meta_ref[jnp.maximum(b - 1, 0)])

    @pl.when(holds_rows & new_expert)
    def _():
        wg_bf[...] = wg_ref[0].astype(BF16)
        wu_bf[...] = wu_ref[0].astype(BF16)
        wd_bf[...] = wd_ref[0].astype(BF16)

    @pl.when(holds_rows)
    def _():
        x = xs_ref[...].astype(BF16)
        g = _dot(x, wg_bf[...])
        u = _dot(x, wu_bf[...])
        hid = g * (1.0 / (1.0 + jnp.exp(-g))) * u
        y_ref[...] = _dot(hid.astype(BF16), wd_bf[...])

    @pl.when(jnp.logical_not(holds_rows))
    def _():
        y_ref[...] = jnp.zeros(y_ref.shape, F32)


def _experts(meta, xs, wg, wu, wd, layer):
    n_blocks = xs.shape[0] // MOE_BLOCK
    rows = pl.BlockSpec((MOE_BLOCK, D_MODEL), lambda b, m: (b, 0))
    weight = lambda shape: pl.BlockSpec(
        (None, 1) + shape, lambda b, m: (layer, m[jnp.minimum(b, m[META_USED] - 1)], 0, 0))
    return pl.pallas_call(
        _expert_kernel,
        grid_spec=pltpu.PrefetchScalarGridSpec(
            num_scalar_prefetch=1,
            grid=(n_blocks,),
            in_specs=[rows, weight((D_MODEL, D_EXPERT)), weight((D_MODEL, D_EXPERT)),
                      weight((D_EXPERT, D_MODEL))],
            out_specs=rows,
            scratch_shapes=[pltpu.VMEM((D_MODEL, D_EXPERT), BF16), pltpu.VMEM((D_MODEL, D_EXPERT), BF16),
                            pltpu.VMEM((D_EXPERT, D_MODEL), BF16)]),
        out_shape=jax.ShapeDtypeStruct(xs.shape, F32),
        compiler_params=_params(1),
        name="moe_experts",
    )(meta, xs, wg, wu, wd)


def _combine_kernel(dest_ref, x1_ref, route_ref, y_ref, o_ref, buf_ref, sem):
    tm = x1_ref.shape[0]

    def copies(r):
        return (_row_copy(y_ref, dest_ref[0, 0, 2 * r], buf_ref.at[0], r, sem),
                _row_copy(y_ref, dest_ref[0, 0, 2 * r + 1], buf_ref.at[1], r, sem))

    def start(r, carry):
        for cp in copies(r):
            cp.start()
        return carry

    lax.fori_loop(0, tm, start, 0, unroll=8)
    for slot in range(2):
        pltpu.make_async_copy(y_ref.at[pl.ds(0, tm), :], buf_ref.at[slot], sem).wait()
    route = route_ref[...]
    o_ref[...] = x1_ref[...] + route[:, 2:3] * buf_ref[0] + route[:, 3:4] * buf_ref[1]


def _combine(dest3, x1, route, y):
    n = x1.shape[0]
    tm = dest3.shape[2] // 2
    row = lambda w: pl.BlockSpec((tm, w), lambda t: (t, 0))
    return pl.pallas_call(
        _combine_kernel,
        grid=(n // tm,),
        in_specs=[pl.BlockSpec((1, 1, 2 * tm), lambda t: (t, 0, 0), memory_space=pltpu.SMEM),
                  row(D_MODEL), row(LANES), pl.BlockSpec(memory_space=pl.ANY)],
        out_specs=row(D_MODEL),
        out_shape=jax.ShapeDtypeStruct((n, D_MODEL), F32),
        scratch_shapes=[pltpu.VMEM((2, tm, D_MODEL), F32), pltpu.SemaphoreType.DMA(())],
        compiler_params=_params(1),
        name="moe_combine",
    )(dest3, x1, route, y)


def _moe(x1, hn, route, wg, wu, wd, layer):
    n = x1.shape[0]
    tm = min(ROW_TILE, n)
    n_blocks = -(-(2 * n + N_EXPERTS * (MOE_BLOCK - 1)) // MOE_BLOCK)
    assert n_blocks <= META_USED
    dest, meta = _positions(route)
    meta = meta.reshape(-1)
    dest3 = dest[:, :2].reshape(n // tm, 1, 2 * tm)
    xs = _dispatch(meta, dest3, hn, n_blocks * MOE_BLOCK)
    y = _experts(meta, xs, wg, wu, wd, layer)
    return _combine(dest3, x1, route, y)


def _rope_tables(seq):
    inv_freq = 1.0 / (ROPE_THETA ** (jnp.arange(0, HEAD_DIM, 2, dtype=F32) / HEAD_DIM))
    ang = jnp.arange(seq, dtype=F32)[:, None] * inv_freq[None, :]
    cos, sin = jnp.cos(ang), jnp.sin(ang)
    cos_t = jnp.tile(jnp.concatenate([cos, cos], axis=-1), (1, N_HEADS))
    sin_t = jnp.tile(jnp.concatenate([-sin, sin], axis=-1), (1, N_HEADS))
    return cos_t, sin_t


def _to_t(v, batch, n_chunk):
    feat = v.shape[1]
    return (v.reshape(batch, n_chunk, ROW_TILE, feat).transpose(0, 1, 3, 2)
            .reshape(batch * n_chunk, feat, ROW_TILE))


def _from_t(o_t):
    b, feat, seq = o_t.shape
    return o_t.transpose(0, 2, 1).reshape(b * seq, feat)


def kernel(x, norm_attn, w_in, qk_gain, idx_k_gain, diff_lambda, diff_subln_gain, w_proj_a, w_proj_b, w_proj_c, w_out, norm_ffn, w_group, b_group, w_router, b_router, w_e_gate, w_e_up, w_e_down):
    batch, seq, d = x.shape
    assert d == D_MODEL and seq % (2 * ROW_TILE) == 0 and ROW_TILE == MOBA_BLOCK
    n = batch * seq
    nq = seq // ROW_TILE
    depth = w_in.shape[0]
    cos_t, sin_t = _rope_tables(seq)
    x2 = x.reshape(n, d)
    for l in range(depth):
        w_pad = jnp.concatenate(
            [w_in[l][:, :KW_SRC], jnp.zeros((d, LANES - IDX_DIM - IDX_HEADS), F32), w_in[l][:, KW_SRC:]],
            axis=1).astype(BF16)
        gains = jnp.tile(qk_gain[l][jnp.array([0, 1, 2, 3, 4, 4, 5, 5])], (1, N_HEADS))
        kgain = jnp.pad(idx_k_gain[l], (0, LANES - IDX_DIM))[None, :]
        (qa, ka, va, qb, kb, vb, qi, ki, wi, q1, q2, k1, k2, vc, sg, kmean) = _project(
            x2, norm_attn[l][None, :], w_pad, cos_t, sin_t, gains, kgain, seq)

        feat_major = lambda t: t.reshape(batch, seq, t.shape[1]).transpose(0, 2, 1)
        o_a = _from_t(_moba(feat_major(qa), ka, _to_t(va, batch, nq), kmean.reshape(batch, nq, HW),
                            batch, seq))
        o_b = _from_t(_dsa(feat_major(qi), wi, ki, feat_major(qb), kb, _to_t(vb, batch, nq), batch, seq))
        lam_init = 0.8 - 0.6 * math.exp(-0.3 * l)
        o_c = _from_t(_diff(feat_major(q1), feat_major(q2), k1, k2, _to_t(vc, batch, nq), diff_lambda[l],
                            diff_subln_gain[l][:, None], lam_init, batch, seq))

        w_r = jnp.concatenate([w_router[l], w_group[l],
                               jnp.zeros((d, LANES - N_EXPERTS - N_GROUPS), F32)], axis=1)
        wr_hi = w_r.astype(BF16)
        wr_lo = (w_r - wr_hi.astype(F32)).astype(BF16)
        b_r = jnp.concatenate([b_router[l], b_group[l],
                               jnp.zeros((LANES - N_EXPERTS - N_GROUPS,), F32)])[None, :]
        x1, hn, route = _merge(x2, o_a, o_b, o_c, sg, w_proj_a[l].astype(BF16), w_proj_b[l].astype(BF16),
                            w_proj_c[l].astype(BF16), w_out[l].astype(BF16), norm_ffn[l][None, :],
                            wr_hi, wr_lo, b_r)
        x2 = _moe(x1, hn, route, w_e_gate, w_e_up, w_e_down, l)
    return x2.reshape(batch, seq, d)
```

```python
import functools
import math

import jax
import jax.numpy as jnp
from jax import lax
from jax.experimental import pallas as pl
from jax.experimental.pallas import tpu as pltpu

F32 = jnp.float32
BF16 = jnp.bfloat16

D_MODEL = 1024
HEAD_DIM = 64
ROPE_THETA = 10000.0
EPS = 1e-6
N_HEADS = 4
MOBA_BLOCK = 256
MOBA_TOPK = 3
IDX_HEADS = 8
IDX_DIM = 64
DSA_TOPK_MAX = 256
C_VDIM = 2 * HEAD_DIM
N_GROUPS = 4
EXPERTS_PER_GROUP = 8
N_EXPERTS = N_GROUPS * EXPERTS_PER_GROUP
D_EXPERT = 512

HW = N_HEADS * HEAD_DIM
LANES = 128
ROW_TILE = 256
DENSE_TILE = 512
VMEM_LIMIT = 56 * 1024 * 1024

_SEG = {}
_off = 0
for _name, _w in (("qa", HW), ("ka", HW), ("va", HW), ("qb", HW), ("kb", HW), ("vb", HW),
                  ("qi", IDX_HEADS * IDX_DIM), ("kw", LANES), ("q1", HW), ("q2", HW), ("k1", HW),
                  ("k2", HW), ("vc", N_HEADS * C_VDIM), ("ga", D_MODEL), ("gb", D_MODEL),
                  ("gc", D_MODEL)):
    _SEG[_name] = (_off, _w)
    _off += _w
D_IN_PAD = _off
KW_SRC = 6 * HW + IDX_HEADS * IDX_DIM + IDX_DIM + IDX_HEADS

NEG_BIG = -1e30
M_FLOOR = -1e20
INT_MIN = -(2 ** 31)
LOG2E = math.log2(math.e)
Q_SCALE = HEAD_DIM ** -0.5 * LOG2E
V_PAD = 16
NT_DIMS = (((1,), (1,)), ((), ()))


def _params(n_axes):
    return pltpu.CompilerParams(dimension_semantics=("arbitrary",) * n_axes,
                                vmem_limit_bytes=VMEM_LIMIT)


def _dot(a, b):
    return jnp.dot(a, b, preferred_element_type=F32)


def _dot_nt(a, b):
    return lax.dot_general(a, b, NT_DIMS, preferred_element_type=F32)


def _split_bf16(a):
    hi = a.astype(BF16)
    return hi, (a - hi.astype(F32)).astype(BF16)


def _swap_halves(y, width):
    lane = lax.broadcasted_iota(jnp.int32, y.shape, 1)
    first = (lane % HEAD_DIM) < (HEAD_DIM // 2)
    return jnp.where(first, pltpu.roll(y, width - HEAD_DIM // 2, 1), pltpu.roll(y, HEAD_DIM // 2, 1))


def _proj_kernel(x_ref, g_ref, w_ref, cos_ref, sin_ref, gain_ref, kgain_ref,
                 qa_ref, ka_ref, va_ref, qb_ref, kb_ref, vb_ref, qi_ref, ki_ref, wi_ref,
                 q1_ref, q2_ref, k1_ref, k2_ref, vc_ref, sg_ref, kmean_ref):
    x = x_ref[...]
    ms = jnp.mean(x * x, axis=-1, keepdims=True)
    h = (x * lax.rsqrt(ms + EPS) * g_ref[...]).astype(BF16)
    cos = cos_ref[...]
    sin = sin_ref[...]
    r = lax.broadcasted_iota(jnp.int32, (HW, HW), 0) // HEAD_DIM
    c = lax.broadcasted_iota(jnp.int32, (HW, HW), 1) // HEAD_DIM
    head_ones = (r == c).astype(BF16)

    def seg(name, lo=0, width=None):
        off, w = _SEG[name]
        width = w if width is None else width
        return _dot(h, w_ref[:, off + lo:off + lo + width])

    def rope(y):
        return y * cos + _swap_halves(y, HW) * sin

    def norm_rope(t, gain_row):
        hi, lo = _split_bf16(t * t)
        ss = _dot(hi, head_ones) + _dot(lo, head_ones)
        yn = t * lax.rsqrt(ss * (1.0 / HEAD_DIM) + EPS) * gain_ref[gain_row:gain_row + 1, :]
        return rope(yn)

    qa_ref[...] = (norm_rope(seg("qa"), 0) * Q_SCALE).astype(BF16)
    ka = norm_rope(seg("ka"), 1)
    ka_ref[...] = ka.astype(BF16)
    for blk in range(ka.shape[0] // MOBA_BLOCK):
        kmean_ref[blk] = jnp.mean(ka[blk * MOBA_BLOCK:(blk + 1) * MOBA_BLOCK], axis=0, keepdims=True)
    va_ref[...] = seg("va").astype(BF16)
    qb_ref[...] = (norm_rope(seg("qb"), 2) * Q_SCALE).astype(BF16)
    kb_ref[...] = norm_rope(seg("kb"), 3).astype(BF16)
    vb_ref[...] = seg("vb").astype(BF16)
    for half in range(2):
        qi_ref[:, half * HW:(half + 1) * HW] = rope(seg("qi", half * HW, HW)).astype(BF16)

    t = seg("kw")
    lane = lax.broadcasted_iota(jnp.int32, t.shape, 1)
    is_k = lane < IDX_DIM
    kms = jnp.sum(jnp.where(is_k, t * t, 0.0), axis=-1, keepdims=True) * (1.0 / IDX_DIM)
    kn = t * lax.rsqrt(kms + EPS) * kgain_ref[...]
    kr = kn * cos[:, :LANES] + _swap_halves(kn, LANES) * sin[:, :LANES]
    ki_ref[...] = kr[:, :IDX_DIM].astype(BF16)
    w_scale = (IDX_HEADS ** -0.5) * (IDX_DIM ** -0.5)
    wi_ref[...] = jnp.where(lane < IDX_HEADS, pltpu.roll(t, LANES - IDX_DIM, 1) * w_scale, 0.0)

    q1_ref[...] = (norm_rope(seg("q1"), 4) * Q_SCALE).astype(BF16)
    q2_ref[...] = (norm_rope(seg("q2"), 5) * Q_SCALE).astype(BF16)
    k1_ref[...] = norm_rope(seg("k1"), 6).astype(BF16)
    k2_ref[...] = norm_rope(seg("k2"), 7).astype(BF16)
    for half in range(2):
        vc_ref[:, half * HW:(half + 1) * HW] = seg("vc", half * HW, HW).astype(BF16)
    for gi, name in enumerate(("ga", "gb", "gc")):
        for part in range(D_MODEL // 512):
            g = seg(name, part * 512, 512)
            lo = gi * D_MODEL + part * 512
            sg_ref[:, lo:lo + 512] = (1.0 / (1.0 + jnp.exp(-g))).astype(BF16)


def _project(x2, norm_g, w_pad, cos_t, sin_t, gains, kgain, seq):
    n = x2.shape[0]
    tm = min(DENSE_TILE, seq)
    assert seq % tm == 0 and tm % MOBA_BLOCK == 0
    n_pos = seq // tm
    row = lambda w: pl.BlockSpec((tm, w), lambda i: (i, 0))
    const = lambda shape: pl.BlockSpec(shape, lambda i: (0,) * len(shape))
    out_widths = [HW] * 6 + [IDX_HEADS * IDX_DIM, IDX_DIM, LANES] + [HW] * 4 + [N_HEADS * C_VDIM, 3 * D_MODEL]
    out_dtypes = [BF16] * 8 + [F32] + [BF16] * 6
    out_shape = [jax.ShapeDtypeStruct((n, w), dt) for w, dt in zip(out_widths, out_dtypes)]
    out_shape.append(jax.ShapeDtypeStruct((n // MOBA_BLOCK, 1, HW), F32))
    out_specs = [row(w) for w in out_widths] + [pl.BlockSpec((tm // MOBA_BLOCK, 1, HW), lambda i: (i, 0, 0))]
    return pl.pallas_call(
        _proj_kernel,
        grid=(n // tm,),
        in_specs=[row(D_MODEL), const((1, D_MODEL)),
                  pl.BlockSpec((D_MODEL, D_IN_PAD), lambda i: (0, 0), pipeline_mode=pl.Buffered(1)),
                  pl.BlockSpec((tm, HW), lambda i: (i % n_pos, 0)),
                  pl.BlockSpec((tm, HW), lambda i: (i % n_pos, 0)),
                  const((8, HW)), const((1, LANES))],
        out_specs=out_specs,
        out_shape=out_shape,
        compiler_params=_params(1),
        name="proj",
    )(x2, norm_g, w_pad, cos_t, sin_t, gains, kgain)


def _online_update(parts, ms, acc_ref):
    ps, out = [], []
    for c, tiles in enumerate(parts):
        m_new = ms[c]
        for s, _, ok in tiles:
            smax = jnp.max(s, axis=0, keepdims=True)
            m_new = jnp.maximum(m_new, smax if ok is None else jnp.where(ok, smax, NEG_BIG))
        m_eff = jnp.maximum(m_new, M_FLOOR)
        out.append(m_new)
        probs = [jnp.exp2(s - (m_eff if ok is None else jnp.where(ok, m_eff, -NEG_BIG))).astype(BF16)
                 for s, _, ok in tiles]
        ps.append((jnp.exp2(ms[c] - m_new), probs))
    for c, (alpha, probs) in enumerate(ps):
        acc = alpha * acc_ref[c]
        for (_, vt, _), p in zip(parts[c], probs):
            acc = acc + _dot(vt, p)
        acc_ref[c] = acc
    return out


def _init_max(n_chains, tq):
    return tuple(jnp.full((1, tq), NEG_BIG, F32) for _ in range(n_chains))


def _with_ones_row(vt):
    row = lax.broadcasted_iota(jnp.int32, (V_PAD, vt.shape[1]), 0)
    return jnp.concatenate([vt, jnp.where(row == 0, 1.0, 0.0).astype(vt.dtype)], axis=0)


def _normalized(acc_ref, c, dv):
    acc = acc_ref[c]
    return acc[:dv] / acc[dv:dv + 1]


def _head_slice(h, width=HEAD_DIM):
    return slice(h * width, (h + 1) * width)


def _moba_kernel(qt_ref, k_ref, vt_ref, kmean_ref, o_ref, acc_ref, *, n_sel):
    i = pl.program_id(1)
    blk = MOBA_BLOCK
    nb = kmean_ref.shape[1]
    km = kmean_ref[0]
    brow = lax.broadcasted_iota(jnp.int32, (nb, blk), 0)
    browf = brow.astype(F32)
    causal = (lax.broadcasted_iota(jnp.int32, (blk, 1), 0)
              <= lax.broadcasted_iota(jnp.int32, (1, blk), 1))
    qts = [qt_ref[0, _head_slice(h), :] for h in range(N_HEADS)]
    sels = []
    for h in range(N_HEADS):
        km_hi, km_lo = _split_bf16(km[:, _head_slice(h)])
        gate = _dot(km_hi, qts[h]) + _dot(km_lo, qts[h])
        gate = jnp.where(brow < i, gate, -jnp.inf)
        sel = jnp.zeros((nb, blk), F32)
        for _ in range(n_sel):
            gm = jnp.max(gate, axis=0, keepdims=True)
            is_m = (gate == gm) & (gm > -jnp.inf)
            first = jnp.min(jnp.where(is_m, browf, float(nb)), axis=0, keepdims=True)
            pick = browf == first
            sel = jnp.where(pick, 1.0, sel)
            gate = jnp.where(pick, -jnp.inf, gate)
        sels.append(sel)
    acc_ref[...] = jnp.zeros(acc_ref.shape, F32)

    def tile(j, h, mask=None, seen=False):
        rows = pl.ds(pl.multiple_of(j * blk, blk), blk)
        s = _dot(k_ref[rows, _head_slice(h)], qts[h])
        if mask is not None:
            s = jnp.where(mask, s, NEG_BIG)
        ok = jnp.sum(jnp.where(brow == j, sels[h], 0.0), axis=0, keepdims=True) > 0.0 if seen else None
        return s, _with_ones_row(vt_ref[j, _head_slice(h), :]), ok

    def pair(j2, ms):
        parts = [[tile(2 * j2, h, seen=True), tile(2 * j2 + 1, h, seen=True)] for h in range(N_HEADS)]
        return tuple(_online_update(parts, ms, acc_ref))

    ms = lax.fori_loop(0, i // 2, pair, _init_max(N_HEADS, blk))

    @pl.when(i % 2 == 1)
    def _():
        _online_update([[tile(i - 1, h, seen=True), tile(i, h, mask=causal)] for h in range(N_HEADS)],
                       ms, acc_ref)

    @pl.when(i % 2 == 0)
    def _():
        _online_update([[tile(i, h, mask=causal)] for h in range(N_HEADS)], ms, acc_ref)

    for h in range(N_HEADS):
        o_ref[0, _head_slice(h), :] = _normalized(acc_ref, h, HEAD_DIM).astype(o_ref.dtype)


def _moba(qt, k, vt, kmean, batch, seq):
    blk = MOBA_BLOCK
    nb = seq // blk
    n_sel = min(MOBA_TOPK, nb - 1)
    tspec = pl.BlockSpec((1, HW, blk), lambda b, i: (b, 0, i))
    return pl.pallas_call(
        functools.partial(_moba_kernel, n_sel=n_sel),
        grid=(batch, nb),
        in_specs=[tspec,
                  pl.BlockSpec((seq, HW), lambda b, i: (b, 0)),
                  pl.BlockSpec((nb, vt.shape[1], blk), lambda b, i: (b, 0, 0)),
                  pl.BlockSpec((1, nb, HW), lambda b, i: (b, 0, 0))],
        out_specs=tspec,
        out_shape=jax.ShapeDtypeStruct((batch, HW, seq), BF16),
        scratch_shapes=[pltpu.VMEM((N_HEADS, HEAD_DIM + V_PAD, blk), F32)],
        compiler_params=_params(2),
        name="moba",
    )(qt, k, vt, kmean)


def _bit_planes(words):
    a = list(words)
    assert len(a) == 32
    mask, j = 0x0000FFFF, 16
    while j:
        k = 0
        while k < 32:
            t = (a[k] ^ (a[k + j] >> j)) & mask
            a[k] = a[k] ^ t
            a[k + j] = a[k + j] ^ (t << j)
            k = (k + j + 1) & ~j
        j >>= 1
        mask = (mask ^ (mask << j)) & 0xFFFFFFFF
    return a


def _dsa_kernel(qit_ref, wi_ref, ki_ref, qt_ref, k_ref, vt_ref, o_ref, plane_ref, sel_ref, acc_ref, *, n_keep):
    i = pl.program_id(1)
    blk = ROW_TILE
    n_chunk = i + 1
    n_planes, n_slots, sub = plane_ref.shape[:3]
    v_bits, s_bits = (n_planes - 1).bit_length(), (sub - 1).bit_length()
    idx_bits = v_bits + s_bits + (n_slots - 1).bit_length()

    @pl.when(i == 0)
    def _():
        plane_ref[:, 1:] = jnp.zeros((n_planes, n_slots - 1) + plane_ref.shape[2:], jnp.int32)

    w_t = wi_ref[...].T
    qpos = i * blk + lax.broadcasted_iota(jnp.int32, (1, blk), 1)
    krow = lax.broadcasted_iota(jnp.int32, (blk, 1), 0)

    def score_chunk(c, carry):
        kc = ki_ref[pl.ds(pl.multiple_of(c * blk, blk), blk), :]
        lgs = [_dot(kc, qit_ref[0, _head_slice(h, IDX_DIM), :]) for h in range(IDX_HEADS)]
        sc = jnp.zeros((blk, blk), F32)
        for h in range(IDX_HEADS):
            sc = sc + w_t[h:h + 1, :] * jnp.maximum(lgs[h], 0.0)
        sc = sc + 0.0
        bits = pltpu.bitcast(sc, jnp.int32)
        key = jnp.where(bits < 0, bits ^ 0x7FFFFFFF, bits)
        key = jnp.where(c * blk + krow <= qpos, key, INT_MIN)
        words = (key ^ INT_MIN).reshape(n_planes, sub, blk)
        planes = _bit_planes([words[v] for v in range(n_planes)])
        for b in range(n_planes):
            plane_ref[b, c] = planes[n_planes - 1 - b]
        return carry

    lax.fori_loop(0, n_chunk, score_chunk, 0)

    keep = float(n_keep)

    def step(plane, eq, gt):
        cand = gt | (eq & plane)
        per_row = jnp.sum(lax.population_count(cand), axis=0)
        take = jnp.sum(per_row.astype(F32), axis=0, keepdims=True) >= keep
        return eq & jnp.where(take, plane, ~plane), jnp.where(take, gt, cand)

    def index_plane(t, shape):
        if t < s_bits:
            s = lax.broadcasted_iota(jnp.int32, shape, 1)
            return jnp.where(((s >> t) & 1) == 0, -1, 0)
        if t < s_bits + v_bits:
            word = sum(1 << j for j in range(n_planes) if (((n_planes - 1 - j) >> (t - s_bits)) & 1) == 0)
            return jnp.full(shape, word - (1 << 32) if word >= (1 << 31) else word, jnp.int32)
        slot = lax.broadcasted_iota(jnp.int32, shape, 0)
        return jnp.where(((slot >> (t - s_bits - v_bits)) & 1) == 0, -1, 0)

    def select(n_used, lo):
        shape = (n_used, sub, LANES)
        lanes = slice(lo, lo + LANES)

        def score_bit(t, carry):
            return step(plane_ref[n_planes - 1 - t, 0:n_used, :, lanes], *carry)

        eq, gt = lax.fori_loop(0, n_planes, score_bit,
                               (jnp.full(shape, -1, jnp.int32), jnp.zeros(shape, jnp.int32)))
        for t in reversed(range(idx_bits)):
            eq, gt = step(index_plane(t, shape), eq, gt)
        slot = lax.broadcasted_iota(jnp.int32, shape, 0)
        s = lax.broadcasted_iota(jnp.int32, shape, 1)
        qoff = lo + lax.broadcasted_iota(jnp.int32, shape, 2)
        reach = jnp.left_shift(-1, (n_planes - 1) - ((qoff - s) >> s_bits))
        valid = jnp.where(slot < i, -1, jnp.where((slot == i) & (qoff >= s), reach, 0))
        return (eq | gt) & valid

    half_slots = n_slots // 2
    for n_used, wanted in ((half_slots, n_chunk <= half_slots), (n_slots, n_chunk > half_slots)):
        @pl.when(wanted)
        def _():
            for lo in range(0, blk, LANES):
                sel_ref[0:n_used, :, lo:lo + LANES] = select(n_used, lo)
                if n_used < n_slots:
                    sel_ref[n_used:, :, lo:lo + LANES] = jnp.zeros((n_slots - n_used, sub, LANES), jnp.int32)

    qts = [qt_ref[0, _head_slice(h), :] for h in range(N_HEADS)]
    acc_ref[...] = jnp.zeros(acc_ref.shape, F32)

    def pair(c2, ms):
        parts = [[] for _ in range(N_HEADS)]
        for c in (2 * c2, 2 * c2 + 1):
            rows = pl.ds(pl.multiple_of(c * blk, blk), blk)
            w = sel_ref[c]
            allowed = jnp.concatenate([jnp.left_shift(w, v) for v in range(n_planes)], axis=0) < 0
            for h in range(N_HEADS):
                s = jnp.where(allowed, _dot(k_ref[rows, _head_slice(h)], qts[h]), NEG_BIG)
                parts[h].append((s, _with_ones_row(vt_ref[c, _head_slice(h), :]), None))
        return tuple(_online_update(parts, ms, acc_ref))

    lax.fori_loop(0, (n_chunk + 1) // 2, pair, _init_max(N_HEADS, blk))
    for h in range(N_HEADS):
        o_ref[0, _head_slice(h), :] = _normalized(acc_ref, h, HEAD_DIM).astype(o_ref.dtype)


def _dsa(qit, wi, ki, qt, k, vt, batch, seq):
    blk = ROW_TILE
    nq = seq // blk
    n_keep = min(DSA_TOPK_MAX, seq // 4)
    n_planes = 32
    tspec = lambda w: pl.BlockSpec((1, w, blk), lambda b, i: (b, 0, i))
    full = lambda w: pl.BlockSpec((seq, w), lambda b, i: (b, 0))
    return pl.pallas_call(
        functools.partial(_dsa_kernel, n_keep=n_keep),
        grid=(batch, nq),
        in_specs=[tspec(IDX_HEADS * IDX_DIM), pl.BlockSpec((blk, LANES), lambda b, i: (b * nq + i, 0)),
                  full(IDX_DIM), tspec(HW), full(HW),
                  pl.BlockSpec((nq, vt.shape[1], blk), lambda b, i: (b, 0, 0))],
        out_specs=tspec(HW),
        out_shape=jax.ShapeDtypeStruct((batch, HW, seq), BF16),
        scratch_shapes=[pltpu.VMEM((n_planes, nq, blk // n_planes, blk), jnp.int32),
                        pltpu.VMEM((nq, blk // n_planes, blk), jnp.int32),
                        pltpu.VMEM((N_HEADS, HEAD_DIM + V_PAD, blk), F32)],
        compiler_params=_params(2),
        name="dsa",
    )(qit, wi, ki, qt, k, vt)


DIFF_GROUP = 4


def _diff_kernel(q1t_ref, q2t_ref, k1_ref, k2_ref, vt_ref, dl_ref, gain_ref, o_ref, acc_ref, *, lam_init):
    i = pl.program_id(1)
    blk = ROW_TILE
    dl = dl_ref[...]
    lam = (jnp.exp(jnp.sum(dl[0:1] * dl[1:2], axis=-1, keepdims=True))
           - jnp.exp(jnp.sum(dl[2:3] * dl[3:4], axis=-1, keepdims=True)) + lam_init)
    causal = (lax.broadcasted_iota(jnp.int32, (blk, 1), 0)
              <= lax.broadcasted_iota(jnp.int32, (1, blk), 1))
    maps = ((q1t_ref, k1_ref), (q2t_ref, k2_ref))

    for h0 in range(0, N_HEADS, DIFF_GROUP):
        chains = [(h, mp) for h in range(h0, h0 + DIFF_GROUP) for mp in range(2)]
        qts = [maps[mp][0][0, _head_slice(h), :] for h, mp in chains]
        acc_ref[...] = jnp.zeros(acc_ref.shape, F32)

        def step(j, ms, diag):
            rows = pl.ds(pl.multiple_of(j * blk, blk), blk)
            ss = [_dot(maps[mp][1][rows, _head_slice(h)], qts[c]) for c, (h, mp) in enumerate(chains)]
            if diag:
                ss = [jnp.where(causal, s, NEG_BIG) for s in ss]
            parts = [[(s, _with_ones_row(vt_ref[j, _head_slice(h, C_VDIM), :]), None)]
                     for s, (h, _) in zip(ss, chains)]
            return tuple(_online_update(parts, ms, acc_ref))

        stats = lax.fori_loop(0, i, lambda j, st: step(j, st, False), _init_max(len(chains), blk))
        step(i, stats, True)
        for g in range(DIFF_GROUP):
            h = h0 + g
            o = _normalized(acc_ref, 2 * g, C_VDIM) - lam * _normalized(acc_ref, 2 * g + 1, C_VDIM)
            ms = jnp.mean(o * o, axis=0, keepdims=True)
            o = o * lax.rsqrt(ms + EPS) * gain_ref[...] * (1.0 - lam_init)
            o_ref[0, _head_slice(h, C_VDIM), :] = o.astype(o_ref.dtype)


def _diff(q1t, q2t, k1, k2, vt, dl, gain, lam_init, batch, seq):
    blk = ROW_TILE
    nq = seq // blk
    tspec = lambda w: pl.BlockSpec((1, w, blk), lambda b, i: (b, 0, i))
    kspec = pl.BlockSpec((seq, HW), lambda b, i: (b, 0))
    vw = N_HEADS * C_VDIM
    return pl.pallas_call(
        functools.partial(_diff_kernel, lam_init=lam_init),
        grid=(batch, nq),
        in_specs=[tspec(HW), tspec(HW), kspec, kspec,
                  pl.BlockSpec((nq, vt.shape[1], blk), lambda b, i: (b, 0, 0)),
                  pl.BlockSpec((4, HEAD_DIM), lambda b, i: (0, 0)),
                  pl.BlockSpec((C_VDIM, 1), lambda b, i: (0, 0))],
        out_specs=tspec(vw),
        out_shape=jax.ShapeDtypeStruct((batch, vw, seq), BF16),
        scratch_shapes=[pltpu.VMEM((2 * DIFF_GROUP, C_VDIM + V_PAD, blk), F32)],
        compiler_params=_params(2),
        name="diff",
    )(q1t, q2t, k1, k2, vt, dl, gain)


def _merge_kernel(x_ref, oa_ref, ob_ref, oc_ref, sg_ref, wa_ref, wb_ref, wc_ref, wo_ref, g_ref,
                  wr_hi_ref, wr_lo_ref, br_ref, x1_ref, hn_ref, route_ref):
    merged = (sg_ref[:, 0:D_MODEL].astype(F32) * _dot(oa_ref[...], wa_ref[...])
              + sg_ref[:, D_MODEL:2 * D_MODEL].astype(F32) * _dot(ob_ref[...], wb_ref[...])
              + sg_ref[:, 2 * D_MODEL:3 * D_MODEL].astype(F32) * _dot(oc_ref[...], wc_ref[...]))
    x1 = x_ref[...] + _dot(merged.astype(BF16), wo_ref[...])
    x1_ref[...] = x1
    ms = jnp.mean(x1 * x1, axis=-1, keepdims=True)
    hn = x1 * lax.rsqrt(ms + EPS) * g_ref[...]
    hn_ref[...] = hn

    hi, lo = _split_bf16(hn)
    lg = (_dot(hi, wr_hi_ref[...]) + _dot(lo, wr_hi_ref[...]) + _dot(hi, wr_lo_ref[...])
          + br_ref[...])
    lane = lax.broadcasted_iota(jnp.int32, lg.shape, 1)
    lanef = lane.astype(F32)
    far = float(LANES)
    is_g = (lane >= N_EXPERTS) & (lane < N_EXPERTS + N_GROUPS)
    gl = jnp.where(is_g, lg, -jnp.inf)
    gmax = jnp.max(gl, axis=-1, keepdims=True)
    gidx = jnp.min(jnp.where(gl == gmax, lanef, far), axis=-1, keepdims=True) - float(N_EXPERTS)
    g_w = 1.0 / jnp.sum(jnp.where(is_g, jnp.exp(gl - gmax), 0.0), axis=-1, keepdims=True)
    in_group = (lane < N_EXPERTS) & ((lane // EXPERTS_PER_GROUP).astype(F32) == gidx)
    el = jnp.where(in_group, lg, -jnp.inf)
    e1 = jnp.max(el, axis=-1, keepdims=True)
    i1 = jnp.min(jnp.where(el == e1, lanef, far), axis=-1, keepdims=True)
    el2 = jnp.where(lanef == i1, -jnp.inf, el)
    e2 = jnp.max(el2, axis=-1, keepdims=True)
    i2 = jnp.min(jnp.where(el2 == e2, lanef, far), axis=-1, keepdims=True)
    t = jnp.exp(e2 - e1)
    w1 = g_w / (1.0 + t)
    w2 = g_w * t / (1.0 + t)
    route_ref[...] = jnp.where(lane == 0, i1, jnp.where(lane == 1, i2, jnp.where(lane == 2, w1, jnp.where(lane == 3, w2, 0.0))))


def _merge(x2, oa, ob, oc, sg, wa, wb, wc, wo, norm_g, wr_hi, wr_lo, br):
    n = x2.shape[0]
    tm = min(DENSE_TILE, n)
    row = lambda w: pl.BlockSpec((tm, w), lambda i: (i, 0))
    const = lambda a: pl.BlockSpec(a.shape, lambda i: (0, 0))
    return pl.pallas_call(
        _merge_kernel,
        grid=(n // tm,),
        in_specs=[row(D_MODEL), row(HW), row(HW), row(N_HEADS * C_VDIM), row(3 * D_MODEL),
                  const(wa), const(wb), const(wc), const(wo), const(norm_g), const(wr_hi),
                  const(wr_lo), const(br)],
        out_specs=[row(D_MODEL), row(D_MODEL), row(LANES)],
        out_shape=[jax.ShapeDtypeStruct((n, D_MODEL), F32), jax.ShapeDtypeStruct((n, D_MODEL), F32),
                   jax.ShapeDtypeStruct((n, LANES), F32)],
        compiler_params=_params(1),
        name="merge",
    )(x2, oa, ob, oc, sg, wa, wb, wc, wo, norm_g, wr_hi, wr_lo, br)


MOE_BLOCK = 256
META_ROWS = 8
META_USED = 3 * LANES
META_END = 4 * LANES
META_PADDED = 5 * LANES


def _lane_prefix_sum(x):
    lane = lax.broadcasted_iota(jnp.int32, x.shape, 1)
    shift = 1
    while shift < LANES:
        x = x + jnp.where(lane >= shift, pltpu.roll(x, shift, 1), 0.0)
        shift *= 2
    return x


def _positions_kernel(route_ref, dest_ref, meta_ref, cnt_ref, base_ref):
    phase = pl.program_id(0)
    t = pl.program_id(1)
    tm = route_ref.shape[0]
    route = route_ref[...]
    lane = lax.broadcasted_iota(jnp.int32, route.shape, 1)
    lanef = lane.astype(F32)
    e1 = route[:, 0:1]
    e2 = route[:, 1:2]
    uses = jnp.where((lanef == e1) | (lanef == e2), 1.0, 0.0)
    tile_cnt = jnp.sum(uses, axis=0, keepdims=True)

    @pl.when((phase == 0) & (t == 0))
    def _():
        cnt_ref[...] = jnp.zeros(cnt_ref.shape, F32)

    @pl.when(phase == 0)
    def _():
        cnt_ref[...] += tile_cnt

    @pl.when((phase == 1) & (t == 0))
    def _():
        cnt = cnt_ref[...]
        padded = jnp.floor((cnt + (MOE_BLOCK - 1)) * (1.0 / MOE_BLOCK)) * MOE_BLOCK
        end = _lane_prefix_sum(padded)
        base_ref[...] = end - padded
        cnt_ref[...] = jnp.zeros(cnt_ref.shape, F32)
        lane1 = lax.broadcasted_iota(jnp.int32, (1, LANES), 1)
        row = lax.broadcasted_iota(jnp.int32, (META_ROWS, LANES), 0)
        col = lax.broadcasted_iota(jnp.int32, (META_ROWS, LANES), 1)
        first_row = ((row * LANES + col) * MOE_BLOCK).astype(F32)
        owner = jnp.zeros((META_ROWS, LANES), F32)
        for e in range(N_EXPERTS):
            end_e = jnp.sum(jnp.where(lane1 == e, end, 0.0), axis=-1, keepdims=True)
            owner = owner + jnp.where(end_e <= first_row, 1.0, 0.0)
        owner = jnp.minimum(owner, float(N_EXPERTS - 1))
        used = jnp.sum(jnp.where(lane1 == N_EXPERTS - 1, end, 0.0), axis=-1, keepdims=True) * (1.0 / MOE_BLOCK)
        meta = jnp.where(row == META_USED // LANES, used,
                         jnp.where(row == META_END // LANES, end,
                                   jnp.where(row == META_PADDED // LANES, padded, owner)))
        meta_ref[...] = meta.astype(jnp.int32)

    @pl.when(phase == 1)
    def _():
        before = (lax.broadcasted_iota(jnp.int32, (tm, tm), 1)
                  < lax.broadcasted_iota(jnp.int32, (tm, tm), 0)).astype(BF16)
        rank = _dot(before, uses.astype(BF16))
        pos = base_ref[...] + cnt_ref[...] + rank
        d1 = jnp.sum(jnp.where(lanef == e1, pos, 0.0), axis=-1, keepdims=True)
        d2 = jnp.sum(jnp.where(lanef == e2, pos, 0.0), axis=-1, keepdims=True)
        dest_ref[...] = jnp.where(lane == 0, d1, jnp.where(lane == 1, d2, 0.0)).astype(jnp.int32)
        cnt_ref[...] += tile_cnt


def _positions(route):
    n = route.shape[0]
    tm = min(4 * ROW_TILE, n)
    return pl.pallas_call(
        _positions_kernel,
        grid=(2, n // tm),
        in_specs=[pl.BlockSpec((tm, LANES), lambda p, t: (t, 0))],
        out_specs=[pl.BlockSpec((tm, LANES), lambda p, t: (t * p, 0)),
                   pl.BlockSpec((META_ROWS, LANES), lambda p, t: (0, 0))],
        out_shape=[jax.ShapeDtypeStruct((n, LANES), jnp.int32),
                   jax.ShapeDtypeStruct((META_ROWS, LANES), jnp.int32)],
        scratch_shapes=[pltpu.VMEM((1, LANES), F32), pltpu.VMEM((1, LANES), F32)],
        compiler_params=_params(2),
        name="moe_positions",
    )(route)


def _row_copy(src_ref, src_row, dst_ref, dst_row, sem):
    return pltpu.make_async_copy(src_ref.at[pl.ds(src_row, 1), :], dst_ref.at[pl.ds(dst_row, 1), :], sem)


def _dispatch_kernel(meta_ref, dest_ref, hn_ref, xs_ref, zero_ref, sem):
    tm = hn_ref.shape[0]

    @pl.when(pl.program_id(0) == 0)
    def _():
        zero_ref[...] = jnp.zeros(zero_ref.shape, F32)

        def fill(e):
            end = pl.multiple_of(meta_ref[META_END + e], MOE_BLOCK)
            return pltpu.make_async_copy(zero_ref, xs_ref.at[pl.ds(end - MOE_BLOCK, MOE_BLOCK), :], sem)

        for e in range(N_EXPERTS):
            @pl.when(meta_ref[META_PADDED + e] > 0)
            def _():
                fill(e).start()
        for e in range(N_EXPERTS):
            @pl.when(meta_ref[META_PADDED + e] > 0)
            def _():
                fill(e).wait()

        def spare(b):
            return pltpu.make_async_copy(
                zero_ref, xs_ref.at[pl.ds(pl.multiple_of(b * MOE_BLOCK, MOE_BLOCK), MOE_BLOCK), :], sem)

        n_blocks = xs_ref.shape[0] // MOE_BLOCK
        lax.fori_loop(meta_ref[META_USED], n_blocks, lambda b, c: (spare(b).start(), c)[1], 0)
        lax.fori_loop(meta_ref[META_USED], n_blocks, lambda b, c: (spare(b).wait(), c)[1], 0)

    def copies(r):
        return (_row_copy(hn_ref, r, xs_ref, dest_ref[0, 0, 2 * r], sem),
                _row_copy(hn_ref, r, xs_ref, dest_ref[0, 0, 2 * r + 1], sem))

    def start(r, carry):
        for cp in copies(r):
            cp.start()
        return carry

    lax.fori_loop(0, tm, start, 0, unroll=8)
    for _ in range(2):
        pltpu.make_async_copy(hn_ref, xs_ref.at[pl.ds(0, tm), :], sem).wait()


def _dispatch(meta, dest3, hn, n_rows):
    n = hn.shape[0]
    tm = dest3.shape[2] // 2
    return pl.pallas_call(
        _dispatch_kernel,
        grid_spec=pltpu.PrefetchScalarGridSpec(
            num_scalar_prefetch=1,
            grid=(n // tm,),
            in_specs=[pl.BlockSpec((1, 1, 2 * tm), lambda t, m: (t, 0, 0), memory_space=pltpu.SMEM),
                      pl.BlockSpec((tm, D_MODEL), lambda t, m: (t, 0))],
            out_specs=pl.BlockSpec(memory_space=pl.ANY),
            scratch_shapes=[pltpu.VMEM((MOE_BLOCK, D_MODEL), F32), pltpu.SemaphoreType.DMA(())]),
        out_shape=jax.ShapeDtypeStruct((n_rows, D_MODEL), F32),
        compiler_params=_params(1),
        name="moe_dispatch",
    )(meta, dest3, hn)


def _expert_kernel(meta_ref, xs_ref, wg_ref, wu_ref, wd_ref, y_ref, wg_bf, wu_bf, wd_bf):
    b = pl.program_id(0)
    holds_rows = b < meta_ref[META_USED]
    new_expert = (b == 0) | (meta_ref[b] != meta_ref[jnp.maximum(b - 1, 0)])

    @pl.when(holds_rows & new_expert)
    def _():
        wg_bf[...] = wg_ref[0].astype(BF16)
        wu_bf[...] = wu_ref[0].astype(BF16)
        wd_bf[...] = wd_ref[0].astype(BF16)

    @pl.when(holds_rows)
    def _():
        x = xs_ref[...].astype(BF16)
        g = _dot(x, wg_bf[...])
        u = _dot(x, wu_bf[...])
        hid = g * (1.0 / (1.0 + jnp.exp(-g))) * u
        y_ref[...] = _dot(hid.astype(BF16), wd_bf[...])

    @pl.when(jnp.logical_not(holds_rows))
    def _():
        y_ref[...] = jnp.zeros(y_ref.shape, F32)


def _experts(meta, xs, wg, wu, wd, layer):
    n_blocks = xs.shape[0] // MOE_BLOCK
    rows = pl.BlockSpec((MOE_BLOCK, D_MODEL), lambda b, m: (b, 0))
    weight = lambda shape: pl.BlockSpec(
        (None, 1) + shape, lambda b, m: (layer, m[jnp.minimum(b, m[META_USED] - 1)], 0, 0))
    return pl.pallas_call(
        _expert_kernel,
        grid_spec=pltpu.PrefetchScalarGridSpec(
            num_scalar_prefetch=1,
            grid=(n_blocks,),
            in_specs=[rows, weight((D_MODEL, D_EXPERT)), weight((D_MODEL, D_EXPERT)),
                      weight((D_EXPERT, D_MODEL))],
            out_specs=rows,
            scratch_shapes=[pltpu.VMEM((D_MODEL, D_EXPERT), BF16), pltpu.VMEM((D_MODEL, D_EXPERT), BF16),
                            pltpu.VMEM((D_EXPERT, D_MODEL), BF16)]),
        out_shape=jax.ShapeDtypeStruct(xs.shape, F32),
        compiler_params=_params(1),
        name="moe_experts",
    )(meta, xs, wg, wu, wd)


def _combine_kernel(dest_ref, x1_ref, route_ref, y_ref, o_ref, buf_ref, sem):
    tm = x1_ref.shape[0]

    def copies(r):
        return (_row_copy(y_ref, dest_ref[0, 0, 2 * r], buf_ref.at[0], r, sem),
                _row_copy(y_ref, dest_ref[0, 0, 2 * r + 1], buf_ref.at[1], r, sem))

    def start(r, carry):
        for cp in copies(r):
            cp.start()
        return carry

    lax.fori_loop(0, tm, start, 0, unroll=8)
    for slot in range(2):
        pltpu.make_async_copy(y_ref.at[pl.ds(0, tm), :], buf_ref.at[slot], sem).wait()
    route = route_ref[...]
    o_ref[...] = x1_ref[...] + route[:, 2:3] * buf_ref[0] + route[:, 3:4] * buf_ref[1]


def _combine(dest3, x1, route, y):
    n = x1.shape[0]
    tm = dest3.shape[2] // 2
    row = lambda w: pl.BlockSpec((tm, w), lambda t: (t, 0))
    return pl.pallas_call(
        _combine_kernel,
        grid=(n // tm,),
        in_specs=[pl.BlockSpec((1, 1, 2 * tm), lambda t: (t, 0, 0), memory_space=pltpu.SMEM),
                  row(D_MODEL), row(LANES), pl.BlockSpec(memory_space=pl.ANY)],
        out_specs=row(D_MODEL),
        out_shape=jax.ShapeDtypeStruct((n, D_MODEL), F32),
        scratch_shapes=[pltpu.VMEM((2, tm, D_MODEL), F32), pltpu.SemaphoreType.DMA(())],
        compiler_params=_params(1),
        name="moe_combine",
    )(dest3, x1, route, y)


def _moe(x1, hn, route, wg, wu, wd, layer):
    n = x1.shape[0]
    tm = min(ROW_TILE, n)
    n_blocks = -(-(2 * n + N_EXPERTS * (MOE_BLOCK - 1)) // MOE_BLOCK)
    assert n_blocks <= META_USED
    dest, meta = _positions(route)
    meta = meta.reshape(-1)
    dest3 = dest[:, :2].reshape(n // tm, 1, 2 * tm)
    xs = _dispatch(meta, dest3, hn, n_blocks * MOE_BLOCK)
    y = _experts(meta, xs, wg, wu, wd, layer)
    return _combine(dest3, x1, route, y)


def _rope_tables(seq):
    inv_freq = 1.0 / (ROPE_THETA ** (jnp.arange(0, HEAD_DIM, 2, dtype=F32) / HEAD_DIM))
    ang = jnp.arange(seq, dtype=F32)[:, None] * inv_freq[None, :]
    cos, sin = jnp.cos(ang), jnp.sin(ang)
    cos_t = jnp.tile(jnp.concatenate([cos, cos], axis=-1), (1, N_HEADS))
    sin_t = jnp.tile(jnp.concatenate([-sin, sin], axis=-1), (1, N_HEADS))
    return cos_t, sin_t


def _to_t(v, batch, n_chunk):
    feat = v.shape[1]
    return (v.reshape(batch, n_chunk, ROW_TILE, feat).transpose(0, 1, 3, 2)
            .reshape(batch * n_chunk, feat, ROW_TILE))


def _from_t(o_t):
    b, feat, seq = o_t.shape
    return o_t.transpose(0, 2, 1).reshape(b * seq, feat)


def kernel(x, norm_attn, w_in, qk_gain, idx_k_gain, diff_lambda, diff_subln_gain, w_proj_a, w_proj_b, w_proj_c, w_out, norm_ffn, w_group, b_group, w_router, b_router, w_e_gate, w_e_up, w_e_down):
    batch, seq, d = x.shape
    assert d == D_MODEL and seq % (2 * ROW_TILE) == 0 and ROW_TILE == MOBA_BLOCK
    n = batch * seq
    nq = seq // ROW_TILE
    depth = w_in.shape[0]
    cos_t, sin_t = _rope_tables(seq)
    x2 = x.reshape(n, d)
    for l in range(depth):
        w_pad = jnp.concatenate(
            [w_in[l][:, :KW_SRC], jnp.zeros((d, LANES - IDX_DIM - IDX_HEADS), F32), w_in[l][:, KW_SRC:]],
            axis=1).astype(BF16)
        gains = jnp.tile(qk_gain[l][jnp.array([0, 1, 2, 3, 4, 4, 5, 5])], (1, N_HEADS))
        kgain = jnp.pad(idx_k_gain[l], (0, LANES - IDX_DIM))[None, :]
        (qa, ka, va, qb, kb, vb, qi, ki, wi, q1, q2, k1, k2, vc, sg, kmean) = _project(
            x2, norm_attn[l][None, :], w_pad, cos_t, sin_t, gains, kgain, seq)

        feat_major = lambda t: t.reshape(batch, seq, t.shape[1]).transpose(0, 2, 1)
        o_a = _from_t(_moba(feat_major(qa), ka, _to_t(va, batch, nq), kmean.reshape(batch, nq, HW),
                            batch, seq))
        o_b = _from_t(_dsa(feat_major(qi), wi, ki, feat_major(qb), kb, _to_t(vb, batch, nq), batch, seq))
        lam_init = 0.8 - 0.6 * math.exp(-0.3 * l)
        o_c = _from_t(_diff(feat_major(q1), feat_major(q2), k1, k2, _to_t(vc, batch, nq), diff_lambda[l],
                            diff_subln_gain[l][:, None], lam_init, batch, seq))

        w_r = jnp.concatenate([w_router[l], w_group[l],
                               jnp.zeros((d, LANES - N_EXPERTS - N_GROUPS), F32)], axis=1)
        wr_hi = w_r.astype(BF16)
        wr_lo = (w_r - wr_hi.astype(F32)).astype(BF16)
        b_r = jnp.concatenate([b_router[l], b_group[l],
                               jnp.zeros((LANES - N_EXPERTS - N_GROUPS,), F32)])[None, :]
        x1, hn, route = _merge(x2, o_a, o_b, o_c, sg, w_proj_a[l].astype(BF16), w_proj_b[l].astype(BF16),
                            w_proj_c[l].astype(BF16), w_out[l].astype(BF16), norm_ffn[l][None, :],
                            wr_hi, wr_lo, b_r)
        x2 = _moe(x1, hn, route, w_e_gate, w_e_up, w_e_down, l)
    return x2.reshape(batch, seq, d)
```

```python
import functools
import math

import jax
import jax.numpy as jnp
from jax import lax
from jax.experimental import pallas as pl
from jax.experimental.pallas import tpu as pltpu

F32 = jnp.float32
BF16 = jnp.bfloat16

D_MODEL = 1024
HEAD_DIM = 64
ROPE_THETA = 10000.0
EPS = 1e-6
N_HEADS = 4
MOBA_BLOCK = 256
MOBA_TOPK = 3
IDX_HEADS = 8
IDX_DIM = 64
DSA_TOPK_MAX = 256
C_VDIM = 2 * HEAD_DIM
N_GROUPS = 4
EXPERTS_PER_GROUP = 8
N_EXPERTS = N_GROUPS * EXPERTS_PER_GROUP
D_EXPERT = 512

HW = N_HEADS * HEAD_DIM
LANES = 128
ROW_TILE = 256
DENSE_TILE = 512
VMEM_LIMIT = 56 * 1024 * 1024

_SEG = {}
_off = 0
for _name, _w in (("qa", HW), ("ka", HW), ("va", HW), ("qb", HW), ("kb", HW), ("vb", HW),
                  ("qi", IDX_HEADS * IDX_DIM), ("kw", LANES), ("q1", HW), ("q2", HW), ("k1", HW),
                  ("k2", HW), ("vc", N_HEADS * C_VDIM), ("ga", D_MODEL), ("gb", D_MODEL),
                  ("gc", D_MODEL)):
    _SEG[_name] = (_off, _w)
    _off += _w
D_IN_PAD = _off
KW_SRC = 6 * HW + IDX_HEADS * IDX_DIM + IDX_DIM + IDX_HEADS

NEG_BIG = -1e30
M_FLOOR = -1e20
INT_MIN = -(2 ** 31)
LOG2E = math.log2(math.e)
Q_SCALE = HEAD_DIM ** -0.5 * LOG2E
V_PAD = 16
NT_DIMS = (((1,), (1,)), ((), ()))


def _params(n_axes):
    return pltpu.CompilerParams(dimension_semantics=("arbitrary",) * n_axes,
                                vmem_limit_bytes=VMEM_LIMIT)


def _dot(a, b):
    return jnp.dot(a, b, preferred_element_type=F32)


def _dot_nt(a, b):
    return lax.dot_general(a, b, NT_DIMS, preferred_element_type=F32)


def _split_bf16(a):
    hi = a.astype(BF16)
    return hi, (a - hi.astype(F32)).astype(BF16)


def _swap_halves(y, width):
    lane = lax.broadcasted_iota(jnp.int32, y.shape, 1)
    first = (lane % HEAD_DIM) < (HEAD_DIM // 2)
    return jnp.where(first, pltpu.roll(y, width - HEAD_DIM // 2, 1), pltpu.roll(y, HEAD_DIM // 2, 1))


def _proj_kernel(x_ref, g_ref, w_ref, cos_ref, sin_ref, gain_ref, kgain_ref,
                 qa_ref, ka_ref, va_ref, qb_ref, kb_ref, vb_ref, qi_ref, ki_ref, wi_ref,
                 q1_ref, q2_ref, k1_ref, k2_ref, vc_ref, sg_ref, kmean_ref):
    x = x_ref[...]
    ms = jnp.mean(x * x, axis=-1, keepdims=True)
    h = (x * lax.rsqrt(ms + EPS) * g_ref[...]).astype(BF16)
    cos = cos_ref[...]
    sin = sin_ref[...]
    r = lax.broadcasted_iota(jnp.int32, (HW, HW), 0) // HEAD_DIM
    c = lax.broadcasted_iota(jnp.int32, (HW, HW), 1) // HEAD_DIM
    head_ones = (r == c).astype(BF16)

    def seg(name, lo=0, width=None):
        off, w = _SEG[name]
        width = w if width is None else width
        return _dot(h, w_ref[:, off + lo:off + lo + width])

    def rope(y):
        return y * cos + _swap_halves(y, HW) * sin

    def norm_rope(t, gain_row):
        hi, lo = _split_bf16(t * t)
        ss = _dot(hi, head_ones) + _dot(lo, head_ones)
        yn = t * lax.rsqrt(ss * (1.0 / HEAD_DIM) + EPS) * gain_ref[gain_row:gain_row + 1, :]
        return rope(yn)

    qa_ref[...] = (norm_rope(seg("qa"), 0) * Q_SCALE).astype(BF16)
    ka = norm_rope(seg("ka"), 1)
    ka_ref[...] = ka.astype(BF16)
    for blk in range(ka.shape[0] // MOBA_BLOCK):
        kmean_ref[blk] = jnp.mean(ka[blk * MOBA_BLOCK:(blk + 1) * MOBA_BLOCK], axis=0, keepdims=True)
    va_ref[...] = seg("va").astype(BF16)
    qb_ref[...] = (norm_rope(seg("qb"), 2) * Q_SCALE).astype(BF16)
    kb_ref[...] = norm_rope(seg("kb"), 3).astype(BF16)
    vb_ref[...] = seg("vb").astype(BF16)
    for half in range(2):
        qi_ref[:, half * HW:(half + 1) * HW] = rope(seg("qi", half * HW, HW)).astype(BF16)

    t = seg("kw")
    lane = lax.broadcasted_iota(jnp.int32, t.shape, 1)
    is_k = lane < IDX_DIM
    kms = jnp.sum(jnp.where(is_k, t * t, 0.0), axis=-1, keepdims=True) * (1.0 / IDX_DIM)
    kn = t * lax.rsqrt(kms + EPS) * kgain_ref[...]
    kr = kn * cos[:, :LANES] + _swap_halves(kn, LANES) * sin[:, :LANES]
    ki_ref[...] = kr[:, :IDX_DIM].astype(BF16)
    w_scale = (IDX_HEADS ** -0.5) * (IDX_DIM ** -0.5)
    wi_ref[...] = jnp.where(lane < IDX_HEADS, pltpu.roll(t, LANES - IDX_DIM, 1) * w_scale, 0.0)

    q1_ref[...] = (norm_rope(seg("q1"), 4) * Q_SCALE).astype(BF16)
    q2_ref[...] = (norm_rope(seg("q2"), 5) * Q_SCALE).astype(BF16)
    k1_ref[...] = norm_rope(seg("k1"), 6).astype(BF16)
    k2_ref[...] = norm_rope(seg("k2"), 7).astype(BF16)
    for half in range(2):
        vc_ref[:, half * HW:(half + 1) * HW] = seg("vc", half * HW, HW).astype(BF16)
    for gi, name in enumerate(("ga", "gb", "gc")):
        for part in range(D_MODEL // 512):
            g = seg(name, part * 512, 512)
            lo = gi * D_MODEL + part * 512
            sg_ref[:, lo:lo + 512] = (1.0 / (1.0 + jnp.exp(-g))).astype(BF16)


def _project(x2, norm_g, w_pad, cos_t, sin_t, gains, kgain, seq):
    n = x2.shape[0]
    tm = min(DENSE_TILE, seq)
    assert seq % tm == 0 and tm % MOBA_BLOCK == 0
    n_pos = seq // tm
    row = lambda w: pl.BlockSpec((tm, w), lambda i: (i, 0))
    const = lambda shape: pl.BlockSpec(shape, lambda i: (0,) * len(shape))
    out_widths = [HW] * 6 + [IDX_HEADS * IDX_DIM, IDX_DIM, LANES] + [HW] * 4 + [N_HEADS * C_VDIM, 3 * D_MODEL]
    out_dtypes = [BF16] * 8 + [F32] + [BF16] * 6
    out_shape = [jax.ShapeDtypeStruct((n, w), dt) for w, dt in zip(out_widths, out_dtypes)]
    out_shape.append(jax.ShapeDtypeStruct((n // MOBA_BLOCK, 1, HW), F32))
    out_specs = [row(w) for w in out_widths] + [pl.BlockSpec((tm // MOBA_BLOCK, 1, HW), lambda i: (i, 0, 0))]
    return pl.pallas_call(
        _proj_kernel,
        grid=(n // tm,),
        in_specs=[row(D_MODEL), const((1, D_MODEL)),
                  pl.BlockSpec((D_MODEL, D_IN_PAD), lambda i: (0, 0), pipeline_mode=pl.Buffered(1)),
                  pl.BlockSpec((tm, HW), lambda i: (i % n_pos, 0)),
                  pl.BlockSpec((tm, HW), lambda i: (i % n_pos, 0)),
                  const((8, HW)), const((1, LANES))],
        out_specs=out_specs,
        out_shape=out_shape,
        compiler_params=_params(1),
        name="proj",
    )(x2, norm_g, w_pad, cos_t, sin_t, gains, kgain)


def _online_update(parts, ms, acc_ref):
    ps, out = [], []
    for c, tiles in enumerate(parts):
        m_new = ms[c]
        for s, _, ok in tiles:
            smax = jnp.max(s, axis=0, keepdims=True)
            m_new = jnp.maximum(m_new, smax if ok is None else jnp.where(ok, smax, NEG_BIG))
        m_eff = jnp.maximum(m_new, M_FLOOR)
        out.append(m_new)
        probs = [jnp.exp2(s - (m_eff if ok is None else jnp.where(ok, m_eff, -NEG_BIG))).astype(BF16)
                 for s, _, ok in tiles]
        ps.append((jnp.exp2(ms[c] - m_new), probs))
    for c, (alpha, probs) in enumerate(ps):
        acc = alpha * acc_ref[c]
        for (_, vt, _), p in zip(parts[c], probs):
            acc = acc + _dot(vt, p)
        acc_ref[c] = acc
    return out


def _init_max(n_chains, tq):
    return tuple(jnp.full((1, tq), NEG_BIG, F32) for _ in range(n_chains))


def _with_ones_row(vt):
    row = lax.broadcasted_iota(jnp.int32, (V_PAD, vt.shape[1]), 0)
    return jnp.concatenate([vt, jnp.where(row == 0, 1.0, 0.0).astype(vt.dtype)], axis=0)


def _normalized(acc_ref, c, dv):
    acc = acc_ref[c]
    return acc[:dv] / acc[dv:dv + 1]


def _head_slice(h, width=HEAD_DIM):
    return slice(h * width, (h + 1) * width)


def _moba_kernel(qt_ref, k_ref, vt_ref, kmean_ref, o_ref, acc_ref, *, n_sel):
    i = pl.program_id(1)
    blk = MOBA_BLOCK
    nb = kmean_ref.shape[1]
    km = kmean_ref[0]
    brow = lax.broadcasted_iota(jnp.int32, (nb, blk), 0)
    browf = brow.astype(F32)
    causal = (lax.broadcasted_iota(jnp.int32, (blk, 1), 0)
              <= lax.broadcasted_iota(jnp.int32, (1, blk), 1))
    qts = [qt_ref[0, _head_slice(h), :] for h in range(N_HEADS)]
    sels = []
    for h in range(N_HEADS):
        km_hi, km_lo = _split_bf16(km[:, _head_slice(h)])
        gate = _dot(km_hi, qts[h]) + _dot(km_lo, qts[h])
        gate = jnp.where(brow < i, gate, -jnp.inf)
        sel = jnp.zeros((nb, blk), F32)
        for _ in range(n_sel):
            gm = jnp.max(gate, axis=0, keepdims=True)
            is_m = (gate == gm) & (gm > -jnp.inf)
            first = jnp.min(jnp.where(is_m, browf, float(nb)), axis=0, keepdims=True)
            pick = browf == first
            sel = jnp.where(pick, 1.0, sel)
            gate = jnp.where(pick, -jnp.inf, gate)
        sels.append(sel)
    acc_ref[...] = jnp.zeros(acc_ref.shape, F32)

    def tile(j, h, mask=None, seen=False):
        rows = pl.ds(pl.multiple_of(j * blk, blk), blk)
        s = _dot(k_ref[rows, _head_slice(h)], qts[h])
        if mask is not None:
            s = jnp.where(mask, s, NEG_BIG)
        ok = jnp.sum(jnp.where(brow == j, sels[h], 0.0), axis=0, keepdims=True) > 0.0 if seen else None
        return s, _with_ones_row(vt_ref[j, _head_slice(h), :]), ok

    def pair(j2, ms):
        parts = [[tile(2 * j2, h, seen=True), tile(2 * j2 + 1, h, seen=True)] for h in range(N_HEADS)]
        return tuple(_online_update(parts, ms, acc_ref))

    ms = lax.fori_loop(0, i // 2, pair, _init_max(N_HEADS, blk))

    @pl.when(i % 2 == 1)
    def _():
        _online_update([[tile(i - 1, h, seen=True), tile(i, h, mask=causal)] for h in range(N_HEADS)],
                       ms, acc_ref)

    @pl.when(i % 2 == 0)
    def _():
        _online_update([[tile(i, h, mask=causal)] for h in range(N_HEADS)], ms, acc_ref)

    for h in range(N_HEADS):
        o_ref[0, _head_slice(h), :] = _normalized(acc_ref, h, HEAD_DIM).astype(o_ref.dtype)


def _moba(qt, k, vt, kmean, batch, seq):
    blk = MOBA_BLOCK
    nb = seq // blk
    n_sel = min(MOBA_TOPK, nb - 1)
    tspec = pl.BlockSpec((1, HW, blk), lambda b, i: (b, 0, i))
    return pl.pallas_call(
        functools.partial(_moba_kernel, n_sel=n_sel),
        grid=(batch, nb),
        in_specs=[tspec,
                  pl.BlockSpec((seq, HW), lambda b, i: (b, 0)),
                  pl.BlockSpec((nb, vt.shape[1], blk), lambda b, i: (b, 0, 0)),
                  pl.BlockSpec((1, nb, HW), lambda b, i: (b, 0, 0))],
        out_specs=tspec,
        out_shape=jax.ShapeDtypeStruct((batch, HW, seq), BF16),
        scratch_shapes=[pltpu.VMEM((N_HEADS, HEAD_DIM + V_PAD, blk), F32)],
        compiler_params=_params(2),
        name="moba",
    )(qt, k, vt, kmean)


def _bit_planes(words):
    a = list(words)
    assert len(a) == 32
    mask, j = 0x0000FFFF, 16
    while j:
        k = 0
        while k < 32:
            t = (a[k] ^ (a[k + j] >> j)) & mask
            a[k] = a[k] ^ t
            a[k + j] = a[k + j] ^ (t << j)
            k = (k + j + 1) & ~j
        j >>= 1
        mask = (mask ^ (mask << j)) & 0xFFFFFFFF
    return a


def _dsa_kernel(qit_ref, wi_ref, ki_ref, qt_ref, k_ref, vt_ref, o_ref, plane_ref, sel_ref, acc_ref, *, n_keep):
    i = pl.program_id(1)
    blk = ROW_TILE
    n_chunk = i + 1
    n_planes, n_slots, sub = plane_ref.shape[:3]
    v_bits, s_bits = (n_planes - 1).bit_length(), (sub - 1).bit_length()
    idx_bits = v_bits + s_bits + (n_slots - 1).bit_length()

    @pl.when(i == 0)
    def _():
        plane_ref[:, 1:] = jnp.zeros((n_planes, n_slots - 1) + plane_ref.shape[2:], jnp.int32)

    w_t = wi_ref[...].T
    qpos = i * blk + lax.broadcasted_iota(jnp.int32, (1, blk), 1)
    krow = lax.broadcasted_iota(jnp.int32, (blk, 1), 0)

    def score_chunk(c, carry):
        kc = ki_ref[pl.ds(pl.multiple_of(c * blk, blk), blk), :]
        lgs = [_dot(kc, qit_ref[0, _head_slice(h, IDX_DIM), :]) for h in range(IDX_HEADS)]
        sc = jnp.zeros((blk, blk), F32)
        for h in range(IDX_HEADS):
            sc = sc + w_t[h:h + 1, :] * jnp.maximum(lgs[h], 0.0)
        sc = sc + 0.0
        bits = pltpu.bitcast(sc, jnp.int32)
        key = jnp.where(bits < 0, bits ^ 0x7FFFFFFF, bits)
        key = jnp.where(c * blk + krow <= qpos, key, INT_MIN)
        words = (key ^ INT_MIN).reshape(n_planes, sub, blk)
        planes = _bit_planes([words[v] for v in range(n_planes)])
        for b in range(n_planes):
            plane_ref[b, c] = planes[n_planes - 1 - b]
        return carry

    lax.fori_loop(0, n_chunk, score_chunk, 0)

    keep = float(n_keep)

    def step(plane, eq, gt):
        cand = gt | (eq & plane)
        per_row = jnp.sum(lax.population_count(cand), axis=0)
        take = jnp.sum(per_row.astype(F32), axis=0, keepdims=True) >= keep
        return eq & jnp.where(take, plane, ~plane), jnp.where(take, gt, cand)

    def index_plane(t, shape):
        if t < s_bits:
            s = lax.broadcasted_iota(jnp.int32, shape, 1)
            return jnp.where(((s >> t) & 1) == 0, -1, 0)
        if t < s_bits + v_bits:
            word = sum(1 << j for j in range(n_planes) if (((n_planes - 1 - j) >> (t - s_bits)) & 1) == 0)
            return jnp.full(shape, word - (1 << 32) if word >= (1 << 31) else word, jnp.int32)
        slot = lax.broadcasted_iota(jnp.int32, shape, 0)
        return jnp.where(((slot >> (t - s_bits - v_bits)) & 1) == 0, -1, 0)

    def select(n_used, lo):
        shape = (n_used, sub, LANES)
        lanes = slice(lo, lo + LANES)

        def score_bit(t, carry):
            return step(plane_ref[n_planes - 1 - t, 0:n_used, :, lanes], *carry)

        eq, gt = lax.fori_loop(0, n_planes, score_bit,
                               (jnp.full(shape, -1, jnp.int32), jnp.zeros(shape, jnp.int32)))
        for t in reversed(range(idx_bits)):
            eq, gt = step(index_plane(t, shape), eq, gt)
        slot = lax.broadcasted_iota(jnp.int32, shape, 0)
        s = lax.broadcasted_iota(jnp.int32, shape, 1)
        qoff = lo + lax.broadcasted_iota(jnp.int32, shape, 2)
        reach = jnp.left_shift(-1, (n_planes - 1) - ((qoff - s) >> s_bits))
        valid = jnp.where(slot < i, -1, jnp.where((slot == i) & (qoff >= s), reach, 0))
        return (eq | gt) & valid

    half_slots = n_slots // 2
    for n_used, wanted in ((half_slots, n_chunk <= half_slots), (n_slots, n_chunk > half_slots)):
        @pl.when(wanted)
        def _():
            for lo in range(0, blk, LANES):
                sel_ref[0:n_used, :, lo:lo + LANES] = select(n_used, lo)
                if n_used < n_slots:
                    sel_ref[n_used:, :, lo:lo + LANES] = jnp.zeros((n_slots - n_used, sub, LANES), jnp.int32)

    qts = [qt_ref[0, _head_slice(h), :] for h in range(N_HEADS)]
    acc_ref[...] = jnp.zeros(acc_ref.shape, F32)

    def pair(c2, ms):
        parts = [[] for _ in range(N_HEADS)]
        for c in (2 * c2, 2 * c2 + 1):
            rows = pl.ds(pl.multiple_of(c * blk, blk), blk)
            w = sel_ref[c]
            allowed = jnp.concatenate([jnp.left_shift(w, v) for v in range(n_planes)], axis=0) < 0
            for h in range(N_HEADS):
                s = jnp.where(allowed, _dot(k_ref[rows, _head_slice(h)], qts[h]), NEG_BIG)
                parts[h].append((s, _with_ones_row(vt_ref[c, _head_slice(h), :]), None))
        return tuple(_online_update(parts, ms, acc_ref))

    lax.fori_loop(0, (n_chunk + 1) // 2, pair, _init_max(N_HEADS, blk))
    for h in range(N_HEADS):
        o_ref[0, _head_slice(h), :] = _normalized(acc_ref, h, HEAD_DIM).astype(o_ref.dtype)


def _dsa(qit, wi, ki, qt, k, vt, batch, seq):
    blk = ROW_TILE
    nq = seq // blk
    n_keep = min(DSA_TOPK_MAX, seq // 4)
    n_planes = 32
    tspec = lambda w: pl.BlockSpec((1, w, blk), lambda b, i: (b, 0, i))
    full = lambda w: pl.BlockSpec((seq, w), lambda b, i: (b, 0))
    return pl.pallas_call(
        functools.partial(_dsa_kernel, n_keep=n_keep),
        grid=(batch, nq),
        in_specs=[tspec(IDX_HEADS * IDX_DIM), pl.BlockSpec((blk, LANES), lambda b, i: (b * nq + i, 0)),
                  full(IDX_DIM), tspec(HW), full(HW),
                  pl.BlockSpec((nq, vt.shape[1], blk), lambda b, i: (b, 0, 0))],
        out_specs=tspec(HW),
        out_shape=jax.ShapeDtypeStruct((batch, HW, seq), BF16),
        scratch_shapes=[pltpu.VMEM((n_planes, nq, blk // n_planes, blk), jnp.int32),
                        pltpu.VMEM((nq, blk // n_planes, blk), jnp.int32),
                        pltpu.VMEM((N_HEADS, HEAD_DIM + V_PAD, blk), F32)],
        compiler_params=_params(2),
        name="dsa",
    )(qit, wi, ki, qt, k, vt)


DIFF_GROUP = 4


def _diff_kernel(q1t_ref, q2t_ref, k1_ref, k2_ref, vt_ref, dl_ref, gain_ref, o_ref, acc_ref, *, lam_init):
    i = pl.program_id(1)
    blk = ROW_TILE
    dl = dl_ref[...]
    lam = (jnp.exp(jnp.sum(dl[0:1] * dl[1:2], axis=-1, keepdims=True))
           - jnp.exp(jnp.sum(dl[2:3] * dl[3:4], axis=-1, keepdims=True)) + lam_init)
    causal = (lax.broadcasted_iota(jnp.int32, (blk, 1), 0)
              <= lax.broadcasted_iota(jnp.int32, (1, blk), 1))
    maps = ((q1t_ref, k1_ref), (q2t_ref, k2_ref))

    for h0 in range(0, N_HEADS, DIFF_GROUP):
        chains = [(h, mp) for h in range(h0, h0 + DIFF_GROUP) for mp in range(2)]
        qts = [maps[mp][0][0, _head_slice(h), :] for h, mp in chains]
        acc_ref[...] = jnp.zeros(acc_ref.shape, F32)

        def tile(j, c, mask=None):
            h, mp = chains[c]
            rows = pl.ds(pl.multiple_of(j * blk, blk), blk)
            s = _dot(maps[mp][1][rows, _head_slice(h)], qts[c])
            if mask is not None:
                s = jnp.where(mask, s, NEG_BIG)
            return s, _with_ones_row(vt_ref[j, _head_slice(h, C_VDIM), :]), None

        def pair(j2, ms):
            parts = [[tile(2 * j2, c), tile(2 * j2 + 1, c)] for c in range(len(chains))]
            return tuple(_online_update(parts, ms, acc_ref))

        ms = lax.fori_loop(0, i // 2, pair, _init_max(len(chains), blk))

        @pl.when(i % 2 == 1)
        def _():
            _online_update([[tile(i - 1, c), tile(i, c, causal)] for c in range(len(chains))], ms, acc_ref)

        @pl.when(i % 2 == 0)
        def _():
            _online_update([[tile(i, c, causal)] for c in range(len(chains))], ms, acc_ref)

        for g in range(DIFF_GROUP):
            h = h0 + g
            o = _normalized(acc_ref, 2 * g, C_VDIM) - lam * _normalized(acc_ref, 2 * g + 1, C_VDIM)
            ms = jnp.mean(o * o, axis=0, keepdims=True)
            o = o * lax.rsqrt(ms + EPS) * gain_ref[...] * (1.0 - lam_init)
            o_ref[0, _head_slice(h, C_VDIM), :] = o.astype(o_ref.dtype)


def _diff(q1t, q2t, k1, k2, vt, dl, gain, lam_init, batch, seq):
    blk = ROW_TILE
    nq = seq // blk
    tspec = lambda w: pl.BlockSpec((1, w, blk), lambda b, i: (b, 0, i))
    kspec = pl.BlockSpec((seq, HW), lambda b, i: (b, 0))
    vw = N_HEADS * C_VDIM
    return pl.pallas_call(
        functools.partial(_diff_kernel, lam_init=lam_init),
        grid=(batch, nq),
        in_specs=[tspec(HW), tspec(HW), kspec, kspec,
                  pl.BlockSpec((nq, vt.shape[1], blk), lambda b, i: (b, 0, 0)),
                  pl.BlockSpec((4, HEAD_DIM), lambda b, i: (0, 0)),
                  pl.BlockSpec((C_VDIM, 1), lambda b, i: (0, 0))],
        out_specs=tspec(vw),
        out_shape=jax.ShapeDtypeStruct((batch, vw, seq), BF16),
        scratch_shapes=[pltpu.VMEM((2 * DIFF_GROUP, C_VDIM + V_PAD, blk), F32)],
        compiler_params=_params(2),
        name="diff",
    )(q1t, q2t, k1, k2, vt, dl, gain)


def _merge_kernel(x_ref, oa_ref, ob_ref, oc_ref, sg_ref, wa_ref, wb_ref, wc_ref, wo_ref, g_ref,
                  wr_hi_ref, wr_lo_ref, br_ref, x1_ref, hn_ref, route_ref):
    merged = (sg_ref[:, 0:D_MODEL].astype(F32) * _dot(oa_ref[...], wa_ref[...])
              + sg_ref[:, D_MODEL:2 * D_MODEL].astype(F32) * _dot(ob_ref[...], wb_ref[...])
              + sg_ref[:, 2 * D_MODEL:3 * D_MODEL].astype(F32) * _dot(oc_ref[...], wc_ref[...]))
    x1 = x_ref[...] + _dot(merged.astype(BF16), wo_ref[...])
    x1_ref[...] = x1
    ms = jnp.mean(x1 * x1, axis=-1, keepdims=True)
    hn = x1 * lax.rsqrt(ms + EPS) * g_ref[...]
    hn_ref[...] = hn

    hi, lo = _split_bf16(hn)
    lg = (_dot(hi, wr_hi_ref[...]) + _dot(lo, wr_hi_ref[...]) + _dot(hi, wr_lo_ref[...])
          + br_ref[...])
    lane = lax.broadcasted_iota(jnp.int32, lg.shape, 1)
    lanef = lane.astype(F32)
    far = float(LANES)
    is_g = (lane >= N_EXPERTS) & (lane < N_EXPERTS + N_GROUPS)
    gl = jnp.where(is_g, lg, -jnp.inf)
    gmax = jnp.max(gl, axis=-1, keepdims=True)
    gidx = jnp.min(jnp.where(gl == gmax, lanef, far), axis=-1, keepdims=True) - float(N_EXPERTS)
    g_w = 1.0 / jnp.sum(jnp.where(is_g, jnp.exp(gl - gmax), 0.0), axis=-1, keepdims=True)
    in_group = (lane < N_EXPERTS) & ((lane // EXPERTS_PER_GROUP).astype(F32) == gidx)
    el = jnp.where(in_group, lg, -jnp.inf)
    e1 = jnp.max(el, axis=-1, keepdims=True)
    i1 = jnp.min(jnp.where(el == e1, lanef, far), axis=-1, keepdims=True)
    el2 = jnp.where(lanef == i1, -jnp.inf, el)
    e2 = jnp.max(el2, axis=-1, keepdims=True)
    i2 = jnp.min(jnp.where(el2 == e2, lanef, far), axis=-1, keepdims=True)
    t = jnp.exp(e2 - e1)
    w1 = g_w / (1.0 + t)
    w2 = g_w * t / (1.0 + t)
    route_ref[...] = jnp.where(lane == 0, i1, jnp.where(lane == 1, i2, jnp.where(lane == 2, w1, jnp.where(lane == 3, w2, 0.0))))


def _merge(x2, oa, ob, oc, sg, wa, wb, wc, wo, norm_g, wr_hi, wr_lo, br):
    n = x2.shape[0]
    tm = min(DENSE_TILE, n)
    row = lambda w: pl.BlockSpec((tm, w), lambda i: (i, 0))
    const = lambda a: pl.BlockSpec(a.shape, lambda i: (0, 0))
    return pl.pallas_call(
        _merge_kernel,
        grid=(n // tm,),
        in_specs=[row(D_MODEL), row(HW), row(HW), row(N_HEADS * C_VDIM), row(3 * D_MODEL),
                  const(wa), const(wb), const(wc), const(wo), const(norm_g), const(wr_hi),
                  const(wr_lo), const(br)],
        out_specs=[row(D_MODEL), row(D_MODEL), row(LANES)],
        out_shape=[jax.ShapeDtypeStruct((n, D_MODEL), F32), jax.ShapeDtypeStruct((n, D_MODEL), F32),
                   jax.ShapeDtypeStruct((n, LANES), F32)],
        compiler_params=_params(1),
        name="merge",
    )(x2, oa, ob, oc, sg, wa, wb, wc, wo, norm_g, wr_hi, wr_lo, br)


MOE_BLOCK = 512
META_ROWS = 8
META_USED = 3 * LANES
META_END = 4 * LANES
META_PADDED = 5 * LANES


def _lane_prefix_sum(x):
    lane = lax.broadcasted_iota(jnp.int32, x.shape, 1)
    shift = 1
    while shift < LANES:
        x = x + jnp.where(lane >= shift, pltpu.roll(x, shift, 1), 0.0)
        shift *= 2
    return x


def _positions_kernel(route_ref, dest_ref, meta_ref, cnt_ref, base_ref):
    phase = pl.program_id(0)
    t = pl.program_id(1)
    tm = route_ref.shape[0]
    route = route_ref[...]
    lane = lax.broadcasted_iota(jnp.int32, route.shape, 1)
    lanef = lane.astype(F32)
    e1 = route[:, 0:1]
    e2 = route[:, 1:2]
    uses = jnp.where((lanef == e1) | (lanef == e2), 1.0, 0.0)
    tile_cnt = jnp.sum(uses, axis=0, keepdims=True)

    @pl.when((phase == 0) & (t == 0))
    def _():
        cnt_ref[...] = jnp.zeros(cnt_ref.shape, F32)

    @pl.when(phase == 0)
    def _():
        cnt_ref[...] += tile_cnt

    @pl.when((phase == 1) & (t == 0))
    def _():
        cnt = cnt_ref[...]
        padded = jnp.floor((cnt + (MOE_BLOCK - 1)) * (1.0 / MOE_BLOCK)) * MOE_BLOCK
        end = _lane_prefix_sum(padded)
        base_ref[...] = end - padded
        cnt_ref[...] = jnp.zeros(cnt_ref.shape, F32)
        lane1 = lax.broadcasted_iota(jnp.int32, (1, LANES), 1)
        row = lax.broadcasted_iota(jnp.int32, (META_ROWS, LANES), 0)
        col = lax.broadcasted_iota(jnp.int32, (META_ROWS, LANES), 1)
        first_row = ((row * LANES + col) * MOE_BLOCK).astype(F32)
        owner = jnp.zeros((META_ROWS, LANES), F32)
        for e in range(N_EXPERTS):
            end_e = jnp.sum(jnp.where(lane1 == e, end, 0.0), axis=-1, keepdims=True)
            owner = owner + jnp.where(end_e <= first_row, 1.0, 0.0)
        owner = jnp.minimum(owner, float(N_EXPERTS - 1))
        used = jnp.sum(jnp.where(lane1 == N_EXPERTS - 1, end, 0.0), axis=-1, keepdims=True) * (1.0 / MOE_BLOCK)
        meta = jnp.where(row == META_USED // LANES, used,
                         jnp.where(row == META_END // LANES, end,
                                   jnp.where(row == META_PADDED // LANES, padded, owner)))
        meta_ref[...] = meta.astype(jnp.int32)

    @pl.when(phase == 1)
    def _():
        before = (lax.broadcasted_iota(jnp.int32, (tm, tm), 1)
                  < lax.broadcasted_iota(jnp.int32, (tm, tm), 0)).astype(BF16)
        rank = _dot(before, uses.astype(BF16))
        pos = base_ref[...] + cnt_ref[...] + rank
        d1 = jnp.sum(jnp.where(lanef == e1, pos, 0.0), axis=-1, keepdims=True)
        d2 = jnp.sum(jnp.where(lanef == e2, pos, 0.0), axis=-1, keepdims=True)
        dest_ref[...] = jnp.where(lane == 0, d1, jnp.where(lane == 1, d2, 0.0)).astype(jnp.int32)
        cnt_ref[...] += tile_cnt


def _positions(route):
    n = route.shape[0]
    tm = min(4 * ROW_TILE, n)
    return pl.pallas_call(
        _positions_kernel,
        grid=(2, n // tm),
        in_specs=[pl.BlockSpec((tm, LANES), lambda p, t: (t, 0))],
        out_specs=[pl.BlockSpec((tm, LANES), lambda p, t: (t * p, 0)),
                   pl.BlockSpec((META_ROWS, LANES), lambda p, t: (0, 0))],
        out_shape=[jax.ShapeDtypeStruct((n, LANES), jnp.int32),
                   jax.ShapeDtypeStruct((META_ROWS, LANES), jnp.int32)],
        scratch_shapes=[pltpu.VMEM((1, LANES), F32), pltpu.VMEM((1, LANES), F32)],
        compiler_params=_params(2),
        name="moe_positions",
    )(route)


def _row_copy(src_ref, src_row, dst_ref, dst_row, sem):
    return pltpu.make_async_copy(src_ref.at[pl.ds(src_row, 1), :], dst_ref.at[pl.ds(dst_row, 1), :], sem)


def _dispatch_kernel(meta_ref, dest_ref, hn_ref, xs_ref, zero_ref, sem):
    tm = hn_ref.shape[0]

    @pl.when(pl.program_id(0) == 0)
    def _():
        zero_ref[...] = jnp.zeros(zero_ref.shape, F32)

        def fill(e):
            end = pl.multiple_of(meta_ref[META_END + e], MOE_BLOCK)
            return pltpu.make_async_copy(zero_ref, xs_ref.at[pl.ds(end - MOE_BLOCK, MOE_BLOCK), :], sem)

        for e in range(N_EXPERTS):
            @pl.when(meta_ref[META_PADDED + e] > 0)
            def _():
                fill(e).start()
        for e in range(N_EXPERTS):
            @pl.when(meta_ref[META_PADDED + e] > 0)
            def _():
                fill(e).wait()

        def spare(b):
            return pltpu.make_async_copy(
                zero_ref, xs_ref.at[pl.ds(pl.multiple_of(b * MOE_BLOCK, MOE_BLOCK), MOE_BLOCK), :], sem)

        n_blocks = xs_ref.shape[0] // MOE_BLOCK
        lax.fori_loop(meta_ref[META_USED], n_blocks, lambda b, c: (spare(b).start(), c)[1], 0)
        lax.fori_loop(meta_ref[META_USED], n_blocks, lambda b, c: (spare(b).wait(), c)[1], 0)

    def copies(r):
        return (_row_copy(hn_ref, r, xs_ref, dest_ref[0, 0, 2 * r], sem),
                _row_copy(hn_ref, r, xs_ref, dest_ref[0, 0, 2 * r + 1], sem))

    def start(r, carry):
        for cp in copies(r):
            cp.start()
        return carry

    lax.fori_loop(0, tm, start, 0, unroll=8)
    for _ in range(2):
        pltpu.make_async_copy(hn_ref, xs_ref.at[pl.ds(0, tm), :], sem).wait()


def _dispatch(meta, dest3, hn, n_rows):
    n = hn.shape[0]
    tm = dest3.shape[2] // 2
    return pl.pallas_call(
        _dispatch_kernel,
        grid_spec=pltpu.PrefetchScalarGridSpec(
            num_scalar_prefetch=1,
            grid=(n // tm,),
            in_specs=[pl.BlockSpec((1, 1, 2 * tm), lambda t, m: (t, 0, 0), memory_space=pltpu.SMEM),
                      pl.BlockSpec((tm, D_MODEL), lambda t, m: (t, 0))],
            out_specs=pl.BlockSpec(memory_space=pl.ANY),
            scratch_shapes=[pltpu.VMEM((MOE_BLOCK, D_MODEL), F32), pltpu.SemaphoreType.DMA(())]),
        out_shape=jax.ShapeDtypeStruct((n_rows, D_MODEL), F32),
        compiler_params=_params(1),
        name="moe_dispatch",
    )(meta, dest3, hn)


def _expert_kernel(meta_ref, xs_ref, wg_ref, wu_ref, wd_ref, y_ref, wg_bf, wu_bf, wd_bf):
    b = pl.program_id(0)
    holds_rows = b < meta_ref[META_USED]
    new_expert = (b == 0) | (meta_ref[b] != meta_ref[jnp.maximum(b - 1, 0)])

    @pl.when(holds_rows & new_expert)
    def _():
        wg_bf[...] = wg_ref[0].astype(BF16)
        wu_bf[...] = wu_ref[0].astype(BF16)
        wd_bf[...] = wd_ref[0].astype(BF16)

    @pl.when(holds_rows)
    def _():
        x = xs_ref[...].astype(BF16)
        g = _dot(x, wg_bf[...])
        u = _dot(x, wu_bf[...])
        hid = g * (1.0 / (1.0 + jnp.exp(-g))) * u
        y_ref[...] = _dot(hid.astype(BF16), wd_bf[...])

    @pl.when(jnp.logical_not(holds_rows))
    def _():
        y_ref[...] = jnp.zeros(y_ref.shape, F32)


def _experts(meta, xs, wg, wu, wd, layer):
    n_blocks = xs.shape[0] // MOE_BLOCK
    rows = pl.BlockSpec((MOE_BLOCK, D_MODEL), lambda b, m: (b, 0))
    weight = lambda shape: pl.BlockSpec(
        (None, 1) + shape, lambda b, m: (layer, m[jnp.minimum(b, m[META_USED] - 1)], 0, 0))
    return pl.pallas_call(
        _expert_kernel,
        grid_spec=pltpu.PrefetchScalarGridSpec(
            num_scalar_prefetch=1,
            grid=(n_blocks,),
            in_specs=[rows, weight((D_MODEL, D_EXPERT)), weight((D_MODEL, D_EXPERT)),
                      weight((D_EXPERT, D_MODEL))],
            out_specs=rows,
            scratch_shapes=[pltpu.VMEM((D_MODEL, D_EXPERT), BF16), pltpu.VMEM((D_MODEL, D_EXPERT), BF16),
                            pltpu.VMEM((D_EXPERT, D_MODEL), BF16)]),
        out_shape=jax.ShapeDtypeStruct(xs.shape, F32),
        compiler_params=_params(1),
        name="moe_experts",
    )(meta, xs, wg, wu, wd)


def _combine_kernel(dest_ref, x1_ref, route_ref, y_ref, o_ref, buf_ref, sem):
    tm = x1_ref.shape[0]

    def copies(r):
        return (_row_copy(y_ref, dest_ref[0, 0, 2 * r], buf_ref.at[0], r, sem),
                _row_copy(y_ref, dest_ref[0, 0, 2 * r + 1], buf_ref.at[1], r, sem))

    def start(r, carry):
        for cp in copies(r):
            cp.start()
        return carry

    lax.fori_loop(0, tm, start, 0, unroll=8)
    for slot in range(2):
        pltpu.make_async_copy(y_ref.at[pl.ds(0, tm), :], buf_ref.at[slot], sem).wait()
    route = route_ref[...]
    o_ref[...] = x1_ref[...] + route[:, 2:3] * buf_ref[0] + route[:, 3:4] * buf_ref[1]


def _combine(dest3, x1, route, y):
    n = x1.shape[0]
    tm = dest3.shape[2] // 2
    row = lambda w: pl.BlockSpec((tm, w), lambda t: (t, 0))
    return pl.pallas_call(
        _combine_kernel,
        grid=(n // tm,),
        in_specs=[pl.BlockSpec((1, 1, 2 * tm), lambda t: (t, 0, 0), memory_space=pltpu.SMEM),
                  row(D_MODEL), row(LANES), pl.BlockSpec(memory_space=pl.ANY)],
        out_specs=row(D_MODEL),
        out_shape=jax.ShapeDtypeStruct((n, D_MODEL), F32),
        scratch_shapes=[pltpu.VMEM((2, tm, D_MODEL), F32), pltpu.SemaphoreType.DMA(())],
        compiler_params=_params(1),
        name="moe_combine",
    )(dest3, x1, route, y)


def _moe(x1, hn, route, wg, wu, wd, layer):
    n = x1.shape[0]
    tm = min(ROW_TILE, n)
    n_blocks = -(-(2 * n + N_EXPERTS * (MOE_BLOCK - 1)) // MOE_BLOCK)
    assert n_blocks <= META_USED
    dest, meta = _positions(route)
    meta = meta.reshape(-1)
    dest3 = dest[:, :2].reshape(n // tm, 1, 2 * tm)
    xs = _dispatch(meta, dest3, hn, n_blocks * MOE_BLOCK)
    y = _experts(meta, xs, wg, wu, wd, layer)
    return _combine(dest3, x1, route, y)


def _rope_tables(seq):
    inv_freq = 1.0 / (ROPE_THETA ** (jnp.arange(0, HEAD_DIM, 2, dtype=F32) / HEAD_DIM))
    ang = jnp.arange(seq, dtype=F32)[:, None] * inv_freq[None, :]
    cos, sin = jnp.cos(ang), jnp.sin(ang)
    cos_t = jnp.tile(jnp.concatenate([cos, cos], axis=-1), (1, N_HEADS))
    sin_t = jnp.tile(jnp.concatenate([-sin, sin], axis=-1), (1, N_HEADS))
    return cos_t, sin_t


def _to_t(v, batch, n_chunk):
    feat = v.shape[1]
    return (v.reshape(batch, n_chunk, ROW_TILE, feat).transpose(0, 1, 3, 2)
            .reshape(batch * n_chunk, feat, ROW_TILE))


def _from_t(o_t):
    b, feat, seq = o_t.shape
    return o_t.transpose(0, 2, 1).reshape(b * seq, feat)


def kernel(x, norm_attn, w_in, qk_gain, idx_k_gain, diff_lambda, diff_subln_gain, w_proj_a, w_proj_b, w_proj_c, w_out, norm_ffn, w_group, b_group, w_router, b_router, w_e_gate, w_e_up, w_e_down):
    batch, seq, d = x.shape
    assert d == D_MODEL and seq % (2 * ROW_TILE) == 0 and ROW_TILE == MOBA_BLOCK
    n = batch * seq
    nq = seq // ROW_TILE
    depth = w_in.shape[0]
    cos_t, sin_t = _rope_tables(seq)
    x2 = x.reshape(n, d)
    for l in range(depth):
        w_pad = jnp.concatenate(
            [w_in[l][:, :KW_SRC], jnp.zeros((d, LANES - IDX_DIM - IDX_HEADS), F32), w_in[l][:, KW_SRC:]],
            axis=1).astype(BF16)
        gains = jnp.tile(qk_gain[l][jnp.array([0, 1, 2, 3, 4, 4, 5, 5])], (1, N_HEADS))
        kgain = jnp.pad(idx_k_gain[l], (0, LANES - IDX_DIM))[None, :]
        (qa, ka, va, qb, kb, vb, qi, ki, wi, q1, q2, k1, k2, vc, sg, kmean) = _project(
            x2, norm_attn[l][None, :], w_pad, cos_t, sin_t, gains, kgain, seq)

        feat_major = lambda t: t.reshape(batch, seq, t.shape[1]).transpose(0, 2, 1)
        o_a = _from_t(_moba(feat_major(qa), ka, _to_t(va, batch, nq), kmean.reshape(batch, nq, HW),
                            batch, seq))
        o_b = _from_t(_dsa(feat_major(qi), wi, ki, feat_major(qb), kb, _to_t(vb, batch, nq), batch, seq))
        lam_init = 0.8 - 0.6 * math.exp(-0.3 * l)
        o_c = _from_t(_diff(feat_major(q1), feat_major(q2), k1, k2, _to_t(vc, batch, nq), diff_lambda[l],
                            diff_subln_gain[l][:, None], lam_init, batch, seq))

        w_r = jnp.concatenate([w_router[l], w_group[l],
                               jnp.zeros((d, LANES - N_EXPERTS - N_GROUPS), F32)], axis=1)
        wr_hi = w_r.astype(BF16)
        wr_lo = (w_r - wr_hi.astype(F32)).astype(BF16)
        b_r = jnp.concatenate([b_router[l], b_group[l],
                               jnp.zeros((LANES - N_EXPERTS - N_GROUPS,), F32)])[None, :]
        x1, hn, route = _merge(x2, o_a, o_b, o_c, sg, w_proj_a[l].astype(BF16), w_proj_b[l].astype(BF16),
                            w_proj_c[l].astype(BF16), w_out[l].astype(BF16), norm_ffn[l][None, :],
                            wr_hi, wr_lo, b_r)
        x2 = _moe(x1, hn, route, w_e_gate, w_e_up, w_e_down, l)
    return x2.reshape(batch, seq, d)
```

```python
import functools
import math

import jax
import jax.numpy as jnp
from jax import lax
from jax.experimental import pallas as pl
from jax.experimental.pallas import tpu as pltpu

F32 = jnp.float32
BF16 = jnp.bfloat16

D_MODEL = 1024
HEAD_DIM = 64
ROPE_THETA = 10000.0
EPS = 1e-6
N_HEADS = 4
MOBA_BLOCK = 256
MOBA_TOPK = 3
IDX_HEADS = 8
IDX_DIM = 64
DSA_TOPK_MAX = 256
C_VDIM = 2 * HEAD_DIM
N_GROUPS = 4
EXPERTS_PER_GROUP = 8
N_EXPERTS = N_GROUPS * EXPERTS_PER_GROUP
D_EXPERT = 512

HW = N_HEADS * HEAD_DIM
LANES = 128
ROW_TILE = 256
DENSE_TILE = 512
VMEM_LIMIT = 56 * 1024 * 1024

_SEG = {}
_off = 0
for _name, _w in (("qa", HW), ("ka", HW), ("va", HW), ("qb", HW), ("kb", HW), ("vb", HW),
                  ("qi", IDX_HEADS * IDX_DIM), ("kw", LANES), ("q1", HW), ("q2", HW), ("k1", HW),
                  ("k2", HW), ("vc", N_HEADS * C_VDIM), ("ga", D_MODEL), ("gb", D_MODEL),
                  ("gc", D_MODEL)):
    _SEG[_name] = (_off, _w)
    _off += _w
D_IN_PAD = _off
KW_SRC = 6 * HW + IDX_HEADS * IDX_DIM + IDX_DIM + IDX_HEADS

NEG_BIG = -1e30
M_FLOOR = -1e20
INT_MIN = -(2 ** 31)
LOG2E = math.log2(math.e)
Q_SCALE = HEAD_DIM ** -0.5 * LOG2E
V_PAD = 16
NT_DIMS = (((1,), (1,)), ((), ()))


def _params(n_axes):
    return pltpu.CompilerParams(dimension_semantics=("arbitrary",) * n_axes,
                                vmem_limit_bytes=VMEM_LIMIT)


def _dot(a, b):
    return jnp.dot(a, b, preferred_element_type=F32)


def _dot_nt(a, b):
    return lax.dot_general(a, b, NT_DIMS, preferred_element_type=F32)


def _split_bf16(a):
    hi = a.astype(BF16)
    return hi, (a - hi.astype(F32)).astype(BF16)


def _swap_halves(y, width):
    lane = lax.broadcasted_iota(jnp.int32, y.shape, 1)
    first = (lane % HEAD_DIM) < (HEAD_DIM // 2)
    return jnp.where(first, pltpu.roll(y, width - HEAD_DIM // 2, 1), pltpu.roll(y, HEAD_DIM // 2, 1))


def _proj_kernel(x_ref, g_ref, w_ref, cos_ref, sin_ref, gain_ref, kgain_ref,
                 qa_ref, ka_ref, va_ref, qb_ref, kb_ref, vb_ref, qi_ref, ki_ref, wi_ref,
                 q1_ref, q2_ref, k1_ref, k2_ref, vc_ref, sg_ref, kmean_ref):
    x = x_ref[...]
    ms = jnp.mean(x * x, axis=-1, keepdims=True)
    h = (x * lax.rsqrt(ms + EPS) * g_ref[...]).astype(BF16)
    cos = cos_ref[...]
    sin = sin_ref[...]
    r = lax.broadcasted_iota(jnp.int32, (HW, HW), 0) // HEAD_DIM
    c = lax.broadcasted_iota(jnp.int32, (HW, HW), 1) // HEAD_DIM
    head_ones = (r == c).astype(BF16)

    def seg(name, lo=0, width=None):
        off, w = _SEG[name]
        width = w if width is None else width
        return _dot(h, w_ref[:, off + lo:off + lo + width])

    def rope(y):
        return y * cos + _swap_halves(y, HW) * sin

    def norm_rope(t, gain_row):
        hi, lo = _split_bf16(t * t)
        ss = _dot(hi, head_ones) + _dot(lo, head_ones)
        yn = t * lax.rsqrt(ss * (1.0 / HEAD_DIM) + EPS) * gain_ref[gain_row:gain_row + 1, :]
        return rope(yn)

    qa_ref[...] = (norm_rope(seg("qa"), 0) * Q_SCALE).astype(BF16)
    ka = norm_rope(seg("ka"), 1)
    ka_ref[...] = ka.astype(BF16)
    for blk in range(ka.shape[0] // MOBA_BLOCK):
        kmean_ref[blk] = jnp.mean(ka[blk * MOBA_BLOCK:(blk + 1) * MOBA_BLOCK], axis=0, keepdims=True)
    va_ref[...] = seg("va").astype(BF16)
    qb_ref[...] = (norm_rope(seg("qb"), 2) * Q_SCALE).astype(BF16)
    kb_ref[...] = norm_rope(seg("kb"), 3).astype(BF16)
    vb_ref[...] = seg("vb").astype(BF16)
    for half in range(2):
        qi_ref[:, half * HW:(half + 1) * HW] = rope(seg("qi", half * HW, HW)).astype(BF16)

    t = seg("kw")
    lane = lax.broadcasted_iota(jnp.int32, t.shape, 1)
    is_k = lane < IDX_DIM
    kms = jnp.sum(jnp.where(is_k, t * t, 0.0), axis=-1, keepdims=True) * (1.0 / IDX_DIM)
    kn = t * lax.rsqrt(kms + EPS) * kgain_ref[...]
    kr = kn * cos[:, :LANES] + _swap_halves(kn, LANES) * sin[:, :LANES]
    ki_ref[...] = kr[:, :IDX_DIM].astype(BF16)
    w_scale = (IDX_HEADS ** -0.5) * (IDX_DIM ** -0.5)
    wi_ref[...] = jnp.where(lane < IDX_HEADS, pltpu.roll(t, LANES - IDX_DIM, 1) * w_scale, 0.0)

    q1_ref[...] = (norm_rope(seg("q1"), 4) * Q_SCALE).astype(BF16)
    q2_ref[...] = (norm_rope(seg("q2"), 5) * Q_SCALE).astype(BF16)
    k1_ref[...] = norm_rope(seg("k1"), 6).astype(BF16)
    k2_ref[...] = norm_rope(seg("k2"), 7).astype(BF16)
    for half in range(2):
        vc_ref[:, half * HW:(half + 1) * HW] = seg("vc", half * HW, HW).astype(BF16)
    for gi, name in enumerate(("ga", "gb", "gc")):
        for part in range(D_MODEL // 512):
            g = seg(name, part * 512, 512)
            lo = gi * D_MODEL + part * 512
            sg_ref[:, lo:lo + 512] = (1.0 / (1.0 + jnp.exp(-g))).astype(BF16)


def _project(x2, norm_g, w_pad, cos_t, sin_t, gains, kgain, seq):
    n = x2.shape[0]
    tm = min(DENSE_TILE, seq)
    assert seq % tm == 0 and tm % MOBA_BLOCK == 0
    n_pos = seq // tm
    row = lambda w: pl.BlockSpec((tm, w), lambda i: (i, 0))
    const = lambda shape: pl.BlockSpec(shape, lambda i: (0,) * len(shape))
    out_widths = [HW] * 6 + [IDX_HEADS * IDX_DIM, IDX_DIM, LANES] + [HW] * 4 + [N_HEADS * C_VDIM, 3 * D_MODEL]
    out_dtypes = [BF16] * 8 + [F32] + [BF16] * 6
    out_shape = [jax.ShapeDtypeStruct((n, w), dt) for w, dt in zip(out_widths, out_dtypes)]
    out_shape.append(jax.ShapeDtypeStruct((n // MOBA_BLOCK, 1, HW), F32))
    out_specs = [row(w) for w in out_widths] + [pl.BlockSpec((tm // MOBA_BLOCK, 1, HW), lambda i: (i, 0, 0))]
    return pl.pallas_call(
        _proj_kernel,
        grid=(n // tm,),
        in_specs=[row(D_MODEL), const((1, D_MODEL)),
                  pl.BlockSpec((D_MODEL, D_IN_PAD), lambda i: (0, 0), pipeline_mode=pl.Buffered(1)),
                  pl.BlockSpec((tm, HW), lambda i: (i % n_pos, 0)),
                  pl.BlockSpec((tm, HW), lambda i: (i % n_pos, 0)),
                  const((8, HW)), const((1, LANES))],
        out_specs=out_specs,
        out_shape=out_shape,
        compiler_params=_params(1),
        name="proj",
    )(x2, norm_g, w_pad, cos_t, sin_t, gains, kgain)


def _online_update(parts, ms, acc_ref):
    ps, out = [], []
    for c, tiles in enumerate(parts):
        m_new = ms[c]
        for s, _, ok in tiles:
            smax = jnp.max(s, axis=0, keepdims=True)
            m_new = jnp.maximum(m_new, smax if ok is None else jnp.where(ok, smax, NEG_BIG))
        m_eff = jnp.maximum(m_new, M_FLOOR)
        out.append(m_new)
        probs = [jnp.exp2(s - (m_eff if ok is None else jnp.where(ok, m_eff, -NEG_BIG))).astype(BF16)
                 for s, _, ok in tiles]
        ps.append((jnp.exp2(ms[c] - m_new), probs))
    for c, (alpha, probs) in enumerate(ps):
        acc = alpha * acc_ref[c]
        for (_, vt, _), p in zip(parts[c], probs):
            acc = acc + _dot(vt, p)
        acc_ref[c] = acc
    return out


def _init_max(n_chains, tq):
    return tuple(jnp.full((1, tq), NEG_BIG, F32) for _ in range(n_chains))


def _with_ones_row(vt):
    row = lax.broadcasted_iota(jnp.int32, (V_PAD, vt.shape[1]), 0)
    return jnp.concatenate([vt, jnp.where(row == 0, 1.0, 0.0).astype(vt.dtype)], axis=0)


def _normalized(acc_ref, c, dv):
    acc = acc_ref[c]
    return acc[:dv] / acc[dv:dv + 1]


def _head_slice(h, width=HEAD_DIM):
    return slice(h * width, (h + 1) * width)


def _moba_kernel(qt_ref, k_ref, vt_ref, kmean_ref, o_ref, acc_ref, *, n_sel):
    i = pl.program_id(1)
    blk = MOBA_BLOCK
    nb = kmean_ref.shape[1]
    km = kmean_ref[0]
    brow = lax.broadcasted_iota(jnp.int32, (nb, blk), 0)
    browf = brow.astype(F32)
    causal = (lax.broadcasted_iota(jnp.int32, (blk, 1), 0)
              <= lax.broadcasted_iota(jnp.int32, (1, blk), 1))
    qts = [qt_ref[0, _head_slice(h), :] for h in range(N_HEADS)]
    sels = []
    for h in range(N_HEADS):
        km_hi, km_lo = _split_bf16(km[:, _head_slice(h)])
        gate = _dot(km_hi, qts[h]) + _dot(km_lo, qts[h])
        gate = jnp.where(brow < i, gate, -jnp.inf)
        sel = jnp.zeros((nb, blk), F32)
        for _ in range(n_sel):
            gm = jnp.max(gate, axis=0, keepdims=True)
            is_m = (gate == gm) & (gm > -jnp.inf)
            first = jnp.min(jnp.where(is_m, browf, float(nb)), axis=0, keepdims=True)
            pick = browf == first
            sel = jnp.where(pick, 1.0, sel)
            gate = jnp.where(pick, -jnp.inf, gate)
        sels.append(sel)
    acc_ref[...] = jnp.zeros(acc_ref.shape, F32)

    def tile(j, h, mask=None, seen=False):
        rows = pl.ds(pl.multiple_of(j * blk, blk), blk)
        s = _dot(k_ref[rows, _head_slice(h)], qts[h])
        if mask is not None:
            s = jnp.where(mask, s, NEG_BIG)
        ok = jnp.sum(jnp.where(brow == j, sels[h], 0.0), axis=0, keepdims=True) > 0.0 if seen else None
        return s, _with_ones_row(vt_ref[j, _head_slice(h), :]), ok

    def pair(j2, ms):
        parts = [[tile(2 * j2, h, seen=True), tile(2 * j2 + 1, h, seen=True)] for h in range(N_HEADS)]
        return tuple(_online_update(parts, ms, acc_ref))

    ms = lax.fori_loop(0, i // 2, pair, _init_max(N_HEADS, blk))

    @pl.when(i % 2 == 1)
    def _():
        _online_update([[tile(i - 1, h, seen=True), tile(i, h, mask=causal)] for h in range(N_HEADS)],
                       ms, acc_ref)

    @pl.when(i % 2 == 0)
    def _():
        _online_update([[tile(i, h, mask=causal)] for h in range(N_HEADS)], ms, acc_ref)

    for h in range(N_HEADS):
        o_ref[0, _head_slice(h), :] = _normalized(acc_ref, h, HEAD_DIM).astype(o_ref.dtype)


def _moba(qt, k, vt, kmean, batch, seq):
    blk = MOBA_BLOCK
    nb = seq // blk
    n_sel = min(MOBA_TOPK, nb - 1)
    tspec = pl.BlockSpec((1, HW, blk), lambda b, i: (b, 0, i))
    return pl.pallas_call(
        functools.partial(_moba_kernel, n_sel=n_sel),
        grid=(batch, nb),
        in_specs=[tspec,
                  pl.BlockSpec((seq, HW), lambda b, i: (b, 0)),
                  pl.BlockSpec((nb, vt.shape[1], blk), lambda b, i: (b, 0, 0)),
                  pl.BlockSpec((1, nb, HW), lambda b, i: (b, 0, 0))],
        out_specs=tspec,
        out_shape=jax.ShapeDtypeStruct((batch, HW, seq), BF16),
        scratch_shapes=[pltpu.VMEM((N_HEADS, HEAD_DIM + V_PAD, blk), F32)],
        compiler_params=_params(2),
        name="moba",
    )(qt, k, vt, kmean)


def _bit_planes(words):
    a = list(words)
    assert len(a) == 32
    mask, j = 0x0000FFFF, 16
    while j:
        k = 0
        while k < 32:
            t = (a[k] ^ (a[k + j] >> j)) & mask
            a[k] = a[k] ^ t
            a[k + j] = a[k + j] ^ (t << j)
            k = (k + j + 1) & ~j
        j >>= 1
        mask = (mask ^ (mask << j)) & 0xFFFFFFFF
    return a


def _dsa_kernel(qit_ref, wi_ref, ki_ref, qt_ref, k_ref, vt_ref, o_ref, plane_ref, sel_ref, acc_ref, *, n_keep):
    i = pl.program_id(1)
    blk = ROW_TILE
    n_chunk = i + 1
    n_planes, n_slots, sub = plane_ref.shape[:3]
    v_bits, s_bits = (n_planes - 1).bit_length(), (sub - 1).bit_length()
    idx_bits = v_bits + s_bits + (n_slots - 1).bit_length()

    @pl.when(i == 0)
    def _():
        plane_ref[:, 1:] = jnp.zeros((n_planes, n_slots - 1) + plane_ref.shape[2:], jnp.int32)

    w_t = wi_ref[...].T
    qpos = i * blk + lax.broadcasted_iota(jnp.int32, (1, blk), 1)
    krow = lax.broadcasted_iota(jnp.int32, (blk, 1), 0)

    def score_chunk(c, carry):
        kc = ki_ref[pl.ds(pl.multiple_of(c * blk, blk), blk), :]
        lgs = [_dot(kc, qit_ref[0, _head_slice(h, IDX_DIM), :]) for h in range(IDX_HEADS)]
        sc = jnp.zeros((blk, blk), F32)
        for h in range(IDX_HEADS):
            sc = sc + w_t[h:h + 1, :] * jnp.maximum(lgs[h], 0.0)
        sc = sc + 0.0
        bits = pltpu.bitcast(sc, jnp.int32)
        key = jnp.where(bits < 0, bits ^ 0x7FFFFFFF, bits)
        key = jnp.where(c * blk + krow <= qpos, key, INT_MIN)
        words = (key ^ INT_MIN).reshape(n_planes, sub, blk)
        planes = _bit_planes([words[v] for v in range(n_planes)])
        for b in range(n_planes):
            plane_ref[b, c] = planes[n_planes - 1 - b]
        return carry

    lax.fori_loop(0, n_chunk, score_chunk, 0)

    keep = float(n_keep)

    def step(plane, eq, gt):
        cand = gt | (eq & plane)
        per_row = jnp.sum(lax.population_count(cand), axis=0)
        take = jnp.sum(per_row.astype(F32), axis=0, keepdims=True) >= keep
        return eq & jnp.where(take, plane, ~plane), jnp.where(take, gt, cand)

    def index_plane(t, shape):
        if t < s_bits:
            s = lax.broadcasted_iota(jnp.int32, shape, 1)
            return jnp.where(((s >> t) & 1) == 0, -1, 0)
        if t < s_bits + v_bits:
            word = sum(1 << j for j in range(n_planes) if (((n_planes - 1 - j) >> (t - s_bits)) & 1) == 0)
            return jnp.full(shape, word - (1 << 32) if word >= (1 << 31) else word, jnp.int32)
        slot = lax.broadcasted_iota(jnp.int32, shape, 0)
        return jnp.where(((slot >> (t - s_bits - v_bits)) & 1) == 0, -1, 0)

    def select(n_used, lo):
        shape = (n_used, sub, LANES)
        lanes = slice(lo, lo + LANES)

        def score_bit(t, carry):
            return step(plane_ref[n_planes - 1 - t, 0:n_used, :, lanes], *carry)

        eq, gt = lax.fori_loop(0, n_planes, score_bit,
                               (jnp.full(shape, -1, jnp.int32), jnp.zeros(shape, jnp.int32)))
        for t in reversed(range(idx_bits)):
            eq, gt = step(index_plane(t, shape), eq, gt)
        slot = lax.broadcasted_iota(jnp.int32, shape, 0)
        s = lax.broadcasted_iota(jnp.int32, shape, 1)
        qoff = lo + lax.broadcasted_iota(jnp.int32, shape, 2)
        reach = jnp.left_shift(-1, (n_planes - 1) - ((qoff - s) >> s_bits))
        valid = jnp.where(slot < i, -1, jnp.where((slot == i) & (qoff >= s), reach, 0))
        return (eq | gt) & valid

    half_slots = n_slots // 2
    for n_used, wanted in ((half_slots, n_chunk <= half_slots), (n_slots, n_chunk > half_slots)):
        @pl.when(wanted)
        def _():
            for lo in range(0, blk, LANES):
                sel_ref[0:n_used, :, lo:lo + LANES] = select(n_used, lo)
                if n_used < n_slots:
                    sel_ref[n_used:, :, lo:lo + LANES] = jnp.zeros((n_slots - n_used, sub, LANES), jnp.int32)

    qts = [qt_ref[0, _head_slice(h), :] for h in range(N_HEADS)]
    acc_ref[...] = jnp.zeros(acc_ref.shape, F32)

    def pair(c2, ms):
        parts = [[] for _ in range(N_HEADS)]
        for c in (2 * c2, 2 * c2 + 1):
            rows = pl.ds(pl.multiple_of(c * blk, blk), blk)
            w = sel_ref[c]
            allowed = jnp.concatenate([jnp.left_shift(w, v) for v in range(n_planes)], axis=0) < 0
            for h in range(N_HEADS):
                s = jnp.where(allowed, _dot(k_ref[rows, _head_slice(h)], qts[h]), NEG_BIG)
                parts[h].append((s, _with_ones_row(vt_ref[c, _head_slice(h), :]), None))
        return tuple(_online_update(parts, ms, acc_ref))

    lax.fori_loop(0, (n_chunk + 1) // 2, pair, _init_max(N_HEADS, blk))
    for h in range(N_HEADS):
        o_ref[0, _head_slice(h), :] = _normalized(acc_ref, h, HEAD_DIM).astype(o_ref.dtype)


def _dsa(qit, wi, ki, qt, k, vt, batch, seq):
    blk = ROW_TILE
    nq = seq // blk
    n_keep = min(DSA_TOPK_MAX, seq // 4)
    n_planes = 32
    tspec = lambda w: pl.BlockSpec((1, w, blk), lambda b, i: (b, 0, i))
    full = lambda w: pl.BlockSpec((seq, w), lambda b, i: (b, 0))
    return pl.pallas_call(
        functools.partial(_dsa_kernel, n_keep=n_keep),
        grid=(batch, nq),
        in_specs=[tspec(IDX_HEADS * IDX_DIM), pl.BlockSpec((blk, LANES), lambda b, i: (b * nq + i, 0)),
                  full(IDX_DIM), tspec(HW), full(HW),
                  pl.BlockSpec((nq, vt.shape[1], blk), lambda b, i: (b, 0, 0))],
        out_specs=tspec(HW),
        out_shape=jax.ShapeDtypeStruct((batch, HW, seq), BF16),
        scratch_shapes=[pltpu.VMEM((n_planes, nq, blk // n_planes, blk), jnp.int32),
                        pltpu.VMEM((nq, blk // n_planes, blk), jnp.int32),
                        pltpu.VMEM((N_HEADS, HEAD_DIM + V_PAD, blk), F32)],
        compiler_params=_params(2),
        name="dsa",
    )(qit, wi, ki, qt, k, vt)


DIFF_GROUP = 4


def _diff_kernel(q1t_ref, q2t_ref, k1_ref, k2_ref, vt_ref, dl_ref, gain_ref, o_ref, acc_ref, *, lam_init):
    i = pl.program_id(1)
    blk = ROW_TILE
    dl = dl_ref[...]
    lam = (jnp.exp(jnp.sum(dl[0:1] * dl[1:2], axis=-1, keepdims=True))
           - jnp.exp(jnp.sum(dl[2:3] * dl[3:4], axis=-1, keepdims=True)) + lam_init)
    causal = (lax.broadcasted_iota(jnp.int32, (blk, 1), 0)
              <= lax.broadcasted_iota(jnp.int32, (1, blk), 1))
    maps = ((q1t_ref, k1_ref), (q2t_ref, k2_ref))

    for h0 in range(0, N_HEADS, DIFF_GROUP):
        chains = [(h, mp) for h in range(h0, h0 + DIFF_GROUP) for mp in range(2)]
        qts = [maps[mp][0][0, _head_slice(h), :] for h, mp in chains]
        acc_ref[...] = jnp.zeros(acc_ref.shape, F32)

        def tile(j, c, mask=None):
            h, mp = chains[c]
            rows = pl.ds(pl.multiple_of(j * blk, blk), blk)
            s = _dot(maps[mp][1][rows, _head_slice(h)], qts[c])
            if mask is not None:
                s = jnp.where(mask, s, NEG_BIG)
            return s, _with_ones_row(vt_ref[j, _head_slice(h, C_VDIM), :]), None

        def pair(j2, ms):
            parts = [[tile(2 * j2, c), tile(2 * j2 + 1, c)] for c in range(len(chains))]
            return tuple(_online_update(parts, ms, acc_ref))

        ms = lax.fori_loop(0, i // 2, pair, _init_max(len(chains), blk))

        @pl.when(i % 2 == 1)
        def _():
            _online_update([[tile(i - 1, c), tile(i, c, causal)] for c in range(len(chains))], ms, acc_ref)

        @pl.when(i % 2 == 0)
        def _():
            _online_update([[tile(i, c, causal)] for c in range(len(chains))], ms, acc_ref)

        for g in range(DIFF_GROUP):
            h = h0 + g
            o = _normalized(acc_ref, 2 * g, C_VDIM) - lam * _normalized(acc_ref, 2 * g + 1, C_VDIM)
            ms = jnp.mean(o * o, axis=0, keepdims=True)
            o = o * lax.rsqrt(ms + EPS) * gain_ref[...] * (1.0 - lam_init)
            o_ref[0, _head_slice(h, C_VDIM), :] = o.astype(o_ref.dtype)


def _diff(q1t, q2t, k1, k2, vt, dl, gain, lam_init, batch, seq):
    blk = ROW_TILE
    nq = seq // blk
    tspec = lambda w: pl.BlockSpec((1, w, blk), lambda b, i: (b, 0, i))
    kspec = pl.BlockSpec((seq, HW), lambda b, i: (b, 0))
    vw = N_HEADS * C_VDIM
    return pl.pallas_call(
        functools.partial(_diff_kernel, lam_init=lam_init),
        grid=(batch, nq),
        in_specs=[tspec(HW), tspec(HW), kspec, kspec,
                  pl.BlockSpec((nq, vt.shape[1], blk), lambda b, i: (b, 0, 0)),
                  pl.BlockSpec((4, HEAD_DIM), lambda b, i: (0, 0)),
                  pl.BlockSpec((C_VDIM, 1), lambda b, i: (0, 0))],
        out_specs=tspec(vw),
        out_shape=jax.ShapeDtypeStruct((batch, vw, seq), BF16),
        scratch_shapes=[pltpu.VMEM((2 * DIFF_GROUP, C_VDIM + V_PAD, blk), F32)],
        compiler_params=_params(2),
        name="diff",
    )(q1t, q2t, k1, k2, vt, dl, gain)


def _merge_kernel(x_ref, oa_ref, ob_ref, oc_ref, sg_ref, wa_ref, wb_ref, wc_ref, wo_ref, g_ref,
                  wr_hi_ref, wr_lo_ref, br_ref, x1_ref, hn_ref, route_ref):
    merged = (sg_ref[:, 0:D_MODEL].astype(F32) * _dot(oa_ref[...], wa_ref[...])
              + sg_ref[:, D_MODEL:2 * D_MODEL].astype(F32) * _dot(ob_ref[...], wb_ref[...])
              + sg_ref[:, 2 * D_MODEL:3 * D_MODEL].astype(F32) * _dot(oc_ref[...], wc_ref[...]))
    x1 = x_ref[...] + _dot(merged.astype(BF16), wo_ref[...])
    x1_ref[...] = x1
    ms = jnp.mean(x1 * x1, axis=-1, keepdims=True)
    hn = x1 * lax.rsqrt(ms + EPS) * g_ref[...]
    hn_ref[...] = hn

    hi, lo = _split_bf16(hn)
    lg = (_dot(hi, wr_hi_ref[...]) + _dot(lo, wr_hi_ref[...]) + _dot(hi, wr_lo_ref[...])
          + br_ref[...])
    lane = lax.broadcasted_iota(jnp.int32, lg.shape, 1)
    lanef = lane.astype(F32)
    far = float(LANES)
    is_g = (lane >= N_EXPERTS) & (lane < N_EXPERTS + N_GROUPS)
    gl = jnp.where(is_g, lg, -jnp.inf)
    gmax = jnp.max(gl, axis=-1, keepdims=True)
    gidx = jnp.min(jnp.where(gl == gmax, lanef, far), axis=-1, keepdims=True) - float(N_EXPERTS)
    g_w = 1.0 / jnp.sum(jnp.where(is_g, jnp.exp(gl - gmax), 0.0), axis=-1, keepdims=True)
    in_group = (lane < N_EXPERTS) & ((lane // EXPERTS_PER_GROUP).astype(F32) == gidx)
    el = jnp.where(in_group, lg, -jnp.inf)
    e1 = jnp.max(el, axis=-1, keepdims=True)
    i1 = jnp.min(jnp.where(el == e1, lanef, far), axis=-1, keepdims=True)
    el2 = jnp.where(lanef == i1, -jnp.inf, el)
    e2 = jnp.max(el2, axis=-1, keepdims=True)
    i2 = jnp.min(jnp.where(el2 == e2, lanef, far), axis=-1, keepdims=True)
    t = jnp.exp(e2 - e1)
    w1 = g_w / (1.0 + t)
    w2 = g_w * t / (1.0 + t)
    route_ref[...] = jnp.where(lane == 0, i1, jnp.where(lane == 1, i2, jnp.where(lane == 2, w1, jnp.where(lane == 3, w2, 0.0))))


def _merge(x2, oa, ob, oc, sg, wa, wb, wc, wo, norm_g, wr_hi, wr_lo, br):
    n = x2.shape[0]
    tm = min(DENSE_TILE, n)
    row = lambda w: pl.BlockSpec((tm, w), lambda i: (i, 0))
    const = lambda a: pl.BlockSpec(a.shape, lambda i: (0, 0))
    return pl.pallas_call(
        _merge_kernel,
        grid=(n // tm,),
        in_specs=[row(D_MODEL), row(HW), row(HW), row(N_HEADS * C_VDIM), row(3 * D_MODEL),
                  const(wa), const(wb), const(wc), const(wo), const(norm_g), const(wr_hi),
                  const(wr_lo), const(br)],
        out_specs=[row(D_MODEL), row(D_MODEL), row(LANES)],
        out_shape=[jax.ShapeDtypeStruct((n, D_MODEL), F32), jax.ShapeDtypeStruct((n, D_MODEL), F32),
                   jax.ShapeDtypeStruct((n, LANES), F32)],
        compiler_params=_params(1),
        name="merge",
    )(x2, oa, ob, oc, sg, wa, wb, wc, wo, norm_g, wr_hi, wr_lo, br)


MOE_BLOCK = 512
META_ROWS = 8
META_USED = 3 * LANES
META_END = 4 * LANES
META_PADDED = 5 * LANES


def _lane_prefix_sum(x):
    lane = lax.broadcasted_iota(jnp.int32, x.shape, 1)
    shift = 1
    while shift < LANES:
        x = x + jnp.where(lane >= shift, pltpu.roll(x, shift, 1), 0.0)
        shift *= 2
    return x


def _positions_kernel(route_ref, dest_ref, meta_ref, cnt_ref, base_ref):
    phase = pl.program_id(0)
    t = pl.program_id(1)
    tm = route_ref.shape[0]
    route = route_ref[...]
    lane = lax.broadcasted_iota(jnp.int32, route.shape, 1)
    lanef = lane.astype(F32)
    e1 = route[:, 0:1]
    e2 = route[:, 1:2]
    uses = jnp.where((lanef == e1) | (lanef == e2), 1.0, 0.0)
    tile_cnt = jnp.sum(uses, axis=0, keepdims=True)

    @pl.when((phase == 0) & (t == 0))
    def _():
        cnt_ref[...] = jnp.zeros(cnt_ref.shape, F32)

    @pl.when(phase == 0)
    def _():
        cnt_ref[...] += tile_cnt

    @pl.when((phase == 1) & (t == 0))
    def _():
        cnt = cnt_ref[...]
        padded = jnp.floor((cnt + (MOE_BLOCK - 1)) * (1.0 / MOE_BLOCK)) * MOE_BLOCK
        end = _lane_prefix_sum(padded)
        base_ref[...] = end - padded
        cnt_ref[...] = jnp.zeros(cnt_ref.shape, F32)
        lane1 = lax.broadcasted_iota(jnp.int32, (1, LANES), 1)
        row = lax.broadcasted_iota(jnp.int32, (META_ROWS, LANES), 0)
        col = lax.broadcasted_iota(jnp.int32, (META_ROWS, LANES), 1)
        first_row = ((row * LANES + col) * MOE_BLOCK).astype(F32)
        owner = jnp.zeros((META_ROWS, LANES), F32)
        for e in range(N_EXPERTS):
            end_e = jnp.sum(jnp.where(lane1 == e, end, 0.0), axis=-1, keepdims=True)
            owner = owner + jnp.where(end_e <= first_row, 1.0, 0.0)
        owner = jnp.minimum(owner, float(N_EXPERTS - 1))
        used = jnp.sum(jnp.where(lane1 == N_EXPERTS - 1, end, 0.0), axis=-1, keepdims=True) * (1.0 / MOE_BLOCK)
        meta = jnp.where(row == META_USED // LANES, used,
                         jnp.where(row == META_END // LANES, end,
                                   jnp.where(row == META_PADDED // LANES, padded, owner)))
        meta_ref[...] = meta.astype(jnp.int32)

    @pl.when(phase == 1)
    def _():
        before = (lax.broadcasted_iota(jnp.int32, (tm, tm), 1)
                  < lax.broadcasted_iota(jnp.int32, (tm, tm), 0)).astype(BF16)
        rank = _dot(before, uses.astype(BF16))
        pos = base_ref[...] + cnt_ref[...] + rank
        d1 = jnp.sum(jnp.where(lanef == e1, pos, 0.0), axis=-1, keepdims=True)
        d2 = jnp.sum(jnp.where(lanef == e2, pos, 0.0), axis=-1, keepdims=True)
        dest_ref[...] = jnp.where(lane == 0, d1, jnp.where(lane == 1, d2, 0.0)).astype(jnp.int32)
        cnt_ref[...] += tile_cnt


def _positions(route):
    n = route.shape[0]
    tm = min(4 * ROW_TILE, n)
    return pl.pallas_call(
        _positions_kernel,
        grid=(2, n // tm),
        in_specs=[pl.BlockSpec((tm, LANES), lambda p, t: (t, 0))],
        out_specs=[pl.BlockSpec((tm, LANES), lambda p, t: (t * p, 0)),
                   pl.BlockSpec((META_ROWS, LANES), lambda p, t: (0, 0))],
        out_shape=[jax.ShapeDtypeStruct((n, LANES), jnp.int32),
                   jax.ShapeDtypeStruct((META_ROWS, LANES), jnp.int32)],
        scratch_shapes=[pltpu.VMEM((1, LANES), F32), pltpu.VMEM((1, LANES), F32)],
        compiler_params=_params(2),
        name="moe_positions",
    )(route)


def _row_copy(src_ref, src_row, dst_ref, dst_row, sem):
    return pltpu.make_async_copy(src_ref.at[pl.ds(src_row, 1), :], dst_ref.at[pl.ds(dst_row, 1), :], sem)


def _dispatch_kernel(meta_ref, dest_ref, hn_ref, xs_ref, zero_ref, sem):
    tm = hn_ref.shape[0]

    @pl.when(pl.program_id(0) == 0)
    def _():
        zero_ref[...] = jnp.zeros(zero_ref.shape, F32)

        def fill(e):
            end = pl.multiple_of(meta_ref[META_END + e], MOE_BLOCK)
            return pltpu.make_async_copy(zero_ref, xs_ref.at[pl.ds(end - MOE_BLOCK, MOE_BLOCK), :], sem)

        for e in range(N_EXPERTS):
            @pl.when(meta_ref[META_PADDED + e] > 0)
            def _():
                fill(e).start()
        for e in range(N_EXPERTS):
            @pl.when(meta_ref[META_PADDED + e] > 0)
            def _():
                fill(e).wait()

        def spare(b):
            return pltpu.make_async_copy(
                zero_ref, xs_ref.at[pl.ds(pl.multiple_of(b * MOE_BLOCK, MOE_BLOCK), MOE_BLOCK), :], sem)

        n_blocks = xs_ref.shape[0] // MOE_BLOCK
        lax.fori_loop(meta_ref[META_USED], n_blocks, lambda b, c: (spare(b).start(), c)[1], 0)
        lax.fori_loop(meta_ref[META_USED], n_blocks, lambda b, c: (spare(b).wait(), c)[1], 0)

    def copies(r):
        return (_row_copy(hn_ref, r, xs_ref, dest_ref[0, 0, 2 * r], sem),
                _row_copy(hn_ref, r, xs_ref, dest_ref[0, 0, 2 * r + 1], sem))

    def start(r, carry):
        for cp in copies(r):
            cp.start()
        return carry

    lax.fori_loop(0, tm, start, 0, unroll=8)
    for _ in range(2):
        pltpu.make_async_copy(hn_ref, xs_ref.at[pl.ds(0, tm), :], sem).wait()


def _dispatch(meta, dest3, hn, n_rows):
    n = hn.shape[0]
    tm = dest3.shape[2] // 2
    return pl.pallas_call(
        _dispatch_kernel,
        grid_spec=pltpu.PrefetchScalarGridSpec(
            num_scalar_prefetch=1,
            grid=(n // tm,),
            in_specs=[pl.BlockSpec((1, 1, 2 * tm), lambda t, m: (t, 0, 0), memory_space=pltpu.SMEM),
                      pl.BlockSpec((tm, D_MODEL), lambda t, m: (t, 0))],
            out_specs=pl.BlockSpec(memory_space=pl.ANY),
            scratch_shapes=[pltpu.VMEM((MOE_BLOCK, D_MODEL), F32), pltpu.SemaphoreType.DMA(())]),
        out_shape=jax.ShapeDtypeStruct((n_rows, D_MODEL), F32),
        compiler_params=_params(1),
        name="moe_dispatch",
    )(meta, dest3, hn)


def _expert_kernel(meta_ref, xs_ref, wg_ref, wu_ref, wd_ref, y_ref, wg_bf, wu_bf, wd_bf):
    b = pl.program_id(0)
    holds_rows = b < meta_ref[META_USED]
    new_expert = (b == 0) | (meta_ref[b] != meta_ref[jnp.maximum(b - 1, 0)])

    @pl.when(holds_rows & new_expert)
    def _():
        wg_bf[...] = wg_ref[0].astype(BF16)
        wu_bf[...] = wu_ref[0].astype(BF16)
        wd_bf[...] = wd_ref[0].astype(BF16)

    @pl.when(holds_rows)
    def _():
        x = xs_ref[...].astype(BF16)
        g = _dot(x, wg_bf[...])
        u = _dot(x, wu_bf[...])
        hid = g * (1.0 / (1.0 + jnp.exp(-g))) * u
        y_ref[...] = _dot(hid.astype(BF16), wd_bf[...])

    @pl.when(jnp.logical_not(holds_rows))
    def _():
        y_ref[...] = jnp.zeros(y_ref.shape, F32)


def _experts(meta, xs, wg, wu, wd, layer):
    n_blocks = xs.shape[0] // MOE_BLOCK
    rows = pl.BlockSpec((MOE_BLOCK, D_MODEL), lambda b, m: (b, 0))
    weight = lambda shape: pl.BlockSpec(
        (None, 1) + shape, lambda b, m: (layer, m[jnp.minimum(b, m[META_USED] - 1)], 0, 0))
    return pl.pallas_call(
        _expert_kernel,
        grid_spec=pltpu.PrefetchScalarGridSpec(
            num_scalar_prefetch=1,
            grid=(n_blocks,),
            in_specs=[rows, weight((D_MODEL, D_EXPERT)), weight((D_MODEL, D_EXPERT)),
                      weight((D_EXPERT, D_MODEL))],
            out_specs=rows,
            scratch_shapes=[pltpu.VMEM((D_MODEL, D_EXPERT), BF16), pltpu.VMEM((D_MODEL, D_EXPERT), BF16),
                            pltpu.VMEM((D_EXPERT, D_MODEL), BF16)]),
        out_shape=jax.ShapeDtypeStruct(xs.shape, F32),
        compiler_params=_params(1),
        name="moe_experts",
    )(meta, xs, wg, wu, wd)


def _combine_kernel(dest_ref, next_dest_ref, x1_ref, route_ref, y_ref, o_ref, buf_ref, sems):
    t = pl.program_id(0)
    tm = x1_ref.shape[0]

    def gather(d_ref, parity):
        def start(g, carry):
            first = pl.multiple_of(g * 8, 8)
            for u in range(8):
                for k in range(2):
                    _row_copy(y_ref, d_ref[0, 0, 2 * (g * 8 + u) + k], buf_ref.at[parity, k], first + u,
                              sems.at[parity]).start()
            return carry
        lax.fori_loop(0, tm // 8, start, 0)

    @pl.when(t == 0)
    def _():
        gather(dest_ref, 0)

    @pl.when(t + 1 < pl.num_programs(0))
    def _():
        gather(next_dest_ref, (t + 1) % 2)

    for k in range(2):
        pltpu.make_async_copy(y_ref.at[pl.ds(0, tm), :], buf_ref.at[t % 2, k], sems.at[t % 2]).wait()
    route = route_ref[...]
    o_ref[...] = x1_ref[...] + route[:, 2:3] * buf_ref[t % 2, 0] + route[:, 3:4] * buf_ref[t % 2, 1]


def _combine(dest3, x1, route, y):
    n = x1.shape[0]
    tm = dest3.shape[2] // 2
    row = lambda w: pl.BlockSpec((tm, w), lambda t: (t, 0))
    n_tiles = n // tm
    slots = lambda index: pl.BlockSpec((1, 1, 2 * tm), index, memory_space=pltpu.SMEM)
    return pl.pallas_call(
        _combine_kernel,
        grid=(n_tiles,),
        in_specs=[slots(lambda t: (t, 0, 0)), slots(lambda t: (jnp.minimum(t + 1, n_tiles - 1), 0, 0)),
                  row(D_MODEL), row(LANES), pl.BlockSpec(memory_space=pl.ANY)],
        out_specs=row(D_MODEL),
        out_shape=jax.ShapeDtypeStruct((n, D_MODEL), F32),
        scratch_shapes=[pltpu.VMEM((2, 2, tm, D_MODEL), F32), pltpu.SemaphoreType.DMA((2,))],
        compiler_params=_params(1),
        name="moe_combine",
    )(dest3, dest3, x1, route, y)


def _moe(x1, hn, route, wg, wu, wd, layer):
    n = x1.shape[0]
    tm = min(ROW_TILE, n)
    n_blocks = -(-(2 * n + N_EXPERTS * (MOE_BLOCK - 1)) // MOE_BLOCK)
    assert n_blocks <= META_USED
    dest, meta = _positions(route)
    meta = meta.reshape(-1)
    dest3 = dest[:, :2].reshape(n // tm, 1, 2 * tm)
    xs = _dispatch(meta, dest3, hn, n_blocks * MOE_BLOCK)
    y = _experts(meta, xs, wg, wu, wd, layer)
    return _combine(dest3, x1, route, y)


def _rope_tables(seq):
    inv_freq = 1.0 / (ROPE_THETA ** (jnp.arange(0, HEAD_DIM, 2, dtype=F32) / HEAD_DIM))
    ang = jnp.arange(seq, dtype=F32)[:, None] * inv_freq[None, :]
    cos, sin = jnp.cos(ang), jnp.sin(ang)
    cos_t = jnp.tile(jnp.concatenate([cos, cos], axis=-1), (1, N_HEADS))
    sin_t = jnp.tile(jnp.concatenate([-sin, sin], axis=-1), (1, N_HEADS))
    return cos_t, sin_t


def _to_t(v, batch, n_chunk):
    feat = v.shape[1]
    return (v.reshape(batch, n_chunk, ROW_TILE, feat).transpose(0, 1, 3, 2)
            .reshape(batch * n_chunk, feat, ROW_TILE))


def _from_t(o_t):
    b, feat, seq = o_t.shape
    return o_t.transpose(0, 2, 1).reshape(b * seq, feat)


def kernel(x, norm_attn, w_in, qk_gain, idx_k_gain, diff_lambda, diff_subln_gain, w_proj_a, w_proj_b, w_proj_c, w_out, norm_ffn, w_group, b_group, w_router, b_router, w_e_gate, w_e_up, w_e_down):
    batch, seq, d = x.shape
    assert d == D_MODEL and seq % (2 * ROW_TILE) == 0 and ROW_TILE == MOBA_BLOCK
    n = batch * seq
    nq = seq // ROW_TILE
    depth = w_in.shape[0]
    cos_t, sin_t = _rope_tables(seq)
    x2 = x.reshape(n, d)
    for l in range(depth):
        w_pad = jnp.concatenate(
            [w_in[l][:, :KW_SRC], jnp.zeros((d, LANES - IDX_DIM - IDX_HEADS), F32), w_in[l][:, KW_SRC:]],
            axis=1).astype(BF16)
        gains = jnp.tile(qk_gain[l][jnp.array([0, 1, 2, 3, 4, 4, 5, 5])], (1, N_HEADS))
        kgain = jnp.pad(idx_k_gain[l], (0, LANES - IDX_DIM))[None, :]
        (qa, ka, va, qb, kb, vb, qi, ki, wi, q1, q2, k1, k2, vc, sg, kmean) = _project(
            x2, norm_attn[l][None, :], w_pad, cos_t, sin_t, gains, kgain, seq)

        feat_major = lambda t: t.reshape(batch, seq, t.shape[1]).transpose(0, 2, 1)
        o_a = _from_t(_moba(feat_major(qa), ka, _to_t(va, batch, nq), kmean.reshape(batch, nq, HW),
                            batch, seq))
        o_b = _from_t(_dsa(feat_major(qi), wi, ki, feat_major(qb), kb, _to_t(vb, batch, nq), batch, seq))
        lam_init = 0.8 - 0.6 * math.exp(-0.3 * l)
        o_c = _from_t(_diff(feat_major(q1), feat_major(q2), k1, k2, _to_t(vc, batch, nq), diff_lambda[l],
                            diff_subln_gain[l][:, None], lam_init, batch, seq))

        w_r = jnp.concatenate([w_router[l], w_group[l],
                               jnp.zeros((d, LANES - N_EXPERTS - N_GROUPS), F32)], axis=1)
        wr_hi = w_r.astype(BF16)
        wr_lo = (w_r - wr_hi.astype(F32)).astype(BF16)
        b_r = jnp.concatenate([b_router[l], b_group[l],
                               jnp.zeros((LANES - N_EXPERTS - N_GROUPS,), F32)])[None, :]
        x1, hn, route = _merge(x2, o_a, o_b, o_c, sg, w_proj_a[l].astype(BF16), w_proj_b[l].astype(BF16),
                            w_proj_c[l].astype(BF16), w_out[l].astype(BF16), norm_ffn[l][None, :],
                            wr_hi, wr_lo, b_r)
        x2 = _moe(x1, hn, route, w_e_gate, w_e_up, w_e_down, l)
    return x2.reshape(batch, seq, d)
```

```python
import functools
import math

import jax
import jax.numpy as jnp
from jax import lax
from jax.experimental import pallas as pl
from jax.experimental.pallas import tpu as pltpu

F32 = jnp.float32
BF16 = jnp.bfloat16

D_MODEL = 1024
HEAD_DIM = 64
ROPE_THETA = 10000.0
EPS = 1e-6
N_HEADS = 4
MOBA_BLOCK = 256
MOBA_TOPK = 3
IDX_HEADS = 8
IDX_DIM = 64
DSA_TOPK_MAX = 256
C_VDIM = 2 * HEAD_DIM
N_GROUPS = 4
EXPERTS_PER_GROUP = 8
N_EXPERTS = N_GROUPS * EXPERTS_PER_GROUP
D_EXPERT = 512

HW = N_HEADS * HEAD_DIM
LANES = 128
ROW_TILE = 256
DENSE_TILE = 512
VMEM_LIMIT = 56 * 1024 * 1024

_SEG = {}
_off = 0
for _name, _w in (("qa", HW), ("ka", HW), ("va", HW), ("qb", HW), ("kb", HW), ("vb", HW),
                  ("qi", IDX_HEADS * IDX_DIM), ("kw", LANES), ("q1", HW), ("q2", HW), ("k1", HW),
                  ("k2", HW), ("vc", N_HEADS * C_VDIM), ("ga", D_MODEL), ("gb", D_MODEL),
                  ("gc", D_MODEL)):
    _SEG[_name] = (_off, _w)
    _off += _w
D_IN_PAD = _off
KW_SRC = 6 * HW + IDX_HEADS * IDX_DIM + IDX_DIM + IDX_HEADS

NEG_BIG = -1e30
M_FLOOR = -1e20
INT_MIN = -(2 ** 31)
LOG2E = math.log2(math.e)
Q_SCALE = HEAD_DIM ** -0.5 * LOG2E
V_PAD = 16
NT_DIMS = (((1,), (1,)), ((), ()))


def _params(n_axes):
    return pltpu.CompilerParams(dimension_semantics=("arbitrary",) * n_axes,
                                vmem_limit_bytes=VMEM_LIMIT)


def _dot(a, b):
    return jnp.dot(a, b, preferred_element_type=F32)


def _dot_nt(a, b):
    return lax.dot_general(a, b, NT_DIMS, preferred_element_type=F32)


def _split_bf16(a):
    hi = a.astype(BF16)
    return hi, (a - hi.astype(F32)).astype(BF16)


def _swap_halves(y, width):
    lane = lax.broadcasted_iota(jnp.int32, y.shape, 1)
    first = (lane % HEAD_DIM) < (HEAD_DIM // 2)
    return jnp.where(first, pltpu.roll(y, width - HEAD_DIM // 2, 1), pltpu.roll(y, HEAD_DIM // 2, 1))


def _proj_kernel(x_ref, g_ref, w_ref, cos_ref, sin_ref, gain_ref, kgain_ref,
                 qa_ref, ka_ref, va_ref, qb_ref, kb_ref, vb_ref, qi_ref, ki_ref, wi_ref,
                 q1_ref, q2_ref, k1_ref, k2_ref, vc_ref, sg_ref, kmean_ref):
    x = x_ref[...]
    ms = jnp.mean(x * x, axis=-1, keepdims=True)
    h = (x * lax.rsqrt(ms + EPS) * g_ref[...]).astype(BF16)
    cos = cos_ref[...]
    sin = sin_ref[...]
    r = lax.broadcasted_iota(jnp.int32, (HW, HW), 0) // HEAD_DIM
    c = lax.broadcasted_iota(jnp.int32, (HW, HW), 1) // HEAD_DIM
    head_ones = (r == c).astype(BF16)

    def seg(name, lo=0, width=None):
        off, w = _SEG[name]
        width = w if width is None else width
        return _dot(h, w_ref[:, off + lo:off + lo + width])

    def rope(y):
        return y * cos + _swap_halves(y, HW) * sin

    def norm_rope(t, gain_row):
        hi, lo = _split_bf16(t * t)
        ss = _dot(hi, head_ones) + _dot(lo, head_ones)
        yn = t * lax.rsqrt(ss * (1.0 / HEAD_DIM) + EPS) * gain_ref[gain_row:gain_row + 1, :]
        return rope(yn)

    qa_ref[...] = (norm_rope(seg("qa"), 0) * Q_SCALE).astype(BF16)
    ka = norm_rope(seg("ka"), 1)
    ka_ref[...] = ka.astype(BF16)
    for blk in range(ka.shape[0] // MOBA_BLOCK):
        kmean_ref[blk] = jnp.mean(ka[blk * MOBA_BLOCK:(blk + 1) * MOBA_BLOCK], axis=0, keepdims=True)
    va_ref[...] = seg("va").astype(BF16)
    qb_ref[...] = (norm_rope(seg("qb"), 2) * Q_SCALE).astype(BF16)
    kb_ref[...] = norm_rope(seg("kb"), 3).astype(BF16)
    vb_ref[...] = seg("vb").astype(BF16)
    for half in range(2):
        qi_ref[:, half * HW:(half + 1) * HW] = rope(seg("qi", half * HW, HW)).astype(BF16)

    t = seg("kw")
    lane = lax.broadcasted_iota(jnp.int32, t.shape, 1)
    is_k = lane < IDX_DIM
    kms = jnp.sum(jnp.where(is_k, t * t, 0.0), axis=-1, keepdims=True) * (1.0 / IDX_DIM)
    kn = t * lax.rsqrt(kms + EPS) * kgain_ref[...]
    kr = kn * cos[:, :LANES] + _swap_halves(kn, LANES) * sin[:, :LANES]
    ki_ref[...] = kr[:, :IDX_DIM].astype(BF16)
    w_scale = (IDX_HEADS ** -0.5) * (IDX_DIM ** -0.5)
    wi_ref[...] = jnp.where(lane < IDX_HEADS, pltpu.roll(t, LANES - IDX_DIM, 1) * w_scale, 0.0)

    q1_ref[...] = (norm_rope(seg("q1"), 4) * Q_SCALE).astype(BF16)
    q2_ref[...] = (norm_rope(seg("q2"), 5) * Q_SCALE).astype(BF16)
    k1_ref[...] = norm_rope(seg("k1"), 6).astype(BF16)
    k2_ref[...] = norm_rope(seg("k2"), 7).astype(BF16)
    for half in range(2):
        vc_ref[:, half * HW:(half + 1) * HW] = seg("vc", half * HW, HW).astype(BF16)
    for gi, name in enumerate(("ga", "gb", "gc")):
        for part in range(D_MODEL // 512):
            g = seg(name, part * 512, 512)
            lo = gi * D_MODEL + part * 512
            sg_ref[:, lo:lo + 512] = (1.0 / (1.0 + jnp.exp(-g))).astype(BF16)


def _project(x2, norm_g, w_pad, layer, cos_t, sin_t, gains, kgain, seq):
    n = x2.shape[0]
    tm = min(DENSE_TILE, seq)
    assert seq % tm == 0 and tm % MOBA_BLOCK == 0
    n_pos = seq // tm
    row = lambda w: pl.BlockSpec((tm, w), lambda i: (i, 0))
    const = lambda shape: pl.BlockSpec(shape, lambda i: (0,) * len(shape))
    out_widths = [HW] * 6 + [IDX_HEADS * IDX_DIM, IDX_DIM, LANES] + [HW] * 4 + [N_HEADS * C_VDIM, 3 * D_MODEL]
    out_dtypes = [BF16] * 8 + [F32] + [BF16] * 6
    out_shape = [jax.ShapeDtypeStruct((n, w), dt) for w, dt in zip(out_widths, out_dtypes)]
    out_shape.append(jax.ShapeDtypeStruct((n // MOBA_BLOCK, 1, HW), F32))
    out_specs = [row(w) for w in out_widths] + [pl.BlockSpec((tm // MOBA_BLOCK, 1, HW), lambda i: (i, 0, 0))]
    return pl.pallas_call(
        _proj_kernel,
        grid=(n // tm,),
        in_specs=[row(D_MODEL), const((1, D_MODEL)),
                  pl.BlockSpec((None, D_MODEL, D_IN_PAD), lambda i: (layer, 0, 0), pipeline_mode=pl.Buffered(1)),
                  pl.BlockSpec((tm, HW), lambda i: (i % n_pos, 0)),
                  pl.BlockSpec((tm, HW), lambda i: (i % n_pos, 0)),
                  const((8, HW)), const((1, LANES))],
        out_specs=out_specs,
        out_shape=out_shape,
        compiler_params=_params(1),
        name="proj",
    )(x2, norm_g, w_pad, cos_t, sin_t, gains, kgain)


def _online_update(parts, ms, acc_ref):
    ps, out = [], []
    for c, tiles in enumerate(parts):
        m_new = ms[c]
        for s, _, ok in tiles:
            smax = jnp.max(s, axis=0, keepdims=True)
            m_new = jnp.maximum(m_new, smax if ok is None else jnp.where(ok, smax, NEG_BIG))
        m_eff = jnp.maximum(m_new, M_FLOOR)
        out.append(m_new)
        probs = [jnp.exp2(s - (m_eff if ok is None else jnp.where(ok, m_eff, -NEG_BIG))).astype(BF16)
                 for s, _, ok in tiles]
        ps.append((jnp.exp2(ms[c] - m_new), probs))
    for c, (alpha, probs) in enumerate(ps):
        acc = alpha * acc_ref[c]
        for (_, vt, _), p in zip(parts[c], probs):
            acc = acc + _dot(vt, p)
        acc_ref[c] = acc
    return out


def _init_max(n_chains, tq):
    return tuple(jnp.full((1, tq), NEG_BIG, F32) for _ in range(n_chains))


def _with_ones_row(vt):
    row = lax.broadcasted_iota(jnp.int32, (V_PAD, vt.shape[1]), 0)
    return jnp.concatenate([vt, jnp.where(row == 0, 1.0, 0.0).astype(vt.dtype)], axis=0)


def _normalized(acc_ref, c, dv):
    acc = acc_ref[c]
    return acc[:dv] / acc[dv:dv + 1]


def _head_slice(h, width=HEAD_DIM):
    return slice(h * width, (h + 1) * width)


def _moba_kernel(qt_ref, k_ref, vt_ref, kmean_ref, o_ref, acc_ref, *, n_sel):
    i = pl.program_id(1)
    blk = MOBA_BLOCK
    nb = kmean_ref.shape[1]
    km = kmean_ref[0]
    brow = lax.broadcasted_iota(jnp.int32, (nb, blk), 0)
    browf = brow.astype(F32)
    causal = (lax.broadcasted_iota(jnp.int32, (blk, 1), 0)
              <= lax.broadcasted_iota(jnp.int32, (1, blk), 1))
    qts = [qt_ref[0, _head_slice(h), :] for h in range(N_HEADS)]
    sels = []
    for h in range(N_HEADS):
        km_hi, km_lo = _split_bf16(km[:, _head_slice(h)])
        gate = _dot(km_hi, qts[h]) + _dot(km_lo, qts[h])
        gate = jnp.where(brow < i, gate, -jnp.inf)
        sel = jnp.zeros((nb, blk), F32)
        for _ in range(n_sel):
            gm = jnp.max(gate, axis=0, keepdims=True)
            is_m = (gate == gm) & (gm > -jnp.inf)
            first = jnp.min(jnp.where(is_m, browf, float(nb)), axis=0, keepdims=True)
            pick = browf == first
            sel = jnp.where(pick, 1.0, sel)
            gate = jnp.where(pick, -jnp.inf, gate)
        sels.append(sel)
    acc_ref[...] = jnp.zeros(acc_ref.shape, F32)

    def tile(j, h, mask=None, seen=False):
        rows = pl.ds(pl.multiple_of(j * blk, blk), blk)
        s = _dot(k_ref[rows, _head_slice(h)], qts[h])
        if mask is not None:
            s = jnp.where(mask, s, NEG_BIG)
        ok = jnp.sum(jnp.where(brow == j, sels[h], 0.0), axis=0, keepdims=True) > 0.0 if seen else None
        return s, _with_ones_row(vt_ref[j, _head_slice(h), :]), ok

    def pair(j2, ms):
        parts = [[tile(2 * j2, h, seen=True), tile(2 * j2 + 1, h, seen=True)] for h in range(N_HEADS)]
        return tuple(_online_update(parts, ms, acc_ref))

    ms = lax.fori_loop(0, i // 2, pair, _init_max(N_HEADS, blk))

    @pl.when(i % 2 == 1)
    def _():
        _online_update([[tile(i - 1, h, seen=True), tile(i, h, mask=causal)] for h in range(N_HEADS)],
                       ms, acc_ref)

    @pl.when(i % 2 == 0)
    def _():
        _online_update([[tile(i, h, mask=causal)] for h in range(N_HEADS)], ms, acc_ref)

    for h in range(N_HEADS):
        o_ref[0, _head_slice(h), :] = _normalized(acc_ref, h, HEAD_DIM).astype(o_ref.dtype)


def _moba(qt, k, vt, kmean, batch, seq):
    blk = MOBA_BLOCK
    nb = seq // blk
    n_sel = min(MOBA_TOPK, nb - 1)
    tspec = pl.BlockSpec((1, HW, blk), lambda b, i: (b, 0, i))
    return pl.pallas_call(
        functools.partial(_moba_kernel, n_sel=n_sel),
        grid=(batch, nb),
        in_specs=[tspec,
                  pl.BlockSpec((seq, HW), lambda b, i: (b, 0)),
                  pl.BlockSpec((nb, vt.shape[1], blk), lambda b, i: (b, 0, 0)),
                  pl.BlockSpec((1, nb, HW), lambda b, i: (b, 0, 0))],
        out_specs=tspec,
        out_shape=jax.ShapeDtypeStruct((batch, HW, seq), BF16),
        scratch_shapes=[pltpu.VMEM((N_HEADS, HEAD_DIM + V_PAD, blk), F32)],
        compiler_params=_params(2),
        name="moba",
    )(qt, k, vt, kmean)


def _bit_planes(words):
    a = list(words)
    assert len(a) == 32
    mask, j = 0x0000FFFF, 16
    while j:
        k = 0
        while k < 32:
            t = (a[k] ^ (a[k + j] >> j)) & mask
            a[k] = a[k] ^ t
            a[k + j] = a[k + j] ^ (t << j)
            k = (k + j + 1) & ~j
        j >>= 1
        mask = (mask ^ (mask << j)) & 0xFFFFFFFF
    return a


def _dsa_kernel(qit_ref, wi_ref, ki_ref, qt_ref, k_ref, vt_ref, o_ref, plane_ref, sel_ref, acc_ref, *, n_keep):
    i = pl.program_id(1)
    blk = ROW_TILE
    n_chunk = i + 1
    n_planes, n_slots, sub = plane_ref.shape[:3]
    v_bits, s_bits = (n_planes - 1).bit_length(), (sub - 1).bit_length()
    idx_bits = v_bits + s_bits + (n_slots - 1).bit_length()

    @pl.when(i == 0)
    def _():
        plane_ref[:, 1:] = jnp.zeros((n_planes, n_slots - 1) + plane_ref.shape[2:], jnp.int32)

    w_t = wi_ref[...].T
    qpos = i * blk + lax.broadcasted_iota(jnp.int32, (1, blk), 1)
    krow = lax.broadcasted_iota(jnp.int32, (blk, 1), 0)

    def score_chunk(c, carry):
        kc = ki_ref[pl.ds(pl.multiple_of(c * blk, blk), blk), :]
        lgs = [_dot(kc, qit_ref[0, _head_slice(h, IDX_DIM), :]) for h in range(IDX_HEADS)]
        sc = jnp.zeros((blk, blk), F32)
        for h in range(IDX_HEADS):
            sc = sc + w_t[h:h + 1, :] * jnp.maximum(lgs[h], 0.0)
        sc = sc + 0.0
        bits = pltpu.bitcast(sc, jnp.int32)
        key = jnp.where(bits < 0, bits ^ 0x7FFFFFFF, bits)
        key = jnp.where(c * blk + krow <= qpos, key, INT_MIN)
        words = (key ^ INT_MIN).reshape(n_planes, sub, blk)
        planes = _bit_planes([words[v] for v in range(n_planes)])
        for b in range(n_planes):
            plane_ref[b, c] = planes[n_planes - 1 - b]
        return carry

    lax.fori_loop(0, n_chunk, score_chunk, 0)

    keep = float(n_keep)

    def step(plane, eq, gt):
        cand = gt | (eq & plane)
        per_row = jnp.sum(lax.population_count(cand), axis=0)
        take = jnp.sum(per_row.astype(F32), axis=0, keepdims=True) >= keep
        return eq & jnp.where(take, plane, ~plane), jnp.where(take, gt, cand)

    def index_plane(t, shape):
        if t < s_bits:
            s = lax.broadcasted_iota(jnp.int32, shape, 1)
            return jnp.where(((s >> t) & 1) == 0, -1, 0)
        if t < s_bits + v_bits:
            word = sum(1 << j for j in range(n_planes) if (((n_planes - 1 - j) >> (t - s_bits)) & 1) == 0)
            return jnp.full(shape, word - (1 << 32) if word >= (1 << 31) else word, jnp.int32)
        slot = lax.broadcasted_iota(jnp.int32, shape, 0)
        return jnp.where(((slot >> (t - s_bits - v_bits)) & 1) == 0, -1, 0)

    def select(n_used, lo):
        shape = (n_used, sub, LANES)
        lanes = slice(lo, lo + LANES)

        def score_bit(t, carry):
            return step(plane_ref[n_planes - 1 - t, 0:n_used, :, lanes], *carry)

        eq, gt = lax.fori_loop(0, n_planes, score_bit,
                               (jnp.full(shape, -1, jnp.int32), jnp.zeros(shape, jnp.int32)))
        for t in reversed(range(idx_bits)):
            eq, gt = step(index_plane(t, shape), eq, gt)
        slot = lax.broadcasted_iota(jnp.int32, shape, 0)
        s = lax.broadcasted_iota(jnp.int32, shape, 1)
        qoff = lo + lax.broadcasted_iota(jnp.int32, shape, 2)
        reach = jnp.left_shift(-1, (n_planes - 1) - ((qoff - s) >> s_bits))
        valid = jnp.where(slot < i, -1, jnp.where((slot == i) & (qoff >= s), reach, 0))
        return (eq | gt) & valid

    half_slots = n_slots // 2
    for n_used, wanted in ((half_slots, n_chunk <= half_slots), (n_slots, n_chunk > half_slots)):
        @pl.when(wanted)
        def _():
            for lo in range(0, blk, LANES):
                sel_ref[0:n_used, :, lo:lo + LANES] = select(n_used, lo)
                if n_used < n_slots:
                    sel_ref[n_used:, :, lo:lo + LANES] = jnp.zeros((n_slots - n_used, sub, LANES), jnp.int32)

    qts = [qt_ref[0, _head_slice(h), :] for h in range(N_HEADS)]
    acc_ref[...] = jnp.zeros(acc_ref.shape, F32)

    def pair(c2, ms):
        parts = [[] for _ in range(N_HEADS)]
        for c in (2 * c2, 2 * c2 + 1):
            rows = pl.ds(pl.multiple_of(c * blk, blk), blk)
            w = sel_ref[c]
            allowed = jnp.concatenate([jnp.left_shift(w, v) for v in range(n_planes)], axis=0) < 0
            for h in range(N_HEADS):
                s = jnp.where(allowed, _dot(k_ref[rows, _head_slice(h)], qts[h]), NEG_BIG)
                parts[h].append((s, _with_ones_row(vt_ref[c, _head_slice(h), :]), None))
        return tuple(_online_update(parts, ms, acc_ref))

    lax.fori_loop(0, (n_chunk + 1) // 2, pair, _init_max(N_HEADS, blk))
    for h in range(N_HEADS):
        o_ref[0, _head_slice(h), :] = _normalized(acc_ref, h, HEAD_DIM).astype(o_ref.dtype)


def _dsa(qit, wi, ki, qt, k, vt, batch, seq):
    blk = ROW_TILE
    nq = seq // blk
    n_keep = min(DSA_TOPK_MAX, seq // 4)
    n_planes = 32
    tspec = lambda w: pl.BlockSpec((1, w, blk), lambda b, i: (b, 0, i))
    full = lambda w: pl.BlockSpec((seq, w), lambda b, i: (b, 0))
    return pl.pallas_call(
        functools.partial(_dsa_kernel, n_keep=n_keep),
        grid=(batch, nq),
        in_specs=[tspec(IDX_HEADS * IDX_DIM), pl.BlockSpec((blk, LANES), lambda b, i: (b * nq + i, 0)),
                  full(IDX_DIM), tspec(HW), full(HW),
                  pl.BlockSpec((nq, vt.shape[1], blk), lambda b, i: (b, 0, 0))],
        out_specs=tspec(HW),
        out_shape=jax.ShapeDtypeStruct((batch, HW, seq), BF16),
        scratch_shapes=[pltpu.VMEM((n_planes, nq, blk // n_planes, blk), jnp.int32),
                        pltpu.VMEM((nq, blk // n_planes, blk), jnp.int32),
                        pltpu.VMEM((N_HEADS, HEAD_DIM + V_PAD, blk), F32)],
        compiler_params=_params(2),
        name="dsa",
    )(qit, wi, ki, qt, k, vt)


DIFF_GROUP = 4


def _diff_kernel(q1t_ref, q2t_ref, k1_ref, k2_ref, vt_ref, dl_ref, gain_ref, o_ref, acc_ref, *, lam_init):
    i = pl.program_id(1)
    blk = ROW_TILE
    dl = dl_ref[...]
    lam = (jnp.exp(jnp.sum(dl[0:1] * dl[1:2], axis=-1, keepdims=True))
           - jnp.exp(jnp.sum(dl[2:3] * dl[3:4], axis=-1, keepdims=True)) + lam_init)
    causal = (lax.broadcasted_iota(jnp.int32, (blk, 1), 0)
              <= lax.broadcasted_iota(jnp.int32, (1, blk), 1))
    maps = ((q1t_ref, k1_ref), (q2t_ref, k2_ref))

    for h0 in range(0, N_HEADS, DIFF_GROUP):
        chains = [(h, mp) for h in range(h0, h0 + DIFF_GROUP) for mp in range(2)]
        qts = [maps[mp][0][0, _head_slice(h), :] for h, mp in chains]
        acc_ref[...] = jnp.zeros(acc_ref.shape, F32)

        def tile(j, c, mask=None):
            h, mp = chains[c]
            rows = pl.ds(pl.multiple_of(j * blk, blk), blk)
            s = _dot(maps[mp][1][rows, _head_slice(h)], qts[c])
            if mask is not None:
                s = jnp.where(mask, s, NEG_BIG)
            return s, _with_ones_row(vt_ref[j, _head_slice(h, C_VDIM), :]), None

        def pair(j2, ms):
            parts = [[tile(2 * j2, c), tile(2 * j2 + 1, c)] for c in range(len(chains))]
            return tuple(_online_update(parts, ms, acc_ref))

        ms = lax.fori_loop(0, i // 2, pair, _init_max(len(chains), blk))

        @pl.when(i % 2 == 1)
        def _():
            _online_update([[tile(i - 1, c), tile(i, c, causal)] for c in range(len(chains))], ms, acc_ref)

        @pl.when(i % 2 == 0)
        def _():
            _online_update([[tile(i, c, causal)] for c in range(len(chains))], ms, acc_ref)

        for g in range(DIFF_GROUP):
            h = h0 + g
            o = _normalized(acc_ref, 2 * g, C_VDIM) - lam * _normalized(acc_ref, 2 * g + 1, C_VDIM)
            ms = jnp.mean(o * o, axis=0, keepdims=True)
            o = o * lax.rsqrt(ms + EPS) * gain_ref[...] * (1.0 - lam_init)
            o_ref[0, _head_slice(h, C_VDIM), :] = o.astype(o_ref.dtype)


def _diff(q1t, q2t, k1, k2, vt, dl, gain, lam_init, batch, seq):
    blk = ROW_TILE
    nq = seq // blk
    tspec = lambda w: pl.BlockSpec((1, w, blk), lambda b, i: (b, 0, i))
    kspec = pl.BlockSpec((seq, HW), lambda b, i: (b, 0))
    vw = N_HEADS * C_VDIM
    return pl.pallas_call(
        functools.partial(_diff_kernel, lam_init=lam_init),
        grid=(batch, nq),
        in_specs=[tspec(HW), tspec(HW), kspec, kspec,
                  pl.BlockSpec((nq, vt.shape[1], blk), lambda b, i: (b, 0, 0)),
                  pl.BlockSpec((4, HEAD_DIM), lambda b, i: (0, 0)),
                  pl.BlockSpec((C_VDIM, 1), lambda b, i: (0, 0))],
        out_specs=tspec(vw),
        out_shape=jax.ShapeDtypeStruct((batch, vw, seq), BF16),
        scratch_shapes=[pltpu.VMEM((2 * DIFF_GROUP, C_VDIM + V_PAD, blk), F32)],
        compiler_params=_params(2),
        name="diff",
    )(q1t, q2t, k1, k2, vt, dl, gain)


def _merge_kernel(x_ref, oa_ref, ob_ref, oc_ref, sg_ref, wa_ref, wb_ref, wc_ref, wo_ref, g_ref,
                  wr_hi_ref, wr_lo_ref, br_ref, x1_ref, hn_ref, route_ref):
    merged = (sg_ref[:, 0:D_MODEL].astype(F32) * _dot(oa_ref[...], wa_ref[...])
              + sg_ref[:, D_MODEL:2 * D_MODEL].astype(F32) * _dot(ob_ref[...], wb_ref[...])
              + sg_ref[:, 2 * D_MODEL:3 * D_MODEL].astype(F32) * _dot(oc_ref[...], wc_ref[...]))
    x1 = x_ref[...] + _dot(merged.astype(BF16), wo_ref[...])
    x1_ref[...] = x1
    ms = jnp.mean(x1 * x1, axis=-1, keepdims=True)
    hn = x1 * lax.rsqrt(ms + EPS) * g_ref[...]
    hn_ref[...] = hn

    hi, lo = _split_bf16(hn)
    lg = (_dot(hi, wr_hi_ref[...]) + _dot(lo, wr_hi_ref[...]) + _dot(hi, wr_lo_ref[...])
          + br_ref[...])
    lane = lax.broadcasted_iota(jnp.int32, lg.shape, 1)
    lanef = lane.astype(F32)
    far = float(LANES)
    is_g = (lane >= N_EXPERTS) & (lane < N_EXPERTS + N_GROUPS)
    gl = jnp.where(is_g, lg, -jnp.inf)
    gmax = jnp.max(gl, axis=-1, keepdims=True)
    gidx = jnp.min(jnp.where(gl == gmax, lanef, far), axis=-1, keepdims=True) - float(N_EXPERTS)
    g_w = 1.0 / jnp.sum(jnp.where(is_g, jnp.exp(gl - gmax), 0.0), axis=-1, keepdims=True)
    in_group = (lane < N_EXPERTS) & ((lane // EXPERTS_PER_GROUP).astype(F32) == gidx)
    el = jnp.where(in_group, lg, -jnp.inf)
    e1 = jnp.max(el, axis=-1, keepdims=True)
    i1 = jnp.min(jnp.where(el == e1, lanef, far), axis=-1, keepdims=True)
    el2 = jnp.where(lanef == i1, -jnp.inf, el)
    e2 = jnp.max(el2, axis=-1, keepdims=True)
    i2 = jnp.min(jnp.where(el2 == e2, lanef, far), axis=-1, keepdims=True)
    t = jnp.exp(e2 - e1)
    w1 = g_w / (1.0 + t)
    w2 = g_w * t / (1.0 + t)
    route_ref[...] = jnp.where(lane == 0, i1, jnp.where(lane == 1, i2, jnp.where(lane == 2, w1, jnp.where(lane == 3, w2, 0.0))))


def _merge(x2, oa, ob, oc, sg, wa, wb, wc, wo, norm_g, wr_hi, wr_lo, br):
    n = x2.shape[0]
    tm = min(DENSE_TILE, n)
    row = lambda w: pl.BlockSpec((tm, w), lambda i: (i, 0))
    const = lambda a: pl.BlockSpec(a.shape, lambda i: (0, 0))
    return pl.pallas_call(
        _merge_kernel,
        grid=(n // tm,),
        in_specs=[row(D_MODEL), row(HW), row(HW), row(N_HEADS * C_VDIM), row(3 * D_MODEL),
                  const(wa), const(wb), const(wc), const(wo), const(norm_g), const(wr_hi),
                  const(wr_lo), const(br)],
        out_specs=[row(D_MODEL), row(D_MODEL), row(LANES)],
        out_shape=[jax.ShapeDtypeStruct((n, D_MODEL), F32), jax.ShapeDtypeStruct((n, D_MODEL), F32),
                   jax.ShapeDtypeStruct((n, LANES), F32)],
        compiler_params=_params(1),
        name="merge",
    )(x2, oa, ob, oc, sg, wa, wb, wc, wo, norm_g, wr_hi, wr_lo, br)


MOE_BLOCK = 512
META_ROWS = 8
META_USED = 3 * LANES
META_END = 4 * LANES
META_PADDED = 5 * LANES


def _lane_prefix_sum(x):
    lane = lax.broadcasted_iota(jnp.int32, x.shape, 1)
    shift = 1
    while shift < LANES:
        x = x + jnp.where(lane >= shift, pltpu.roll(x, shift, 1), 0.0)
        shift *= 2
    return x


def _positions_kernel(route_ref, dest_ref, meta_ref, cnt_ref, base_ref):
    phase = pl.program_id(0)
    t = pl.program_id(1)
    tm = route_ref.shape[0]
    route = route_ref[...]
    lane = lax.broadcasted_iota(jnp.int32, route.shape, 1)
    lanef = lane.astype(F32)
    e1 = route[:, 0:1]
    e2 = route[:, 1:2]
    uses = jnp.where((lanef == e1) | (lanef == e2), 1.0, 0.0)
    tile_cnt = jnp.sum(uses, axis=0, keepdims=True)

    @pl.when((phase == 0) & (t == 0))
    def _():
        cnt_ref[...] = jnp.zeros(cnt_ref.shape, F32)

    @pl.when(phase == 0)
    def _():
        cnt_ref[...] += tile_cnt

    @pl.when((phase == 1) & (t == 0))
    def _():
        cnt = cnt_ref[...]
        padded = jnp.floor((cnt + (MOE_BLOCK - 1)) * (1.0 / MOE_BLOCK)) * MOE_BLOCK
        end = _lane_prefix_sum(padded)
        base_ref[...] = end - padded
        cnt_ref[...] = jnp.zeros(cnt_ref.shape, F32)
        lane1 = lax.broadcasted_iota(jnp.int32, (1, LANES), 1)
        row = lax.broadcasted_iota(jnp.int32, (META_ROWS, LANES), 0)
        col = lax.broadcasted_iota(jnp.int32, (META_ROWS, LANES), 1)
        first_row = ((row * LANES + col) * MOE_BLOCK).astype(F32)
        owner = jnp.zeros((META_ROWS, LANES), F32)
        for e in range(N_EXPERTS):
            end_e = jnp.sum(jnp.where(lane1 == e, end, 0.0), axis=-1, keepdims=True)
            owner = owner + jnp.where(end_e <= first_row, 1.0, 0.0)
        owner = jnp.minimum(owner, float(N_EXPERTS - 1))
        used = jnp.sum(jnp.where(lane1 == N_EXPERTS - 1, end, 0.0), axis=-1, keepdims=True) * (1.0 / MOE_BLOCK)
        meta = jnp.where(row == META_USED // LANES, used,
                         jnp.where(row == META_END // LANES, end,
                                   jnp.where(row == META_PADDED // LANES, padded, owner)))
        meta_ref[...] = meta.astype(jnp.int32)

    @pl.when(phase == 1)
    def _():
        before = (lax.broadcasted_iota(jnp.int32, (tm, tm), 1)
                  < lax.broadcasted_iota(jnp.int32, (tm, tm), 0)).astype(BF16)
        rank = _dot(before, uses.astype(BF16))
        pos = base_ref[...] + cnt_ref[...] + rank
        d1 = jnp.sum(jnp.where(lanef == e1, pos, 0.0), axis=-1, keepdims=True)
        d2 = jnp.sum(jnp.where(lanef == e2, pos, 0.0), axis=-1, keepdims=True)
        dest_ref[...] = jnp.where(lane == 0, d1, jnp.where(lane == 1, d2, 0.0)).astype(jnp.int32)
        cnt_ref[...] += tile_cnt


def _positions(route):
    n = route.shape[0]
    tm = min(4 * ROW_TILE, n)
    return pl.pallas_call(
        _positions_kernel,
        grid=(2, n // tm),
        in_specs=[pl.BlockSpec((tm, LANES), lambda p, t: (t, 0))],
        out_specs=[pl.BlockSpec((tm, LANES), lambda p, t: (t * p, 0)),
                   pl.BlockSpec((META_ROWS, LANES), lambda p, t: (0, 0))],
        out_shape=[jax.ShapeDtypeStruct((n, LANES), jnp.int32),
                   jax.ShapeDtypeStruct((META_ROWS, LANES), jnp.int32)],
        scratch_shapes=[pltpu.VMEM((1, LANES), F32), pltpu.VMEM((1, LANES), F32)],
        compiler_params=_params(2),
        name="moe_positions",
    )(route)


def _row_copy(src_ref, src_row, dst_ref, dst_row, sem):
    return pltpu.make_async_copy(src_ref.at[pl.ds(src_row, 1), :], dst_ref.at[pl.ds(dst_row, 1), :], sem)


def _dispatch_kernel(meta_ref, dest_ref, hn_ref, xs_ref, zero_ref, stage_ref, sem, in_sems, row_sems):
    t = pl.program_id(0)
    last = pl.num_programs(0) - 1
    tm = stage_ref.shape[1]

    def fetch(tile):
        return pltpu.make_async_copy(hn_ref.at[pl.ds(pl.multiple_of(tile * tm, tm), tm), :],
                                     stage_ref.at[tile % 3], in_sems.at[tile % 2])

    @pl.when(t == 0)
    def _():
        fetch(0).start()

    @pl.when(t < last)
    def _():
        fetch(t + 1).start()

    @pl.when(t == 0)
    def _():
        zero_ref[...] = jnp.zeros(zero_ref.shape, F32)

        def fill(e):
            end = pl.multiple_of(meta_ref[META_END + e], MOE_BLOCK)
            return pltpu.make_async_copy(zero_ref, xs_ref.at[pl.ds(end - MOE_BLOCK, MOE_BLOCK), :], sem)

        for e in range(N_EXPERTS):
            @pl.when(meta_ref[META_PADDED + e] > 0)
            def _():
                fill(e).start()
        for e in range(N_EXPERTS):
            @pl.when(meta_ref[META_PADDED + e] > 0)
            def _():
                fill(e).wait()

        def spare(b):
            return pltpu.make_async_copy(
                zero_ref, xs_ref.at[pl.ds(pl.multiple_of(b * MOE_BLOCK, MOE_BLOCK), MOE_BLOCK), :], sem)

        n_blocks = xs_ref.shape[0] // MOE_BLOCK
        lax.fori_loop(meta_ref[META_USED], n_blocks, lambda b, c: (spare(b).start(), c)[1], 0)
        lax.fori_loop(meta_ref[META_USED], n_blocks, lambda b, c: (spare(b).wait(), c)[1], 0)

    fetch(t).wait()
    rows_ref = stage_ref.at[t % 3]

    def start(r, carry):
        for k in range(2):
            _row_copy(rows_ref, r, xs_ref, dest_ref[0, 0, 2 * r + k], row_sems.at[t % 2]).start()
        return carry

    lax.fori_loop(0, tm, start, 0, unroll=8)

    def wait_rows(tile):
        for _ in range(2):
            pltpu.make_async_copy(stage_ref.at[tile % 3], xs_ref.at[pl.ds(0, tm), :], row_sems.at[tile % 2]).wait()

    @pl.when(t > 0)
    def _():
        wait_rows(t - 1)

    @pl.when(t == last)
    def _():
        wait_rows(t)


def _dispatch(meta, dest3, hn, n_rows):
    n = hn.shape[0]
    tm = dest3.shape[2] // 2
    return pl.pallas_call(
        _dispatch_kernel,
        grid_spec=pltpu.PrefetchScalarGridSpec(
            num_scalar_prefetch=1,
            grid=(n // tm,),
            in_specs=[pl.BlockSpec((1, 1, 2 * tm), lambda t, m: (t, 0, 0), memory_space=pltpu.SMEM),
                      pl.BlockSpec(memory_space=pl.ANY)],
            out_specs=pl.BlockSpec(memory_space=pl.ANY),
            scratch_shapes=[pltpu.VMEM((MOE_BLOCK, D_MODEL), F32), pltpu.VMEM((3, tm, D_MODEL), F32),
                            pltpu.SemaphoreType.DMA(()), pltpu.SemaphoreType.DMA((2,)),
                            pltpu.SemaphoreType.DMA((2,))]),
        out_shape=jax.ShapeDtypeStruct((n_rows, D_MODEL), F32),
        compiler_params=_params(1),
        name="moe_dispatch",
    )(meta, dest3, hn)


def _expert_kernel(meta_ref, xs_ref, wg_ref, wu_ref, wd_ref, y_ref, wg_bf, wu_bf, wd_bf):
    b = pl.program_id(0)
    holds_rows = b < meta_ref[META_USED]
    new_expert = (b == 0) | (meta_ref[b] != meta_ref[jnp.maximum(b - 1, 0)])

    @pl.when(holds_rows & new_expert)
    def _():
        wg_bf[...] = wg_ref[0].astype(BF16)
        wu_bf[...] = wu_ref[0].astype(BF16)
        wd_bf[...] = wd_ref[0].astype(BF16)

    @pl.when(holds_rows)
    def _():
        x = xs_ref[...].astype(BF16)
        g = _dot(x, wg_bf[...])
        u = _dot(x, wu_bf[...])
        hid = g * (1.0 / (1.0 + jnp.exp(-g))) * u
        y_ref[...] = _dot(hid.astype(BF16), wd_bf[...])

    @pl.when(jnp.logical_not(holds_rows))
    def _():
        y_ref[...] = jnp.zeros(y_ref.shape, F32)


def _experts(meta, xs, wg, wu, wd, layer):
    n_blocks = xs.shape[0] // MOE_BLOCK
    rows = pl.BlockSpec((MOE_BLOCK, D_MODEL), lambda b, m: (b, 0))
    weight = lambda shape: pl.BlockSpec(
        (None, 1) + shape, lambda b, m: (layer, m[jnp.minimum(b, m[META_USED] - 1)], 0, 0))
    return pl.pallas_call(
        _expert_kernel,
        grid_spec=pltpu.PrefetchScalarGridSpec(
            num_scalar_prefetch=1,
            grid=(n_blocks,),
            in_specs=[rows, weight((D_MODEL, D_EXPERT)), weight((D_MODEL, D_EXPERT)),
                      weight((D_EXPERT, D_MODEL))],
            out_specs=rows,
            scratch_shapes=[pltpu.VMEM((D_MODEL, D_EXPERT), BF16), pltpu.VMEM((D_MODEL, D_EXPERT), BF16),
                            pltpu.VMEM((D_EXPERT, D_MODEL), BF16)]),
        out_shape=jax.ShapeDtypeStruct(xs.shape, F32),
        compiler_params=_params(1),
        name="moe_experts",
    )(meta, xs, wg, wu, wd)


def _combine_kernel(dest_ref, next_dest_ref, x1_ref, route_ref, y_ref, o_ref, buf_ref, sems):
    t = pl.program_id(0)
    tm = x1_ref.shape[0]

    def gather(d_ref, parity):
        def start(g, carry):
            first = pl.multiple_of(g * 8, 8)
            for u in range(8):
                for k in range(2):
                    _row_copy(y_ref, d_ref[0, 0, 2 * (g * 8 + u) + k], buf_ref.at[parity, k], first + u,
                              sems.at[parity]).start()
            return carry
        lax.fori_loop(0, tm // 8, start, 0)

    @pl.when(t == 0)
    def _():
        gather(dest_ref, 0)

    @pl.when(t + 1 < pl.num_programs(0))
    def _():
        gather(next_dest_ref, (t + 1) % 2)

    for k in range(2):
        pltpu.make_async_copy(y_ref.at[pl.ds(0, tm), :], buf_ref.at[t % 2, k], sems.at[t % 2]).wait()
    route = route_ref[...]
    o_ref[...] = x1_ref[...] + route[:, 2:3] * buf_ref[t % 2, 0] + route[:, 3:4] * buf_ref[t % 2, 1]


def _combine(dest3, x1, route, y):
    n = x1.shape[0]
    tm = dest3.shape[2] // 2
    row = lambda w: pl.BlockSpec((tm, w), lambda t: (t, 0))
    n_tiles = n // tm
    slots = lambda index: pl.BlockSpec((1, 1, 2 * tm), index, memory_space=pltpu.SMEM)
    return pl.pallas_call(
        _combine_kernel,
        grid=(n_tiles,),
        in_specs=[slots(lambda t: (t, 0, 0)), slots(lambda t: (jnp.minimum(t + 1, n_tiles - 1), 0, 0)),
                  row(D_MODEL), row(LANES), pl.BlockSpec(memory_space=pl.ANY)],
        out_specs=row(D_MODEL),
        out_shape=jax.ShapeDtypeStruct((n, D_MODEL), F32),
        scratch_shapes=[pltpu.VMEM((2, 2, tm, D_MODEL), F32), pltpu.SemaphoreType.DMA((2,))],
        compiler_params=_params(1),
        name="moe_combine",
    )(dest3, dest3, x1, route, y)


def _moe(x1, hn, route, wg, wu, wd, layer):
    n = x1.shape[0]
    tm = min(ROW_TILE, n)
    n_blocks = -(-(2 * n + N_EXPERTS * (MOE_BLOCK - 1)) // MOE_BLOCK)
    assert n_blocks <= META_USED
    dest, meta = _positions(route)
    meta = meta.reshape(-1)
    dest3 = dest[:, :2].reshape(n // tm, 1, 2 * tm)
    xs = _dispatch(meta, dest3, hn, n_blocks * MOE_BLOCK)
    y = _experts(meta, xs, wg, wu, wd, layer)
    return _combine(dest3, x1, route, y)


def _rope_tables(seq):
    inv_freq = 1.0 / (ROPE_THETA ** (jnp.arange(0, HEAD_DIM, 2, dtype=F32) / HEAD_DIM))
    ang = jnp.arange(seq, dtype=F32)[:, None] * inv_freq[None, :]
    cos, sin = jnp.cos(ang), jnp.sin(ang)
    cos_t = jnp.tile(jnp.concatenate([cos, cos], axis=-1), (1, N_HEADS))
    sin_t = jnp.tile(jnp.concatenate([-sin, sin], axis=-1), (1, N_HEADS))
    return cos_t, sin_t


def _to_t(v, batch, n_chunk):
    feat = v.shape[1]
    return (v.reshape(batch, n_chunk, ROW_TILE, feat).transpose(0, 1, 3, 2)
            .reshape(batch * n_chunk, feat, ROW_TILE))


def _from_t(o_t):
    b, feat, seq = o_t.shape
    return o_t.transpose(0, 2, 1).reshape(b * seq, feat)


def kernel(x, norm_attn, w_in, qk_gain, idx_k_gain, diff_lambda, diff_subln_gain, w_proj_a, w_proj_b, w_proj_c, w_out, norm_ffn, w_group, b_group, w_router, b_router, w_e_gate, w_e_up, w_e_down):
    batch, seq, d = x.shape
    assert d == D_MODEL and seq % (2 * ROW_TILE) == 0 and ROW_TILE == MOBA_BLOCK
    n = batch * seq
    nq = seq // ROW_TILE
    depth = w_in.shape[0]
    cos_t, sin_t = _rope_tables(seq)
    x2 = x.reshape(n, d)
    w_pad = jnp.concatenate(
        [w_in[:, :, :KW_SRC], jnp.zeros((depth, d, LANES - IDX_DIM - IDX_HEADS), F32), w_in[:, :, KW_SRC:]],
        axis=2).astype(BF16)
    for l in range(depth):
        gains = jnp.tile(qk_gain[l][jnp.array([0, 1, 2, 3, 4, 4, 5, 5])], (1, N_HEADS))
        kgain = jnp.pad(idx_k_gain[l], (0, LANES - IDX_DIM))[None, :]
        (qa, ka, va, qb, kb, vb, qi, ki, wi, q1, q2, k1, k2, vc, sg, kmean) = _project(
            x2, norm_attn[l][None, :], w_pad, l, cos_t, sin_t, gains, kgain, seq)

        feat_major = lambda t: t.reshape(batch, seq, t.shape[1]).transpose(0, 2, 1)
        o_a = _from_t(_moba(feat_major(qa), ka, _to_t(va, batch, nq), kmean.reshape(batch, nq, HW),
                            batch, seq))
        o_b = _from_t(_dsa(feat_major(qi), wi, ki, feat_major(qb), kb, _to_t(vb, batch, nq), batch, seq))
        lam_init = 0.8 - 0.6 * math.exp(-0.3 * l)
        o_c = _from_t(_diff(feat_major(q1), feat_major(q2), k1, k2, _to_t(vc, batch, nq), diff_lambda[l],
                            diff_subln_gain[l][:, None], lam_init, batch, seq))

        w_r = jnp.concatenate([w_router[l], w_group[l],
                               jnp.zeros((d, LANES - N_EXPERTS - N_GROUPS), F32)], axis=1)
        wr_hi = w_r.astype(BF16)
        wr_lo = (w_r - wr_hi.astype(F32)).astype(BF16)
        b_r = jnp.concatenate([b_router[l], b_group[l],
                               jnp.zeros((LANES - N_EXPERTS - N_GROUPS,), F32)])[None, :]
        x1, hn, route = _merge(x2, o_a, o_b, o_c, sg, w_proj_a[l].astype(BF16), w_proj_b[l].astype(BF16),
                            w_proj_c[l].astype(BF16), w_out[l].astype(BF16), norm_ffn[l][None, :],
                            wr_hi, wr_lo, b_r)
        x2 = _moe(x1, hn, route, w_e_gate, w_e_up, w_e_down, l)
    return x2.reshape(batch, seq, d)
```

```python
import functools
import math

import jax
import jax.numpy as jnp
from jax import lax
from jax.experimental import pallas as pl
from jax.experimental.pallas import tpu as pltpu

F32 = jnp.float32
BF16 = jnp.bfloat16

D_MODEL = 1024
HEAD_DIM = 64
ROPE_THETA = 10000.0
EPS = 1e-6
N_HEADS = 4
MOBA_BLOCK = 256
MOBA_TOPK = 3
IDX_HEADS = 8
IDX_DIM = 64
DSA_TOPK_MAX = 256
C_VDIM = 2 * HEAD_DIM
N_GROUPS = 4
EXPERTS_PER_GROUP = 8
N_EXPERTS = N_GROUPS * EXPERTS_PER_GROUP
D_EXPERT = 512

HW = N_HEADS * HEAD_DIM
LANES = 128
ROW_TILE = 256
DENSE_TILE = 512
VMEM_LIMIT = 56 * 1024 * 1024

_SEG = {}
_off = 0
for _name, _w in (("qa", HW), ("ka", HW), ("va", HW), ("qb", HW), ("kb", HW), ("vb", HW),
                  ("qi", IDX_HEADS * IDX_DIM), ("kw", LANES), ("q1", HW), ("q2", HW), ("k1", HW),
                  ("k2", HW), ("vc", N_HEADS * C_VDIM), ("ga", D_MODEL), ("gb", D_MODEL),
                  ("gc", D_MODEL)):
    _SEG[_name] = (_off, _w)
    _off += _w
D_IN_PAD = _off
KW_SRC = 6 * HW + IDX_HEADS * IDX_DIM + IDX_DIM + IDX_HEADS

NEG_BIG = -1e30
M_FLOOR = -1e20
INT_MIN = -(2 ** 31)
LOG2E = math.log2(math.e)
Q_SCALE = HEAD_DIM ** -0.5 * LOG2E
V_PAD = 16
NT_DIMS = (((1,), (1,)), ((), ()))


def _params(n_axes):
    return pltpu.CompilerParams(dimension_semantics=("arbitrary",) * n_axes,
                                vmem_limit_bytes=VMEM_LIMIT)


def _dot(a, b):
    return jnp.dot(a, b, preferred_element_type=F32)


def _dot_nt(a, b):
    return lax.dot_general(a, b, NT_DIMS, preferred_element_type=F32)


def _split_bf16(a):
    hi = a.astype(BF16)
    return hi, (a - hi.astype(F32)).astype(BF16)


def _swap_halves(y, width):
    lane = lax.broadcasted_iota(jnp.int32, y.shape, 1)
    first = (lane % HEAD_DIM) < (HEAD_DIM // 2)
    return jnp.where(first, pltpu.roll(y, width - HEAD_DIM // 2, 1), pltpu.roll(y, HEAD_DIM // 2, 1))


def _proj_kernel(x_ref, g_ref, w_ref, cos_ref, sin_ref, gain_ref, kgain_ref,
                 qa_ref, ka_ref, va_ref, qb_ref, kb_ref, vb_ref, qi_ref, ki_ref, wi_ref,
                 q1_ref, q2_ref, k1_ref, k2_ref, vc_ref, sg_ref, kmean_ref):
    x = x_ref[...]
    ms = jnp.mean(x * x, axis=-1, keepdims=True)
    h = (x * lax.rsqrt(ms + EPS) * g_ref[...]).astype(BF16)
    cos = cos_ref[...]
    sin = sin_ref[...]
    r = lax.broadcasted_iota(jnp.int32, (HW, HW), 0) // HEAD_DIM
    c = lax.broadcasted_iota(jnp.int32, (HW, HW), 1) // HEAD_DIM
    head_ones = (r == c).astype(BF16)

    def seg(name, lo=0, width=None):
        off, w = _SEG[name]
        width = w if width is None else width
        return _dot(h, w_ref[:, off + lo:off + lo + width])

    def rope(y):
        return y * cos + _swap_halves(y, HW) * sin

    def norm_rope(t, gain_row):
        hi, lo = _split_bf16(t * t)
        ss = _dot(hi, head_ones) + _dot(lo, head_ones)
        yn = t * lax.rsqrt(ss * (1.0 / HEAD_DIM) + EPS) * gain_ref[gain_row:gain_row + 1, :]
        return rope(yn)

    qa_ref[...] = (norm_rope(seg("qa"), 0) * Q_SCALE).astype(BF16)
    ka = norm_rope(seg("ka"), 1)
    ka_ref[...] = ka.astype(BF16)
    for blk in range(ka.shape[0] // MOBA_BLOCK):
        kmean_ref[blk] = jnp.mean(ka[blk * MOBA_BLOCK:(blk + 1) * MOBA_BLOCK], axis=0, keepdims=True)
    va_ref[...] = seg("va").astype(BF16)
    qb_ref[...] = (norm_rope(seg("qb"), 2) * Q_SCALE).astype(BF16)
    kb_ref[...] = norm_rope(seg("kb"), 3).astype(BF16)
    vb_ref[...] = seg("vb").astype(BF16)
    for half in range(2):
        qi_ref[:, half * HW:(half + 1) * HW] = rope(seg("qi", half * HW, HW)).astype(BF16)

    t = seg("kw")
    lane = lax.broadcasted_iota(jnp.int32, t.shape, 1)
    is_k = lane < IDX_DIM
    kms = jnp.sum(jnp.where(is_k, t * t, 0.0), axis=-1, keepdims=True) * (1.0 / IDX_DIM)
    kn = t * lax.rsqrt(kms + EPS) * kgain_ref[...]
    kr = kn * cos[:, :LANES] + _swap_halves(kn, LANES) * sin[:, :LANES]
    ki_ref[...] = kr[:, :IDX_DIM].astype(BF16)
    w_scale = (IDX_HEADS ** -0.5) * (IDX_DIM ** -0.5)
    wi_ref[...] = jnp.where(lane < IDX_HEADS, pltpu.roll(t, LANES - IDX_DIM, 1) * w_scale, 0.0)

    q1_ref[...] = (norm_rope(seg("q1"), 4) * Q_SCALE).astype(BF16)
    q2_ref[...] = (norm_rope(seg("q2"), 5) * Q_SCALE).astype(BF16)
    k1_ref[...] = norm_rope(seg("k1"), 6).astype(BF16)
    k2_ref[...] = norm_rope(seg("k2"), 7).astype(BF16)
    for half in range(2):
        vc_ref[:, half * HW:(half + 1) * HW] = seg("vc", half * HW, HW).astype(BF16)
    for gi, name in enumerate(("ga", "gb", "gc")):
        for part in range(D_MODEL // 512):
            g = seg(name, part * 512, 512)
            lo = gi * D_MODEL + part * 512
            sg_ref[:, lo:lo + 512] = (1.0 / (1.0 + jnp.exp(-g))).astype(BF16)


def _project(x2, norm_g, w_pad, layer, cos_t, sin_t, gains, kgain, seq):
    n = x2.shape[0]
    tm = min(DENSE_TILE, seq)
    assert seq % tm == 0 and tm % MOBA_BLOCK == 0
    n_pos = seq // tm
    row = lambda w: pl.BlockSpec((tm, w), lambda i: (i, 0))
    const = lambda shape: pl.BlockSpec(shape, lambda i: (0,) * len(shape))
    out_widths = [HW] * 6 + [IDX_HEADS * IDX_DIM, IDX_DIM, LANES] + [HW] * 4 + [N_HEADS * C_VDIM, 3 * D_MODEL]
    out_dtypes = [BF16] * 8 + [F32] + [BF16] * 6
    out_shape = [jax.ShapeDtypeStruct((n, w), dt) for w, dt in zip(out_widths, out_dtypes)]
    out_shape.append(jax.ShapeDtypeStruct((n // MOBA_BLOCK, 1, HW), F32))
    out_specs = [row(w) for w in out_widths] + [pl.BlockSpec((tm // MOBA_BLOCK, 1, HW), lambda i: (i, 0, 0))]
    return pl.pallas_call(
        _proj_kernel,
        grid=(n // tm,),
        in_specs=[row(D_MODEL), const((1, D_MODEL)),
                  pl.BlockSpec((None, D_MODEL, D_IN_PAD), lambda i: (layer, 0, 0), pipeline_mode=pl.Buffered(1)),
                  pl.BlockSpec((tm, HW), lambda i: (i % n_pos, 0)),
                  pl.BlockSpec((tm, HW), lambda i: (i % n_pos, 0)),
                  const((8, HW)), const((1, LANES))],
        out_specs=out_specs,
        out_shape=out_shape,
        compiler_params=_params(1),
        name="proj",
    )(x2, norm_g, w_pad, cos_t, sin_t, gains, kgain)


def _online_update(parts, ms, acc_ref):
    ps, out = [], []
    for c, tiles in enumerate(parts):
        m_new = ms[c]
        for s, _, ok in tiles:
            smax = jnp.max(s, axis=0, keepdims=True)
            m_new = jnp.maximum(m_new, smax if ok is None else jnp.where(ok, smax, NEG_BIG))
        m_eff = jnp.maximum(m_new, M_FLOOR)
        out.append(m_new)
        probs = [jnp.exp2(s - (m_eff if ok is None else jnp.where(ok, m_eff, -NEG_BIG))).astype(BF16)
                 for s, _, ok in tiles]
        ps.append((jnp.exp2(ms[c] - m_new), probs))
    for c, (alpha, probs) in enumerate(ps):
        acc = alpha * acc_ref[c]
        for (_, vt, _), p in zip(parts[c], probs):
            acc = acc + _dot(vt, p)
        acc_ref[c] = acc
    return out


def _init_max(n_chains, tq):
    return tuple(jnp.full((1, tq), NEG_BIG, F32) for _ in range(n_chains))


def _with_ones_row(vt):
    row = lax.broadcasted_iota(jnp.int32, (V_PAD, vt.shape[1]), 0)
    return jnp.concatenate([vt, jnp.where(row == 0, 1.0, 0.0).astype(vt.dtype)], axis=0)


def _normalized(acc_ref, c, dv):
    acc = acc_ref[c]
    return acc[:dv] / acc[dv:dv + 1]


def _head_slice(h, width=HEAD_DIM):
    return slice(h * width, (h + 1) * width)


MOBA_GROUP = 4


def _moba_kernel(qt_ref, k_ref, vt_ref, kmean_ref, o_ref, acc_ref, *, n_sel):
    i = pl.program_id(1)
    blk = MOBA_BLOCK
    nb = kmean_ref.shape[1]
    km = kmean_ref[0]
    brow = lax.broadcasted_iota(jnp.int32, (nb, blk), 0)
    browf = brow.astype(F32)
    causal = (lax.broadcasted_iota(jnp.int32, (blk, 1), 0)
              <= lax.broadcasted_iota(jnp.int32, (1, blk), 1))
    qts = [qt_ref[0, _head_slice(h), :] for h in range(N_HEADS)]
    sels = []
    for h in range(N_HEADS):
        km_hi, km_lo = _split_bf16(km[:, _head_slice(h)])
        gate = _dot(km_hi, qts[h]) + _dot(km_lo, qts[h])
        gate = jnp.where(brow < i, gate, -jnp.inf)
        sel = jnp.zeros((nb, blk), F32)
        for _ in range(n_sel):
            gm = jnp.max(gate, axis=0, keepdims=True)
            is_m = (gate == gm) & (gm > -jnp.inf)
            first = jnp.min(jnp.where(is_m, browf, float(nb)), axis=0, keepdims=True)
            pick = browf == first
            sel = jnp.where(pick, 1.0, sel)
            gate = jnp.where(pick, -jnp.inf, gate)
        sels.append(sel)
    acc_ref[...] = jnp.zeros(acc_ref.shape, F32)

    def tile(j, h, mask=None, seen=False):
        rows = pl.ds(pl.multiple_of(j * blk, blk), blk)
        s = _dot(k_ref[rows, _head_slice(h)], qts[h])
        if mask is not None:
            s = jnp.where(mask, s, NEG_BIG)
        ok = jnp.sum(jnp.where(brow == j, sels[h], 0.0), axis=0, keepdims=True) > 0.0 if seen else None
        return s, _with_ones_row(vt_ref[j, _head_slice(h), :]), ok

    def absorb(first, n_past, diag, ms):
        parts = [[tile(first + u, h, seen=True) for u in range(n_past)]
                 + ([tile(i, h, mask=causal)] if diag else []) for h in range(N_HEADS)]
        return tuple(_online_update(parts, ms, acc_ref))

    group = MOBA_GROUP
    ms = lax.fori_loop(0, i // group, lambda g, ms: absorb(g * group, group, False, ms),
                       _init_max(N_HEADS, blk))
    for rest in range(group):
        @pl.when(i % group == rest)
        def _():
            absorb(i - rest, rest, True, ms)

    for h in range(N_HEADS):
        o_ref[0, _head_slice(h), :] = _normalized(acc_ref, h, HEAD_DIM).astype(o_ref.dtype)


def _moba(qt, k, vt, kmean, batch, seq):
    blk = MOBA_BLOCK
    nb = seq // blk
    n_sel = min(MOBA_TOPK, nb - 1)
    tspec = pl.BlockSpec((1, HW, blk), lambda b, i: (b, 0, i))
    return pl.pallas_call(
        functools.partial(_moba_kernel, n_sel=n_sel),
        grid=(batch, nb),
        in_specs=[tspec,
                  pl.BlockSpec((seq, HW), lambda b, i: (b, 0)),
                  pl.BlockSpec((nb, vt.shape[1], blk), lambda b, i: (b, 0, 0)),
                  pl.BlockSpec((1, nb, HW), lambda b, i: (b, 0, 0))],
        out_specs=tspec,
        out_shape=jax.ShapeDtypeStruct((batch, HW, seq), BF16),
        scratch_shapes=[pltpu.VMEM((N_HEADS, HEAD_DIM + V_PAD, blk), F32)],
        compiler_params=_params(2),
        name="moba",
    )(qt, k, vt, kmean)


def _bit_planes(words):
    a = list(words)
    assert len(a) == 32
    mask, j = 0x0000FFFF, 16
    while j:
        k = 0
        while k < 32:
            t = (a[k] ^ (a[k + j] >> j)) & mask
            a[k] = a[k] ^ t
            a[k + j] = a[k + j] ^ (t << j)
            k = (k + j + 1) & ~j
        j >>= 1
        mask = (mask ^ (mask << j)) & 0xFFFFFFFF
    return a


def _dsa_kernel(qit_ref, wi_ref, ki_ref, qt_ref, k_ref, vt_ref, o_ref, plane_ref, sel_ref, acc_ref, *, n_keep):
    i = pl.program_id(1)
    blk = ROW_TILE
    n_chunk = i + 1
    n_planes, n_slots, sub = plane_ref.shape[:3]
    v_bits, s_bits = (n_planes - 1).bit_length(), (sub - 1).bit_length()
    idx_bits = v_bits + s_bits + (n_slots - 1).bit_length()

    @pl.when(i == 0)
    def _():
        plane_ref[:, 1:] = jnp.zeros((n_planes, n_slots - 1) + plane_ref.shape[2:], jnp.int32)

    w_t = wi_ref[...].T
    causal = (lax.broadcasted_iota(jnp.int32, (blk, 1), 0)
              <= lax.broadcasted_iota(jnp.int32, (1, blk), 1))

    def score_chunk(c, diag):
        kc = ki_ref[pl.ds(pl.multiple_of(c * blk, blk), blk), :]
        lgs = [_dot(kc, qit_ref[0, _head_slice(h, IDX_DIM), :]) for h in range(IDX_HEADS)]
        sc = jnp.zeros((blk, blk), F32)
        for h in range(IDX_HEADS):
            sc = sc + w_t[h:h + 1, :] * jnp.maximum(lgs[h], 0.0)
        sc = sc + 0.0
        bits = pltpu.bitcast(sc, jnp.int32)
        key = bits ^ ((bits >> 31) | INT_MIN)
        if diag:
            key = jnp.where(causal, key, 0)
        words = key.reshape(n_planes, sub, blk)
        planes = _bit_planes([words[v] for v in range(n_planes)])
        for b in range(n_planes):
            plane_ref[b, c] = planes[n_planes - 1 - b]

    def score_pair(c2, carry):
        score_chunk(2 * c2, False)
        score_chunk(2 * c2 + 1, False)
        return carry

    lax.fori_loop(0, i // 2, score_pair, 0)

    @pl.when(i % 2 == 1)
    def _():
        score_chunk(i - 1, False)
        score_chunk(i, True)

    @pl.when(i % 2 == 0)
    def _():
        score_chunk(i, True)

    keep = float(n_keep)

    def step(plane, eq, gt):
        cand = gt | (eq & plane)
        per_row = jnp.sum(lax.population_count(cand), axis=0)
        take = jnp.sum(per_row.astype(F32), axis=0, keepdims=True) >= keep
        return eq & jnp.where(take, plane, ~plane), jnp.where(take, gt, cand)

    def index_plane(t, shape):
        if t < s_bits:
            s = lax.broadcasted_iota(jnp.int32, shape, 1)
            return jnp.where(((s >> t) & 1) == 0, -1, 0)
        if t < s_bits + v_bits:
            word = sum(1 << j for j in range(n_planes) if (((n_planes - 1 - j) >> (t - s_bits)) & 1) == 0)
            return jnp.full(shape, word - (1 << 32) if word >= (1 << 31) else word, jnp.int32)
        slot = lax.broadcasted_iota(jnp.int32, shape, 0)
        return jnp.where(((slot >> (t - s_bits - v_bits)) & 1) == 0, -1, 0)

    def select(n_used, lo):
        shape = (n_used, sub, LANES)
        lanes = slice(lo, lo + LANES)

        def score_bit(t, carry):
            return step(plane_ref[n_planes - 1 - t, 0:n_used, :, lanes], *carry)

        eq, gt = lax.fori_loop(0, n_planes, score_bit,
                               (jnp.full(shape, -1, jnp.int32), jnp.zeros(shape, jnp.int32)))
        for t in reversed(range(idx_bits)):
            eq, gt = step(index_plane(t, shape), eq, gt)
        slot = lax.broadcasted_iota(jnp.int32, shape, 0)
        s = lax.broadcasted_iota(jnp.int32, shape, 1)
        qoff = lo + lax.broadcasted_iota(jnp.int32, shape, 2)
        reach = jnp.left_shift(-1, (n_planes - 1) - ((qoff - s) >> s_bits))
        valid = jnp.where(slot < i, -1, jnp.where((slot == i) & (qoff >= s), reach, 0))
        return (eq | gt) & valid

    half_slots = n_slots // 2
    for n_used, wanted in ((half_slots, n_chunk <= half_slots), (n_slots, n_chunk > half_slots)):
        @pl.when(wanted)
        def _():
            for lo in range(0, blk, LANES):
                sel_ref[0:n_used, :, lo:lo + LANES] = select(n_used, lo)
                if n_used < n_slots:
                    sel_ref[n_used:, :, lo:lo + LANES] = jnp.zeros((n_slots - n_used, sub, LANES), jnp.int32)

    qts = [qt_ref[0, _head_slice(h), :] for h in range(N_HEADS)]
    acc_ref[...] = jnp.zeros(acc_ref.shape, F32)

    def pair(c2, ms):
        parts = [[] for _ in range(N_HEADS)]
        for c in (2 * c2, 2 * c2 + 1):
            rows = pl.ds(pl.multiple_of(c * blk, blk), blk)
            w = sel_ref[c]
            allowed = jnp.concatenate([jnp.left_shift(w, v) for v in range(n_planes)], axis=0) < 0
            for h in range(N_HEADS):
                s = jnp.where(allowed, _dot(k_ref[rows, _head_slice(h)], qts[h]), NEG_BIG)
                parts[h].append((s, _with_ones_row(vt_ref[c, _head_slice(h), :]), None))
        return tuple(_online_update(parts, ms, acc_ref))

    lax.fori_loop(0, (n_chunk + 1) // 2, pair, _init_max(N_HEADS, blk))
    for h in range(N_HEADS):
        o_ref[0, _head_slice(h), :] = _normalized(acc_ref, h, HEAD_DIM).astype(o_ref.dtype)


def _dsa(qit, wi, ki, qt, k, vt, batch, seq):
    blk = ROW_TILE
    nq = seq // blk
    n_keep = min(DSA_TOPK_MAX, seq // 4)
    n_planes = 32
    tspec = lambda w: pl.BlockSpec((1, w, blk), lambda b, i: (b, 0, i))
    full = lambda w: pl.BlockSpec((seq, w), lambda b, i: (b, 0))
    return pl.pallas_call(
        functools.partial(_dsa_kernel, n_keep=n_keep),
        grid=(batch, nq),
        in_specs=[tspec(IDX_HEADS * IDX_DIM), pl.BlockSpec((blk, LANES), lambda b, i: (b * nq + i, 0)),
                  full(IDX_DIM), tspec(HW), full(HW),
                  pl.BlockSpec((nq, vt.shape[1], blk), lambda b, i: (b, 0, 0))],
        out_specs=tspec(HW),
        out_shape=jax.ShapeDtypeStruct((batch, HW, seq), BF16),
        scratch_shapes=[pltpu.VMEM((n_planes, nq, blk // n_planes, blk), jnp.int32),
                        pltpu.VMEM((nq, blk // n_planes, blk), jnp.int32),
                        pltpu.VMEM((N_HEADS, HEAD_DIM + V_PAD, blk), F32)],
        compiler_params=_params(2),
        name="dsa",
    )(qit, wi, ki, qt, k, vt)


DIFF_GROUP = 4


def _diff_kernel(q1t_ref, q2t_ref, k1_ref, k2_ref, vt_ref, dl_ref, gain_ref, o_ref, acc_ref, *, lam_init):
    i = pl.program_id(1)
    blk = ROW_TILE
    dl = dl_ref[...]
    lam = (jnp.exp(jnp.sum(dl[0:1] * dl[1:2], axis=-1, keepdims=True))
           - jnp.exp(jnp.sum(dl[2:3] * dl[3:4], axis=-1, keepdims=True)) + lam_init)
    causal = (lax.broadcasted_iota(jnp.int32, (blk, 1), 0)
              <= lax.broadcasted_iota(jnp.int32, (1, blk), 1))
    maps = ((q1t_ref, k1_ref), (q2t_ref, k2_ref))

    for h0 in range(0, N_HEADS, DIFF_GROUP):
        chains = [(h, mp) for h in range(h0, h0 + DIFF_GROUP) for mp in range(2)]
        qts = [maps[mp][0][0, _head_slice(h), :] for h, mp in chains]
        acc_ref[...] = jnp.zeros(acc_ref.shape, F32)

        def tile(j, c, mask=None):
            h, mp = chains[c]
            rows = pl.ds(pl.multiple_of(j * blk, blk), blk)
            s = _dot(maps[mp][1][rows, _head_slice(h)], qts[c])
            if mask is not None:
                s = jnp.where(mask, s, NEG_BIG)
            return s, _with_ones_row(vt_ref[j, _head_slice(h, C_VDIM), :]), None

        def pair(j2, ms):
            parts = [[tile(2 * j2, c), tile(2 * j2 + 1, c)] for c in range(len(chains))]
            return tuple(_online_update(parts, ms, acc_ref))

        ms = lax.fori_loop(0, i // 2, pair, _init_max(len(chains), blk))

        @pl.when(i % 2 == 1)
        def _():
            _online_update([[tile(i - 1, c), tile(i, c, causal)] for c in range(len(chains))], ms, acc_ref)

        @pl.when(i % 2 == 0)
        def _():
            _online_update([[tile(i, c, causal)] for c in range(len(chains))], ms, acc_ref)

        for g in range(DIFF_GROUP):
            h = h0 + g
            o = _normalized(acc_ref, 2 * g, C_VDIM) - lam * _normalized(acc_ref, 2 * g + 1, C_VDIM)
            ms = jnp.mean(o * o, axis=0, keepdims=True)
            o = o * lax.rsqrt(ms + EPS) * gain_ref[...] * (1.0 - lam_init)
            o_ref[0, _head_slice(h, C_VDIM), :] = o.astype(o_ref.dtype)


def _diff(q1t, q2t, k1, k2, vt, dl, gain, lam_init, batch, seq):
    blk = ROW_TILE
    nq = seq // blk
    tspec = lambda w: pl.BlockSpec((1, w, blk), lambda b, i: (b, 0, i))
    kspec = pl.BlockSpec((seq, HW), lambda b, i: (b, 0))
    vw = N_HEADS * C_VDIM
    return pl.pallas_call(
        functools.partial(_diff_kernel, lam_init=lam_init),
        grid=(batch, nq),
        in_specs=[tspec(HW), tspec(HW), kspec, kspec,
                  pl.BlockSpec((nq, vt.shape[1], blk), lambda b, i: (b, 0, 0)),
                  pl.BlockSpec((4, HEAD_DIM), lambda b, i: (0, 0)),
                  pl.BlockSpec((C_VDIM, 1), lambda b, i: (0, 0))],
        out_specs=tspec(vw),
        out_shape=jax.ShapeDtypeStruct((batch, vw, seq), BF16),
        scratch_shapes=[pltpu.VMEM((2 * DIFF_GROUP, C_VDIM + V_PAD, blk), F32)],
        compiler_params=_params(2),
        name="diff",
    )(q1t, q2t, k1, k2, vt, dl, gain)


def _merge_kernel(x_ref, oa_ref, ob_ref, oc_ref, sg_ref, wa_ref, wb_ref, wc_ref, wo_ref, g_ref,
                  wr_hi_ref, wr_lo_ref, br_ref, x1_ref, hn_ref, route_ref):
    merged = (sg_ref[:, 0:D_MODEL].astype(F32) * _dot(oa_ref[...], wa_ref[...])
              + sg_ref[:, D_MODEL:2 * D_MODEL].astype(F32) * _dot(ob_ref[...], wb_ref[...])
              + sg_ref[:, 2 * D_MODEL:3 * D_MODEL].astype(F32) * _dot(oc_ref[...], wc_ref[...]))
    x1 = x_ref[...] + _dot(merged.astype(BF16), wo_ref[...])
    x1_ref[...] = x1
    ms = jnp.mean(x1 * x1, axis=-1, keepdims=True)
    hn = x1 * lax.rsqrt(ms + EPS) * g_ref[...]
    hn_ref[...] = hn

    hi, lo = _split_bf16(hn)
    lg = (_dot(hi, wr_hi_ref[...]) + _dot(lo, wr_hi_ref[...]) + _dot(hi, wr_lo_ref[...])
          + br_ref[...])
    lane = lax.broadcasted_iota(jnp.int32, lg.shape, 1)
    lanef = lane.astype(F32)
    far = float(LANES)
    is_g = (lane >= N_EXPERTS) & (lane < N_EXPERTS + N_GROUPS)
    gl = jnp.where(is_g, lg, -jnp.inf)
    gmax = jnp.max(gl, axis=-1, keepdims=True)
    gidx = jnp.min(jnp.where(gl == gmax, lanef, far), axis=-1, keepdims=True) - float(N_EXPERTS)
    g_w = 1.0 / jnp.sum(jnp.where(is_g, jnp.exp(gl - gmax), 0.0), axis=-1, keepdims=True)
    in_group = (lane < N_EXPERTS) & ((lane // EXPERTS_PER_GROUP).astype(F32) == gidx)
    el = jnp.where(in_group, lg, -jnp.inf)
    e1 = jnp.max(el, axis=-1, keepdims=True)
    i1 = jnp.min(jnp.where(el == e1, lanef, far), axis=-1, keepdims=True)
    el2 = jnp.where(lanef == i1, -jnp.inf, el)
    e2 = jnp.max(el2, axis=-1, keepdims=True)
    i2 = jnp.min(jnp.where(el2 == e2, lanef, far), axis=-1, keepdims=True)
    t = jnp.exp(e2 - e1)
    w1 = g_w / (1.0 + t)
    w2 = g_w * t / (1.0 + t)
    route_ref[...] = jnp.where(lane == 0, i1, jnp.where(lane == 1, i2, jnp.where(lane == 2, w1, jnp.where(lane == 3, w2, 0.0))))


def _merge(x2, oa, ob, oc, sg, wa, wb, wc, wo, norm_g, wr_hi, wr_lo, br):
    n = x2.shape[0]
    tm = min(DENSE_TILE, n)
    row = lambda w: pl.BlockSpec((tm, w), lambda i: (i, 0))
    const = lambda a: pl.BlockSpec(a.shape, lambda i: (0, 0))
    return pl.pallas_call(
        _merge_kernel,
        grid=(n // tm,),
        in_specs=[row(D_MODEL), row(HW), row(HW), row(N_HEADS * C_VDIM), row(3 * D_MODEL),
                  const(wa), const(wb), const(wc), const(wo), const(norm_g), const(wr_hi),
                  const(wr_lo), const(br)],
        out_specs=[row(D_MODEL), row(D_MODEL), row(LANES)],
        out_shape=[jax.ShapeDtypeStruct((n, D_MODEL), F32), jax.ShapeDtypeStruct((n, D_MODEL), F32),
                   jax.ShapeDtypeStruct((n, LANES), F32)],
        compiler_params=_params(1),
        name="merge",
    )(x2, oa, ob, oc, sg, wa, wb, wc, wo, norm_g, wr_hi, wr_lo, br)


MOE_BLOCK = 512
META_ROWS = 8
META_USED = 3 * LANES
META_END = 4 * LANES
META_PADDED = 5 * LANES


def _lane_prefix_sum(x):
    lane = lax.broadcasted_iota(jnp.int32, x.shape, 1)
    shift = 1
    while shift < LANES:
        x = x + jnp.where(lane >= shift, pltpu.roll(x, shift, 1), 0.0)
        shift *= 2
    return x


def _positions_kernel(route_ref, dest_ref, meta_ref, cnt_ref, base_ref):
    phase = pl.program_id(0)
    t = pl.program_id(1)
    tm = route_ref.shape[0]
    route = route_ref[...]
    lane = lax.broadcasted_iota(jnp.int32, route.shape, 1)
    lanef = lane.astype(F32)
    e1 = route[:, 0:1]
    e2 = route[:, 1:2]
    uses = jnp.where((lanef == e1) | (lanef == e2), 1.0, 0.0)
    tile_cnt = jnp.sum(uses, axis=0, keepdims=True)

    @pl.when((phase == 0) & (t == 0))
    def _():
        cnt_ref[...] = jnp.zeros(cnt_ref.shape, F32)

    @pl.when(phase == 0)
    def _():
        cnt_ref[...] += tile_cnt

    @pl.when((phase == 1) & (t == 0))
    def _():
        cnt = cnt_ref[...]
        padded = jnp.floor((cnt + (MOE_BLOCK - 1)) * (1.0 / MOE_BLOCK)) * MOE_BLOCK
        end = _lane_prefix_sum(padded)
        base_ref[...] = end - padded
        cnt_ref[...] = jnp.zeros(cnt_ref.shape, F32)
        lane1 = lax.broadcasted_iota(jnp.int32, (1, LANES), 1)
        row = lax.broadcasted_iota(jnp.int32, (META_ROWS, LANES), 0)
        col = lax.broadcasted_iota(jnp.int32, (META_ROWS, LANES), 1)
        first_row = ((row * LANES + col) * MOE_BLOCK).astype(F32)
        owner = jnp.zeros((META_ROWS, LANES), F32)
        for e in range(N_EXPERTS):
            end_e = jnp.sum(jnp.where(lane1 == e, end, 0.0), axis=-1, keepdims=True)
            owner = owner + jnp.where(end_e <= first_row, 1.0, 0.0)
        owner = jnp.minimum(owner, float(N_EXPERTS - 1))
        used = jnp.sum(jnp.where(lane1 == N_EXPERTS - 1, end, 0.0), axis=-1, keepdims=True) * (1.0 / MOE_BLOCK)
        meta = jnp.where(row == META_USED // LANES, used,
                         jnp.where(row == META_END // LANES, end,
                                   jnp.where(row == META_PADDED // LANES, padded, owner)))
        meta_ref[...] = meta.astype(jnp.int32)

    @pl.when(phase == 1)
    def _():
        before = (lax.broadcasted_iota(jnp.int32, (tm, tm), 1)
                  < lax.broadcasted_iota(jnp.int32, (tm, tm), 0)).astype(BF16)
        rank = _dot(before, uses.astype(BF16))
        pos = base_ref[...] + cnt_ref[...] + rank
        d1 = jnp.sum(jnp.where(lanef == e1, pos, 0.0), axis=-1, keepdims=True)
        d2 = jnp.sum(jnp.where(lanef == e2, pos, 0.0), axis=-1, keepdims=True)
        dest_ref[...] = jnp.where(lane == 0, d1, jnp.where(lane == 1, d2, 0.0)).astype(jnp.int32)
        cnt_ref[...] += tile_cnt


def _positions(route):
    n = route.shape[0]
    tm = min(4 * ROW_TILE, n)
    return pl.pallas_call(
        _positions_kernel,
        grid=(2, n // tm),
        in_specs=[pl.BlockSpec((tm, LANES), lambda p, t: (t, 0))],
        out_specs=[pl.BlockSpec((tm, LANES), lambda p, t: (t * p, 0)),
                   pl.BlockSpec((META_ROWS, LANES), lambda p, t: (0, 0))],
        out_shape=[jax.ShapeDtypeStruct((n, LANES), jnp.int32),
                   jax.ShapeDtypeStruct((META_ROWS, LANES), jnp.int32)],
        scratch_shapes=[pltpu.VMEM((1, LANES), F32), pltpu.VMEM((1, LANES), F32)],
        compiler_params=_params(2),
        name="moe_positions",
    )(route)


def _row_copy(src_ref, src_row, dst_ref, dst_row, sem):
    return pltpu.make_async_copy(src_ref.at[pl.ds(src_row, 1), :], dst_ref.at[pl.ds(dst_row, 1), :], sem)


def _dispatch_kernel(meta_ref, dest_ref, hn_ref, xs_ref, zero_ref, stage_ref, sem, in_sems, row_sems):
    t = pl.program_id(0)
    last = pl.num_programs(0) - 1
    tm = stage_ref.shape[1]

    def fetch(tile):
        return pltpu.make_async_copy(hn_ref.at[pl.ds(pl.multiple_of(tile * tm, tm), tm), :],
                                     stage_ref.at[tile % 3], in_sems.at[tile % 2])

    @pl.when(t == 0)
    def _():
        fetch(0).start()

    @pl.when(t < last)
    def _():
        fetch(t + 1).start()

    @pl.when(t == 0)
    def _():
        zero_ref[...] = jnp.zeros(zero_ref.shape, F32)

        def fill(e):
            end = pl.multiple_of(meta_ref[META_END + e], MOE_BLOCK)
            return pltpu.make_async_copy(zero_ref, xs_ref.at[pl.ds(end - MOE_BLOCK, MOE_BLOCK), :], sem)

        for e in range(N_EXPERTS):
            @pl.when(meta_ref[META_PADDED + e] > 0)
            def _():
                fill(e).start()
        for e in range(N_EXPERTS):
            @pl.when(meta_ref[META_PADDED + e] > 0)
            def _():
                fill(e).wait()

        def spare(b):
            return pltpu.make_async_copy(
                zero_ref, xs_ref.at[pl.ds(pl.multiple_of(b * MOE_BLOCK, MOE_BLOCK), MOE_BLOCK), :], sem)

        n_blocks = xs_ref.shape[0] // MOE_BLOCK
        lax.fori_loop(meta_ref[META_USED], n_blocks, lambda b, c: (spare(b).start(), c)[1], 0)
        lax.fori_loop(meta_ref[META_USED], n_blocks, lambda b, c: (spare(b).wait(), c)[1], 0)

    fetch(t).wait()
    rows_ref = stage_ref.at[t % 3]

    def start(r, carry):
        for k in range(2):
            _row_copy(rows_ref, r, xs_ref, dest_ref[0, 0, 2 * r + k], row_sems.at[t % 2]).start()
        return carry

    lax.fori_loop(0, tm, start, 0, unroll=8)

    def wait_rows(tile):
        for _ in range(2):
            pltpu.make_async_copy(stage_ref.at[tile % 3], xs_ref.at[pl.ds(0, tm), :], row_sems.at[tile % 2]).wait()

    @pl.when(t > 0)
    def _():
        wait_rows(t - 1)

    @pl.when(t == last)
    def _():
        wait_rows(t)


def _dispatch(meta, dest3, hn, n_rows):
    n = hn.shape[0]
    tm = dest3.shape[2] // 2
    return pl.pallas_call(
        _dispatch_kernel,
        grid_spec=pltpu.PrefetchScalarGridSpec(
            num_scalar_prefetch=1,
            grid=(n // tm,),
            in_specs=[pl.BlockSpec((1, 1, 2 * tm), lambda t, m: (t, 0, 0), memory_space=pltpu.SMEM),
                      pl.BlockSpec(memory_space=pl.ANY)],
            out_specs=pl.BlockSpec(memory_space=pl.ANY),
            scratch_shapes=[pltpu.VMEM((MOE_BLOCK, D_MODEL), F32), pltpu.VMEM((3, tm, D_MODEL), F32),
                            pltpu.SemaphoreType.DMA(()), pltpu.SemaphoreType.DMA((2,)),
                            pltpu.SemaphoreType.DMA((2,))]),
        out_shape=jax.ShapeDtypeStruct((n_rows, D_MODEL), F32),
        compiler_params=_params(1),
        name="moe_dispatch",
    )(meta, dest3, hn)


def _expert_kernel(meta_ref, xs_ref, wg_ref, wu_ref, wd_ref, y_ref, wg_bf, wu_bf, wd_bf):
    b = pl.program_id(0)
    holds_rows = b < meta_ref[META_USED]
    new_expert = (b == 0) | (meta_ref[b] != meta_ref[jnp.maximum(b - 1, 0)])

    @pl.when(holds_rows & new_expert)
    def _():
        wg_bf[...] = wg_ref[0].astype(BF16)
        wu_bf[...] = wu_ref[0].astype(BF16)
        wd_bf[...] = wd_ref[0].astype(BF16)

    @pl.when(holds_rows)
    def _():
        x = xs_ref[...].astype(BF16)
        g = _dot(x, wg_bf[...])
        u = _dot(x, wu_bf[...])
        hid = g * (1.0 / (1.0 + jnp.exp(-g))) * u
        y_ref[...] = _dot(hid.astype(BF16), wd_bf[...])

    @pl.when(jnp.logical_not(holds_rows))
    def _():
        y_ref[...] = jnp.zeros(y_ref.shape, F32)


def _experts(meta, xs, wg, wu, wd, layer):
    n_blocks = xs.shape[0] // MOE_BLOCK
    rows = pl.BlockSpec((MOE_BLOCK, D_MODEL), lambda b, m: (b, 0))
    weight = lambda shape: pl.BlockSpec(
        (None, 1) + shape, lambda b, m: (layer, m[jnp.minimum(b, m[META_USED] - 1)], 0, 0))
    return pl.pallas_call(
        _expert_kernel,
        grid_spec=pltpu.PrefetchScalarGridSpec(
            num_scalar_prefetch=1,
            grid=(n_blocks,),
            in_specs=[rows, weight((D_MODEL, D_EXPERT)), weight((D_MODEL, D_EXPERT)),
                      weight((D_EXPERT, D_MODEL))],
            out_specs=rows,
            scratch_shapes=[pltpu.VMEM((D_MODEL, D_EXPERT), BF16), pltpu.VMEM((D_MODEL, D_EXPERT), BF16),
                            pltpu.VMEM((D_EXPERT, D_MODEL), BF16)]),
        out_shape=jax.ShapeDtypeStruct(xs.shape, F32),
        compiler_params=_params(1),
        name="moe_experts",
    )(meta, xs, wg, wu, wd)


def _combine_kernel(dest_ref, next_dest_ref, x1_ref, route_ref, y_ref, o_ref, buf_ref, sems):
    t = pl.program_id(0)
    tm = x1_ref.shape[0]

    def gather(d_ref, parity):
        def start(g, carry):
            first = pl.multiple_of(g * 8, 8)
            for u in range(8):
                for k in range(2):
                    _row_copy(y_ref, d_ref[0, 0, 2 * (g * 8 + u) + k], buf_ref.at[parity, k], first + u,
                              sems.at[parity]).start()
            return carry
        lax.fori_loop(0, tm // 8, start, 0)

    @pl.when(t == 0)
    def _():
        gather(dest_ref, 0)

    @pl.when(t + 1 < pl.num_programs(0))
    def _():
        gather(next_dest_ref, (t + 1) % 2)

    for k in range(2):
        pltpu.make_async_copy(y_ref.at[pl.ds(0, tm), :], buf_ref.at[t % 2, k], sems.at[t % 2]).wait()
    route = route_ref[...]
    o_ref[...] = x1_ref[...] + route[:, 2:3] * buf_ref[t % 2, 0] + route[:, 3:4] * buf_ref[t % 2, 1]


def _combine(dest3, x1, route, y):
    n = x1.shape[0]
    tm = dest3.shape[2] // 2
    row = lambda w: pl.BlockSpec((tm, w), lambda t: (t, 0))
    n_tiles = n // tm
    slots = lambda index: pl.BlockSpec((1, 1, 2 * tm), index, memory_space=pltpu.SMEM)
    return pl.pallas_call(
        _combine_kernel,
        grid=(n_tiles,),
        in_specs=[slots(lambda t: (t, 0, 0)), slots(lambda t: (jnp.minimum(t + 1, n_tiles - 1), 0, 0)),
                  row(D_MODEL), row(LANES), pl.BlockSpec(memory_space=pl.ANY)],
        out_specs=row(D_MODEL),
        out_shape=jax.ShapeDtypeStruct((n, D_MODEL), F32),
        scratch_shapes=[pltpu.VMEM((2, 2, tm, D_MODEL), F32), pltpu.SemaphoreType.DMA((2,))],
        compiler_params=_params(1),
        name="moe_combine",
    )(dest3, dest3, x1, route, y)


def _moe(x1, hn, route, wg, wu, wd, layer):
    n = x1.shape[0]
    tm = min(DENSE_TILE, n)
    n_blocks = -(-(2 * n + N_EXPERTS * (MOE_BLOCK - 1)) // MOE_BLOCK)
    assert n_blocks <= META_USED
    dest, meta = _positions(route)
    meta = meta.reshape(-1)
    dest3 = dest[:, :2].reshape(n // tm, 1, 2 * tm)
    xs = _dispatch(meta, dest3, hn, n_blocks * MOE_BLOCK)
    y = _experts(meta, xs, wg, wu, wd, layer)
    return _combine(dest3, x1, route, y)


def _rope_tables(seq):
    inv_freq = 1.0 / (ROPE_THETA ** (jnp.arange(0, HEAD_DIM, 2, dtype=F32) / HEAD_DIM))
    ang = jnp.arange(seq, dtype=F32)[:, None] * inv_freq[None, :]
    cos, sin = jnp.cos(ang), jnp.sin(ang)
    cos_t = jnp.tile(jnp.concatenate([cos, cos], axis=-1), (1, N_HEADS))
    sin_t = jnp.tile(jnp.concatenate([-sin, sin], axis=-1), (1, N_HEADS))
    return cos_t, sin_t


def _to_t(v, batch, n_chunk):
    feat = v.shape[1]
    return (v.reshape(batch, n_chunk, ROW_TILE, feat).transpose(0, 1, 3, 2)
            .reshape(batch * n_chunk, feat, ROW_TILE))


def _from_t(o_t):
    b, feat, seq = o_t.shape
    return o_t.transpose(0, 2, 1).reshape(b * seq, feat)


def kernel(x, norm_attn, w_in, qk_gain, idx_k_gain, diff_lambda, diff_subln_gain, w_proj_a, w_proj_b, w_proj_c, w_out, norm_ffn, w_group, b_group, w_router, b_router, w_e_gate, w_e_up, w_e_down):
    batch, seq, d = x.shape
    assert d == D_MODEL and seq % (2 * ROW_TILE) == 0 and ROW_TILE == MOBA_BLOCK
    n = batch * seq
    nq = seq // ROW_TILE
    depth = w_in.shape[0]
    cos_t, sin_t = _rope_tables(seq)
    x2 = x.reshape(n, d)
    w_pad = jnp.concatenate(
        [w_in[:, :, :KW_SRC], jnp.zeros((depth, d, LANES - IDX_DIM - IDX_HEADS), F32), w_in[:, :, KW_SRC:]],
        axis=2).astype(BF16)
    for l in range(depth):
        gains = jnp.tile(qk_gain[l][jnp.array([0, 1, 2, 3, 4, 4, 5, 5])], (1, N_HEADS))
        kgain = jnp.pad(idx_k_gain[l], (0, LANES - IDX_DIM))[None, :]
        (qa, ka, va, qb, kb, vb, qi, ki, wi, q1, q2, k1, k2, vc, sg, kmean) = _project(
            x2, norm_attn[l][None, :], w_pad, l, cos_t, sin_t, gains, kgain, seq)

        feat_major = lambda t: t.reshape(batch, seq, t.shape[1]).transpose(0, 2, 1)
        o_a = _from_t(_moba(feat_major(qa), ka, _to_t(va, batch, nq), kmean.reshape(batch, nq, HW),
                            batch, seq))
        o_b = _from_t(_dsa(feat_major(qi), wi, ki, feat_major(qb), kb, _to_t(vb, batch, nq), batch, seq))
        lam_init = 0.8 - 0.6 * math.exp(-0.3 * l)
        o_c = _from_t(_diff(feat_major(q1), feat_major(q2), k1, k2, _to_t(vc, batch, nq), diff_lambda[l],
                            diff_subln_gain[l][:, None], lam_init, batch, seq))

        w_r = jnp.concatenate([w_router[l], w_group[l],
                               jnp.zeros((d, LANES - N_EXPERTS - N_GROUPS), F32)], axis=1)
        wr_hi = w_r.astype(BF16)
        wr_lo = (w_r - wr_hi.astype(F32)).astype(BF16)
        b_r = jnp.concatenate([b_router[l], b_group[l],
                               jnp.zeros((LANES - N_EXPERTS - N_GROUPS,), F32)])[None, :]
        x1, hn, route = _merge(x2, o_a, o_b, o_c, sg, w_proj_a[l].astype(BF16), w_proj_b[l].astype(BF16),
                            w_proj_c[l].astype(BF16), w_out[l].astype(BF16), norm_ffn[l][None, :],
                            wr_hi, wr_lo, b_r)
        x2 = _moe(x1, hn, route, w_e_gate, w_e_up, w_e_down, l)
    return x2.reshape(batch, seq, d)
```

```python
import functools
import math

import jax
import jax.numpy as jnp
from jax import lax
from jax.experimental import pallas as pl
from jax.experimental.pallas import tpu as pltpu

F32 = jnp.float32
BF16 = jnp.bfloat16

D_MODEL = 1024
HEAD_DIM = 64
ROPE_THETA = 10000.0
EPS = 1e-6
N_HEADS = 4
MOBA_BLOCK = 256
MOBA_TOPK = 3
IDX_HEADS = 8
IDX_DIM = 64
DSA_TOPK_MAX = 256
C_VDIM = 2 * HEAD_DIM
N_GROUPS = 4
EXPERTS_PER_GROUP = 8
N_EXPERTS = N_GROUPS * EXPERTS_PER_GROUP
D_EXPERT = 512

HW = N_HEADS * HEAD_DIM
LANES = 128
ROW_TILE = 256
DENSE_TILE = 512
VMEM_LIMIT = 56 * 1024 * 1024

_SEG = {}
_off = 0
for _name, _w in (("qa", HW), ("ka", HW), ("va", HW), ("qb", HW), ("kb", HW), ("vb", HW),
                  ("qi", IDX_HEADS * IDX_DIM), ("kw", LANES), ("q1", HW), ("q2", HW), ("k1", HW),
                  ("k2", HW), ("vc", N_HEADS * C_VDIM), ("ga", D_MODEL), ("gb", D_MODEL),
                  ("gc", D_MODEL)):
    _SEG[_name] = (_off, _w)
    _off += _w
D_IN_PAD = _off
KW_SRC = 6 * HW + IDX_HEADS * IDX_DIM + IDX_DIM + IDX_HEADS

NEG_BIG = -1e30
M_FLOOR = -1e20
INT_MIN = -(2 ** 31)
LOG2E = math.log2(math.e)
Q_SCALE = HEAD_DIM ** -0.5 * LOG2E
V_PAD = 16
NT_DIMS = (((1,), (1,)), ((), ()))


def _params(n_axes):
    return pltpu.CompilerParams(dimension_semantics=("arbitrary",) * n_axes,
                                vmem_limit_bytes=VMEM_LIMIT)


def _dot(a, b):
    return jnp.dot(a, b, preferred_element_type=F32)


def _dot_nt(a, b):
    return lax.dot_general(a, b, NT_DIMS, preferred_element_type=F32)


def _split_bf16(a):
    hi = a.astype(BF16)
    return hi, (a - hi.astype(F32)).astype(BF16)


def _swap_halves(y, width):
    lane = lax.broadcasted_iota(jnp.int32, y.shape, 1)
    first = (lane % HEAD_DIM) < (HEAD_DIM // 2)
    return jnp.where(first, pltpu.roll(y, width - HEAD_DIM // 2, 1), pltpu.roll(y, HEAD_DIM // 2, 1))


def _proj_kernel(x_ref, g_ref, w_ref, cos_ref, sin_ref, gain_ref, kgain_ref,
                 qa_ref, ka_ref, va_ref, qb_ref, kb_ref, vb_ref, qi_ref, ki_ref, wi_ref,
                 q1_ref, q2_ref, k1_ref, k2_ref, vc_ref, sg_ref, kmean_ref):
    x = x_ref[...]
    ms = jnp.mean(x * x, axis=-1, keepdims=True)
    h = (x * lax.rsqrt(ms + EPS) * g_ref[...]).astype(BF16)
    cos = cos_ref[...]
    sin = sin_ref[...]
    r = lax.broadcasted_iota(jnp.int32, (HW, HW), 0) // HEAD_DIM
    c = lax.broadcasted_iota(jnp.int32, (HW, HW), 1) // HEAD_DIM
    head_ones = (r == c).astype(BF16)

    def seg(name, lo=0, width=None):
        off, w = _SEG[name]
        width = w if width is None else width
        return _dot(h, w_ref[:, off + lo:off + lo + width])

    def rope(y):
        return y * cos + _swap_halves(y, HW) * sin

    def norm_rope(t, gain_row):
        hi, lo = _split_bf16(t * t)
        ss = _dot(hi, head_ones) + _dot(lo, head_ones)
        yn = t * lax.rsqrt(ss * (1.0 / HEAD_DIM) + EPS) * gain_ref[gain_row:gain_row + 1, :]
        return rope(yn)

    qa_ref[...] = (norm_rope(seg("qa"), 0) * Q_SCALE).astype(BF16)
    ka = norm_rope(seg("ka"), 1)
    ka_ref[...] = ka.astype(BF16)
    for blk in range(ka.shape[0] // MOBA_BLOCK):
        kmean_ref[blk] = jnp.mean(ka[blk * MOBA_BLOCK:(blk + 1) * MOBA_BLOCK], axis=0, keepdims=True)
    va_ref[...] = seg("va").astype(BF16)
    qb_ref[...] = (norm_rope(seg("qb"), 2) * Q_SCALE).astype(BF16)
    kb_ref[...] = norm_rope(seg("kb"), 3).astype(BF16)
    vb_ref[...] = seg("vb").astype(BF16)
    for half in range(2):
        qi_ref[:, half * HW:(half + 1) * HW] = rope(seg("qi", half * HW, HW)).astype(BF16)

    t = seg("kw")
    lane = lax.broadcasted_iota(jnp.int32, t.shape, 1)
    is_k = lane < IDX_DIM
    kms = jnp.sum(jnp.where(is_k, t * t, 0.0), axis=-1, keepdims=True) * (1.0 / IDX_DIM)
    kn = t * lax.rsqrt(kms + EPS) * kgain_ref[...]
    kr = kn * cos[:, :LANES] + _swap_halves(kn, LANES) * sin[:, :LANES]
    ki_ref[...] = kr[:, :IDX_DIM].astype(BF16)
    w_scale = (IDX_HEADS ** -0.5) * (IDX_DIM ** -0.5)
    wi_ref[...] = jnp.where(lane < IDX_HEADS, pltpu.roll(t, LANES - IDX_DIM, 1) * w_scale, 0.0)

    q1_ref[...] = (norm_rope(seg("q1"), 4) * Q_SCALE).astype(BF16)
    q2_ref[...] = (norm_rope(seg("q2"), 5) * Q_SCALE).astype(BF16)
    k1_ref[...] = norm_rope(seg("k1"), 6).astype(BF16)
    k2_ref[...] = norm_rope(seg("k2"), 7).astype(BF16)
    for half in range(2):
        vc_ref[:, half * HW:(half + 1) * HW] = seg("vc", half * HW, HW).astype(BF16)
    for gi, name in enumerate(("ga", "gb", "gc")):
        for part in range(D_MODEL // 512):
            g = seg(name, part * 512, 512)
            lo = gi * D_MODEL + part * 512
            sg_ref[:, lo:lo + 512] = (1.0 / (1.0 + jnp.exp(-g))).astype(BF16)


def _project(x2, norm_g, w_pad, layer, cos_t, sin_t, gains, kgain, seq):
    n = x2.shape[0]
    tm = min(DENSE_TILE, seq)
    assert seq % tm == 0 and tm % MOBA_BLOCK == 0
    n_pos = seq // tm
    row = lambda w: pl.BlockSpec((tm, w), lambda i: (i, 0))
    const = lambda shape: pl.BlockSpec(shape, lambda i: (0,) * len(shape))
    out_widths = [HW] * 6 + [IDX_HEADS * IDX_DIM, IDX_DIM, LANES] + [HW] * 4 + [N_HEADS * C_VDIM, 3 * D_MODEL]
    out_dtypes = [BF16] * 8 + [F32] + [BF16] * 6
    out_shape = [jax.ShapeDtypeStruct((n, w), dt) for w, dt in zip(out_widths, out_dtypes)]
    out_shape.append(jax.ShapeDtypeStruct((n // MOBA_BLOCK, 1, HW), F32))
    out_specs = [row(w) for w in out_widths] + [pl.BlockSpec((tm // MOBA_BLOCK, 1, HW), lambda i: (i, 0, 0))]
    return pl.pallas_call(
        _proj_kernel,
        grid=(n // tm,),
        in_specs=[row(D_MODEL), const((1, D_MODEL)),
                  pl.BlockSpec((None, D_MODEL, D_IN_PAD), lambda i: (layer, 0, 0), pipeline_mode=pl.Buffered(1)),
                  pl.BlockSpec((tm, HW), lambda i: (i % n_pos, 0)),
                  pl.BlockSpec((tm, HW), lambda i: (i % n_pos, 0)),
                  const((8, HW)), const((1, LANES))],
        out_specs=out_specs,
        out_shape=out_shape,
        compiler_params=_params(1),
        name="proj",
    )(x2, norm_g, w_pad, cos_t, sin_t, gains, kgain)


def _online_update(parts, ms, acc_ref):
    ps, out = [], []
    for c, tiles in enumerate(parts):
        m_new = ms[c]
        for s, _, ok in tiles:
            smax = jnp.max(s, axis=0, keepdims=True)
            m_new = jnp.maximum(m_new, smax if ok is None else jnp.where(ok, smax, NEG_BIG))
        m_eff = jnp.maximum(m_new, M_FLOOR)
        out.append(m_new)
        probs = [jnp.exp2(s - (m_eff if ok is None else jnp.where(ok, m_eff, -NEG_BIG))).astype(BF16)
                 for s, _, ok in tiles]
        ps.append((jnp.exp2(ms[c] - m_new), probs))
    for c, (alpha, probs) in enumerate(ps):
        acc = alpha * acc_ref[c]
        for (_, vt, _), p in zip(parts[c], probs):
            acc = acc + _dot(vt, p)
        acc_ref[c] = acc
    return out


def _init_max(n_chains, tq):
    return tuple(jnp.full((1, tq), NEG_BIG, F32) for _ in range(n_chains))


def _with_ones_row(vt):
    row = lax.broadcasted_iota(jnp.int32, (V_PAD, vt.shape[1]), 0)
    return jnp.concatenate([vt, jnp.where(row == 0, 1.0, 0.0).astype(vt.dtype)], axis=0)


def _normalized(acc_ref, c, dv):
    acc = acc_ref[c]
    return acc[:dv] / acc[dv:dv + 1]


def _head_slice(h, width=HEAD_DIM):
    return slice(h * width, (h + 1) * width)


MOBA_GROUP = 4


def _moba_kernel(qt_ref, k_ref, vt_ref, kmean_ref, o_ref, acc_ref, *, n_sel):
    i = pl.program_id(1)
    blk = MOBA_BLOCK
    nb = kmean_ref.shape[1]
    km = kmean_ref[0]
    brow = lax.broadcasted_iota(jnp.int32, (nb, blk), 0)
    browf = brow.astype(F32)
    causal = (lax.broadcasted_iota(jnp.int32, (blk, 1), 0)
              <= lax.broadcasted_iota(jnp.int32, (1, blk), 1))
    qts = [qt_ref[0, _head_slice(h), :] for h in range(N_HEADS)]
    sels = []
    for h in range(N_HEADS):
        km_hi, km_lo = _split_bf16(km[:, _head_slice(h)])
        gate = _dot(km_hi, qts[h]) + _dot(km_lo, qts[h])
        gate = jnp.where(brow < i, gate, -jnp.inf)
        sel = jnp.zeros((nb, blk), F32)
        for _ in range(n_sel):
            gm = jnp.max(gate, axis=0, keepdims=True)
            is_m = (gate == gm) & (gm > -jnp.inf)
            first = jnp.min(jnp.where(is_m, browf, float(nb)), axis=0, keepdims=True)
            pick = browf == first
            sel = jnp.where(pick, 1.0, sel)
            gate = jnp.where(pick, -jnp.inf, gate)
        sels.append(sel)
    acc_ref[...] = jnp.zeros(acc_ref.shape, F32)

    def tile(j, h, mask=None, seen=False):
        rows = pl.ds(pl.multiple_of(j * blk, blk), blk)
        s = _dot(k_ref[rows, _head_slice(h)], qts[h])
        if mask is not None:
            s = jnp.where(mask, s, NEG_BIG)
        ok = jnp.sum(jnp.where(brow == j, sels[h], 0.0), axis=0, keepdims=True) > 0.0 if seen else None
        return s, _with_ones_row(vt_ref[j, _head_slice(h), :]), ok

    def absorb(first, n_past, diag, ms):
        parts = [[tile(first + u, h, seen=True) for u in range(n_past)]
                 + ([tile(i, h, mask=causal)] if diag else []) for h in range(N_HEADS)]
        return tuple(_online_update(parts, ms, acc_ref))

    group = MOBA_GROUP
    ms = lax.fori_loop(0, i // group, lambda g, ms: absorb(g * group, group, False, ms),
                       _init_max(N_HEADS, blk))
    for rest in range(group):
        @pl.when(i % group == rest)
        def _():
            absorb(i - rest, rest, True, ms)

    for h in range(N_HEADS):
        o_ref[0, _head_slice(h), :] = _normalized(acc_ref, h, HEAD_DIM).astype(o_ref.dtype)


def _moba(qt, k, vt, kmean, batch, seq):
    blk = MOBA_BLOCK
    nb = seq // blk
    n_sel = min(MOBA_TOPK, nb - 1)
    tspec = pl.BlockSpec((1, HW, blk), lambda b, i: (b, 0, i))
    return pl.pallas_call(
        functools.partial(_moba_kernel, n_sel=n_sel),
        grid=(batch, nb),
        in_specs=[tspec,
                  pl.BlockSpec((seq, HW), lambda b, i: (b, 0)),
                  pl.BlockSpec((nb, vt.shape[1], blk), lambda b, i: (b, 0, 0)),
                  pl.BlockSpec((1, nb, HW), lambda b, i: (b, 0, 0))],
        out_specs=tspec,
        out_shape=jax.ShapeDtypeStruct((batch, HW, seq), BF16),
        scratch_shapes=[pltpu.VMEM((N_HEADS, HEAD_DIM + V_PAD, blk), F32)],
        compiler_params=_params(2),
        name="moba",
    )(qt, k, vt, kmean)


def _bit_planes(words):
    a = list(words)
    assert len(a) == 32
    mask, j = 0x0000FFFF, 16
    while j:
        k = 0
        while k < 32:
            t = (a[k] ^ (a[k + j] >> j)) & mask
            a[k] = a[k] ^ t
            a[k + j] = a[k + j] ^ (t << j)
            k = (k + j + 1) & ~j
        j >>= 1
        mask = (mask ^ (mask << j)) & 0xFFFFFFFF
    return a


DSA_GROUP = 4


def _dsa_kernel(qit_ref, wi_ref, ki_ref, qt_ref, k_ref, vt_ref, o_ref, plane_ref, sel_ref, acc_ref, *, n_keep):
    i = pl.program_id(1)
    blk = ROW_TILE
    n_chunk = i + 1
    n_planes, n_slots, sub = plane_ref.shape[:3]
    v_bits, s_bits = (n_planes - 1).bit_length(), (sub - 1).bit_length()
    idx_bits = v_bits + s_bits + (n_slots - 1).bit_length()

    @pl.when(i == 0)
    def _():
        plane_ref[:, 1:] = jnp.zeros((n_planes, n_slots - 1) + plane_ref.shape[2:], jnp.int32)

    w_t = wi_ref[...].T
    causal = (lax.broadcasted_iota(jnp.int32, (blk, 1), 0)
              <= lax.broadcasted_iota(jnp.int32, (1, blk), 1))

    def score_chunk(c, diag):
        kc = ki_ref[pl.ds(pl.multiple_of(c * blk, blk), blk), :]
        lgs = [_dot(kc, qit_ref[0, _head_slice(h, IDX_DIM), :]) for h in range(IDX_HEADS)]
        sc = jnp.zeros((blk, blk), F32)
        for h in range(IDX_HEADS):
            sc = sc + w_t[h:h + 1, :] * jnp.maximum(lgs[h], 0.0)
        sc = sc + 0.0
        bits = pltpu.bitcast(sc, jnp.int32)
        key = bits ^ ((bits >> 31) | INT_MIN)
        if diag:
            key = jnp.where(causal, key, 0)
        words = key.reshape(n_planes, sub, blk)
        planes = _bit_planes([words[v] for v in range(n_planes)])
        for b in range(n_planes):
            plane_ref[b, c] = planes[n_planes - 1 - b]

    def score_pair(c2, carry):
        score_chunk(2 * c2, False)
        score_chunk(2 * c2 + 1, False)
        return carry

    lax.fori_loop(0, i // 2, score_pair, 0)

    @pl.when(i % 2 == 1)
    def _():
        score_chunk(i - 1, False)
        score_chunk(i, True)

    @pl.when(i % 2 == 0)
    def _():
        score_chunk(i, True)

    keep = float(n_keep)

    def step(plane, eq, gt):
        cand = gt | (eq & plane)
        per_row = jnp.sum(lax.population_count(cand), axis=0)
        take = jnp.sum(per_row.astype(F32), axis=0, keepdims=True) >= keep
        return eq & jnp.where(take, plane, ~plane), jnp.where(take, gt, cand)

    def index_plane(t, shape):
        if t < s_bits:
            s = lax.broadcasted_iota(jnp.int32, shape, 1)
            return jnp.where(((s >> t) & 1) == 0, -1, 0)
        if t < s_bits + v_bits:
            word = sum(1 << j for j in range(n_planes) if (((n_planes - 1 - j) >> (t - s_bits)) & 1) == 0)
            return jnp.full(shape, word - (1 << 32) if word >= (1 << 31) else word, jnp.int32)
        slot = lax.broadcasted_iota(jnp.int32, shape, 0)
        return jnp.where(((slot >> (t - s_bits - v_bits)) & 1) == 0, -1, 0)

    def select(n_used, lo):
        shape = (n_used, sub, LANES)
        lanes = slice(lo, lo + LANES)

        def score_bit(t, carry):
            return step(plane_ref[n_planes - 1 - t, 0:n_used, :, lanes], *carry)

        eq, gt = lax.fori_loop(0, n_planes, score_bit,
                               (jnp.full(shape, -1, jnp.int32), jnp.zeros(shape, jnp.int32)))
        def break_ties(eq, gt):
            for t in reversed(range(idx_bits)):
                eq, gt = step(index_plane(t, shape), eq, gt)
            return eq, gt

        chosen = jnp.sum(jnp.sum(lax.population_count(eq | gt), axis=0).astype(F32), axis=0, keepdims=True)
        eq, gt = lax.cond(jnp.max(chosen) > keep, break_ties, lambda eq, gt: (eq, gt), eq, gt)
        slot = lax.broadcasted_iota(jnp.int32, shape, 0)
        s = lax.broadcasted_iota(jnp.int32, shape, 1)
        qoff = lo + lax.broadcasted_iota(jnp.int32, shape, 2)
        reach = jnp.left_shift(-1, (n_planes - 1) - ((qoff - s) >> s_bits))
        valid = jnp.where(slot < i, -1, jnp.where((slot == i) & (qoff >= s), reach, 0))
        return (eq | gt) & valid

    half_slots = n_slots // 2
    for n_used, wanted in ((half_slots, n_chunk <= half_slots), (n_slots, n_chunk > half_slots)):
        @pl.when(wanted)
        def _():
            for lo in range(0, blk, LANES):
                sel_ref[0:n_used, :, lo:lo + LANES] = select(n_used, lo)
                if n_used < n_slots:
                    sel_ref[n_used:, :, lo:lo + LANES] = jnp.zeros((n_slots - n_used, sub, LANES), jnp.int32)

    qts = [qt_ref[0, _head_slice(h), :] for h in range(N_HEADS)]
    acc_ref[...] = jnp.zeros(acc_ref.shape, F32)

    def absorb(first, count, ms):
        parts = [[] for _ in range(N_HEADS)]
        for u in range(count):
            c = first + u
            rows = pl.ds(pl.multiple_of(c * blk, blk), blk)
            w = sel_ref[c]
            allowed = jnp.concatenate([jnp.left_shift(w, v) for v in range(n_planes)], axis=0) < 0
            for h in range(N_HEADS):
                s = jnp.where(allowed, _dot(k_ref[rows, _head_slice(h)], qts[h]), NEG_BIG)
                parts[h].append((s, _with_ones_row(vt_ref[c, _head_slice(h), :]), None))
        return tuple(_online_update(parts, ms, acc_ref))

    group = DSA_GROUP
    ms = lax.fori_loop(0, n_chunk // group, lambda g, ms: absorb(g * group, group, ms),
                       _init_max(N_HEADS, blk))
    for rest in range(1, group):
        @pl.when(n_chunk % group == rest)
        def _():
            absorb(n_chunk - rest, rest, ms)

    for h in range(N_HEADS):
        o_ref[0, _head_slice(h), :] = _normalized(acc_ref, h, HEAD_DIM).astype(o_ref.dtype)


def _dsa(qit, wi, ki, qt, k, vt, batch, seq):
    blk = ROW_TILE
    nq = seq // blk
    n_keep = min(DSA_TOPK_MAX, seq // 4)
    n_planes = 32
    tspec = lambda w: pl.BlockSpec((1, w, blk), lambda b, i: (b, 0, i))
    full = lambda w: pl.BlockSpec((seq, w), lambda b, i: (b, 0))
    return pl.pallas_call(
        functools.partial(_dsa_kernel, n_keep=n_keep),
        grid=(batch, nq),
        in_specs=[tspec(IDX_HEADS * IDX_DIM), pl.BlockSpec((blk, LANES), lambda b, i: (b * nq + i, 0)),
                  full(IDX_DIM), tspec(HW), full(HW),
                  pl.BlockSpec((nq, vt.shape[1], blk), lambda b, i: (b, 0, 0))],
        out_specs=tspec(HW),
        out_shape=jax.ShapeDtypeStruct((batch, HW, seq), BF16),
        scratch_shapes=[pltpu.VMEM((n_planes, nq, blk // n_planes, blk), jnp.int32),
                        pltpu.VMEM((nq, blk // n_planes, blk), jnp.int32),
                        pltpu.VMEM((N_HEADS, HEAD_DIM + V_PAD, blk), F32)],
        compiler_params=_params(2),
        name="dsa",
    )(qit, wi, ki, qt, k, vt)


DIFF_GROUP = 4
DIFF_KEY_GROUP = 2


def _diff_kernel(q1t_ref, q2t_ref, k1_ref, k2_ref, vt_ref, dl_ref, gain_ref, o_ref, acc_ref, *, lam_init):
    i = pl.program_id(1)
    blk = ROW_TILE
    dl = dl_ref[...]
    lam = (jnp.exp(jnp.sum(dl[0:1] * dl[1:2], axis=-1, keepdims=True))
           - jnp.exp(jnp.sum(dl[2:3] * dl[3:4], axis=-1, keepdims=True)) + lam_init)
    causal = (lax.broadcasted_iota(jnp.int32, (blk, 1), 0)
              <= lax.broadcasted_iota(jnp.int32, (1, blk), 1))
    maps = ((q1t_ref, k1_ref), (q2t_ref, k2_ref))

    for h0 in range(0, N_HEADS, DIFF_GROUP):
        chains = [(h, mp) for h in range(h0, h0 + DIFF_GROUP) for mp in range(2)]
        qts = [maps[mp][0][0, _head_slice(h), :] for h, mp in chains]
        acc_ref[...] = jnp.zeros(acc_ref.shape, F32)

        def tile(j, c, mask=None):
            h, mp = chains[c]
            rows = pl.ds(pl.multiple_of(j * blk, blk), blk)
            s = _dot(maps[mp][1][rows, _head_slice(h)], qts[c])
            if mask is not None:
                s = jnp.where(mask, s, NEG_BIG)
            return s, _with_ones_row(vt_ref[j, _head_slice(h, C_VDIM), :]), None

        def absorb(first, n_past, diag, ms):
            parts = [[tile(first + u, c) for u in range(n_past)] + ([tile(i, c, causal)] if diag else [])
                     for c in range(len(chains))]
            return tuple(_online_update(parts, ms, acc_ref))

        group = DIFF_KEY_GROUP
        ms = lax.fori_loop(0, i // group, lambda g, ms: absorb(g * group, group, False, ms),
                           _init_max(len(chains), blk))
        for rest in range(group):
            @pl.when(i % group == rest)
            def _():
                absorb(i - rest, rest, True, ms)

        for g in range(DIFF_GROUP):
            h = h0 + g
            o = _normalized(acc_ref, 2 * g, C_VDIM) - lam * _normalized(acc_ref, 2 * g + 1, C_VDIM)
            ms = jnp.mean(o * o, axis=0, keepdims=True)
            o = o * lax.rsqrt(ms + EPS) * gain_ref[...] * (1.0 - lam_init)
            o_ref[0, _head_slice(h, C_VDIM), :] = o.astype(o_ref.dtype)


def _diff(q1t, q2t, k1, k2, vt, dl, gain, lam_init, batch, seq):
    blk = ROW_TILE
    nq = seq // blk
    tspec = lambda w: pl.BlockSpec((1, w, blk), lambda b, i: (b, 0, i))
    kspec = pl.BlockSpec((seq, HW), lambda b, i: (b, 0))
    vw = N_HEADS * C_VDIM
    return pl.pallas_call(
        functools.partial(_diff_kernel, lam_init=lam_init),
        grid=(batch, nq),
        in_specs=[tspec(HW), tspec(HW), kspec, kspec,
                  pl.BlockSpec((nq, vt.shape[1], blk), lambda b, i: (b, 0, 0)),
                  pl.BlockSpec((4, HEAD_DIM), lambda b, i: (0, 0)),
                  pl.BlockSpec((C_VDIM, 1), lambda b, i: (0, 0))],
        out_specs=tspec(vw),
        out_shape=jax.ShapeDtypeStruct((batch, vw, seq), BF16),
        scratch_shapes=[pltpu.VMEM((2 * DIFF_GROUP, C_VDIM + V_PAD, blk), F32)],
        compiler_params=_params(2),
        name="diff",
    )(q1t, q2t, k1, k2, vt, dl, gain)


def _merge_kernel(x_ref, oa_ref, ob_ref, oc_ref, sg_ref, wa_ref, wb_ref, wc_ref, wo_ref, g_ref,
                  wr_hi_ref, wr_lo_ref, br_ref, x1_ref, hn_ref, route_ref):
    merged = (sg_ref[:, 0:D_MODEL].astype(F32) * _dot(oa_ref[...], wa_ref[...])
              + sg_ref[:, D_MODEL:2 * D_MODEL].astype(F32) * _dot(ob_ref[...], wb_ref[...])
              + sg_ref[:, 2 * D_MODEL:3 * D_MODEL].astype(F32) * _dot(oc_ref[...], wc_ref[...]))
    x1 = x_ref[...] + _dot(merged.astype(BF16), wo_ref[...])
    x1_ref[...] = x1
    ms = jnp.mean(x1 * x1, axis=-1, keepdims=True)
    hn = x1 * lax.rsqrt(ms + EPS) * g_ref[...]
    hn_ref[...] = hn

    hi, lo = _split_bf16(hn)
    lg = (_dot(hi, wr_hi_ref[...]) + _dot(lo, wr_hi_ref[...]) + _dot(hi, wr_lo_ref[...])
          + br_ref[...])
    lane = lax.broadcasted_iota(jnp.int32, lg.shape, 1)
    lanef = lane.astype(F32)
    far = float(LANES)
    is_g = (lane >= N_EXPERTS) & (lane < N_EXPERTS + N_GROUPS)
    gl = jnp.where(is_g, lg, -jnp.inf)
    gmax = jnp.max(gl, axis=-1, keepdims=True)
    gidx = jnp.min(jnp.where(gl == gmax, lanef, far), axis=-1, keepdims=True) - float(N_EXPERTS)
    g_w = 1.0 / jnp.sum(jnp.where(is_g, jnp.exp(gl - gmax), 0.0), axis=-1, keepdims=True)
    in_group = (lane < N_EXPERTS) & ((lane // EXPERTS_PER_GROUP).astype(F32) == gidx)
    el = jnp.where(in_group, lg, -jnp.inf)
    e1 = jnp.max(el, axis=-1, keepdims=True)
    i1 = jnp.min(jnp.where(el == e1, lanef, far), axis=-1, keepdims=True)
    el2 = jnp.where(lanef == i1, -jnp.inf, el)
    e2 = jnp.max(el2, axis=-1, keepdims=True)
    i2 = jnp.min(jnp.where(el2 == e2, lanef, far), axis=-1, keepdims=True)
    t = jnp.exp(e2 - e1)
    w1 = g_w / (1.0 + t)
    w2 = g_w * t / (1.0 + t)
    route_ref[...] = jnp.where(lane == 0, i1, jnp.where(lane == 1, i2, jnp.where(lane == 2, w1, jnp.where(lane == 3, w2, 0.0))))


def _merge(x2, oa, ob, oc, sg, wa, wb, wc, wo, norm_g, wr_hi, wr_lo, br):
    n = x2.shape[0]
    tm = min(DENSE_TILE, n)
    row = lambda w: pl.BlockSpec((tm, w), lambda i: (i, 0))
    const = lambda a: pl.BlockSpec(a.shape, lambda i: (0, 0))
    return pl.pallas_call(
        _merge_kernel,
        grid=(n // tm,),
        in_specs=[row(D_MODEL), row(HW), row(HW), row(N_HEADS * C_VDIM), row(3 * D_MODEL),
                  const(wa), const(wb), const(wc), const(wo), const(norm_g), const(wr_hi),
                  const(wr_lo), const(br)],
        out_specs=[row(D_MODEL), row(D_MODEL), row(LANES)],
        out_shape=[jax.ShapeDtypeStruct((n, D_MODEL), F32), jax.ShapeDtypeStruct((n, D_MODEL), F32),
                   jax.ShapeDtypeStruct((n, LANES), F32)],
        compiler_params=_params(1),
        name="merge",
    )(x2, oa, ob, oc, sg, wa, wb, wc, wo, norm_g, wr_hi, wr_lo, br)


MOE_BLOCK = 512
META_ROWS = 8
META_USED = 3 * LANES
META_END = 4 * LANES
META_PADDED = 5 * LANES


def _lane_prefix_sum(x):
    lane = lax.broadcasted_iota(jnp.int32, x.shape, 1)
    shift = 1
    while shift < LANES:
        x = x + jnp.where(lane >= shift, pltpu.roll(x, shift, 1), 0.0)
        shift *= 2
    return x


def _positions_kernel(route_ref, dest_ref, meta_ref, cnt_ref, base_ref):
    phase = pl.program_id(0)
    t = pl.program_id(1)
    tm = route_ref.shape[0]
    route = route_ref[...]
    lane = lax.broadcasted_iota(jnp.int32, route.shape, 1)
    lanef = lane.astype(F32)
    e1 = route[:, 0:1]
    e2 = route[:, 1:2]
    uses = jnp.where((lanef == e1) | (lanef == e2), 1.0, 0.0)
    tile_cnt = jnp.sum(uses, axis=0, keepdims=True)

    @pl.when((phase == 0) & (t == 0))
    def _():
        cnt_ref[...] = jnp.zeros(cnt_ref.shape, F32)

    @pl.when(phase == 0)
    def _():
        cnt_ref[...] += tile_cnt

    @pl.when((phase == 1) & (t == 0))
    def _():
        cnt = cnt_ref[...]
        padded = jnp.floor((cnt + (MOE_BLOCK - 1)) * (1.0 / MOE_BLOCK)) * MOE_BLOCK
        end = _lane_prefix_sum(padded)
        base_ref[...] = end - padded
        cnt_ref[...] = jnp.zeros(cnt_ref.shape, F32)
        lane1 = lax.broadcasted_iota(jnp.int32, (1, LANES), 1)
        row = lax.broadcasted_iota(jnp.int32, (META_ROWS, LANES), 0)
        col = lax.broadcasted_iota(jnp.int32, (META_ROWS, LANES), 1)
        first_row = ((row * LANES + col) * MOE_BLOCK).astype(F32)
        owner = jnp.zeros((META_ROWS, LANES), F32)
        for e in range(N_EXPERTS):
            end_e = jnp.sum(jnp.where(lane1 == e, end, 0.0), axis=-1, keepdims=True)
            owner = owner + jnp.where(end_e <= first_row, 1.0, 0.0)
        owner = jnp.minimum(owner, float(N_EXPERTS - 1))
        used = jnp.sum(jnp.where(lane1 == N_EXPERTS - 1, end, 0.0), axis=-1, keepdims=True) * (1.0 / MOE_BLOCK)
        meta = jnp.where(row == META_USED // LANES, used,
                         jnp.where(row == META_END // LANES, end,
                                   jnp.where(row == META_PADDED // LANES, padded, owner)))
        meta_ref[...] = meta.astype(jnp.int32)

    @pl.when(phase == 1)
    def _():
        before = (lax.broadcasted_iota(jnp.int32, (tm, tm), 1)
                  < lax.broadcasted_iota(jnp.int32, (tm, tm), 0)).astype(BF16)
        rank = _dot(before, uses.astype(BF16))
        pos = base_ref[...] + cnt_ref[...] + rank
        d1 = jnp.sum(jnp.where(lanef == e1, pos, 0.0), axis=-1, keepdims=True)
        d2 = jnp.sum(jnp.where(lanef == e2, pos, 0.0), axis=-1, keepdims=True)
        dest_ref[...] = jnp.where(lane == 0, d1, jnp.where(lane == 1, d2, 0.0)).astype(jnp.int32)
        cnt_ref[...] += tile_cnt


def _positions(route):
    n = route.shape[0]
    tm = min(4 * ROW_TILE, n)
    return pl.pallas_call(
        _positions_kernel,
        grid=(2, n // tm),
        in_specs=[pl.BlockSpec((tm, LANES), lambda p, t: (t, 0))],
        out_specs=[pl.BlockSpec((tm, LANES), lambda p, t: (t * p, 0)),
                   pl.BlockSpec((META_ROWS, LANES), lambda p, t: (0, 0))],
        out_shape=[jax.ShapeDtypeStruct((n, LANES), jnp.int32),
                   jax.ShapeDtypeStruct((META_ROWS, LANES), jnp.int32)],
        scratch_shapes=[pltpu.VMEM((1, LANES), F32), pltpu.VMEM((1, LANES), F32)],
        compiler_params=_params(2),
        name="moe_positions",
    )(route)


def _row_copy(src_ref, src_row, dst_ref, dst_row, sem):
    return pltpu.make_async_copy(src_ref.at[pl.ds(src_row, 1), :], dst_ref.at[pl.ds(dst_row, 1), :], sem)


def _dispatch_kernel(meta_ref, dest_ref, hn_ref, xs_ref, zero_ref, stage_ref, sem, in_sems, row_sems):
    t = pl.program_id(0)
    last = pl.num_programs(0) - 1
    tm = stage_ref.shape[1]

    def fetch(tile):
        return pltpu.make_async_copy(hn_ref.at[pl.ds(pl.multiple_of(tile * tm, tm), tm), :],
                                     stage_ref.at[tile % 3], in_sems.at[tile % 2])

    @pl.when(t == 0)
    def _():
        fetch(0).start()

    @pl.when(t < last)
    def _():
        fetch(t + 1).start()

    @pl.when(t == 0)
    def _():
        zero_ref[...] = jnp.zeros(zero_ref.shape, F32)

        def fill(e):
            end = pl.multiple_of(meta_ref[META_END + e], MOE_BLOCK)
            return pltpu.make_async_copy(zero_ref, xs_ref.at[pl.ds(end - MOE_BLOCK, MOE_BLOCK), :], sem)

        for e in range(N_EXPERTS):
            @pl.when(meta_ref[META_PADDED + e] > 0)
            def _():
                fill(e).start()
        for e in range(N_EXPERTS):
            @pl.when(meta_ref[META_PADDED + e] > 0)
            def _():
                fill(e).wait()

        def spare(b):
            return pltpu.make_async_copy(
                zero_ref, xs_ref.at[pl.ds(pl.multiple_of(b * MOE_BLOCK, MOE_BLOCK), MOE_BLOCK), :], sem)

        n_blocks = xs_ref.shape[0] // MOE_BLOCK
        lax.fori_loop(meta_ref[META_USED], n_blocks, lambda b, c: (spare(b).start(), c)[1], 0)
        lax.fori_loop(meta_ref[META_USED], n_blocks, lambda b, c: (spare(b).wait(), c)[1], 0)

    fetch(t).wait()
    rows_ref = stage_ref.at[t % 3]

    def start(r, carry):
        for k in range(2):
            _row_copy(rows_ref, r, xs_ref, dest_ref[0, 0, 2 * r + k], row_sems.at[t % 2]).start()
        return carry

    lax.fori_loop(0, tm, start, 0, unroll=8)

    def wait_rows(tile):
        for _ in range(2):
            pltpu.make_async_copy(stage_ref.at[tile % 3], xs_ref.at[pl.ds(0, tm), :], row_sems.at[tile % 2]).wait()

    @pl.when(t > 0)
    def _():
        wait_rows(t - 1)

    @pl.when(t == last)
    def _():
        wait_rows(t)


def _dispatch(meta, dest3, hn, n_rows):
    n = hn.shape[0]
    tm = dest3.shape[2] // 2
    return pl.pallas_call(
        _dispatch_kernel,
        grid_spec=pltpu.PrefetchScalarGridSpec(
            num_scalar_prefetch=1,
            grid=(n // tm,),
            in_specs=[pl.BlockSpec((1, 1, 2 * tm), lambda t, m: (t, 0, 0), memory_space=pltpu.SMEM),
                      pl.BlockSpec(memory_space=pl.ANY)],
            out_specs=pl.BlockSpec(memory_space=pl.ANY),
            scratch_shapes=[pltpu.VMEM((MOE_BLOCK, D_MODEL), F32), pltpu.VMEM((3, tm, D_MODEL), F32),
                            pltpu.SemaphoreType.DMA(()), pltpu.SemaphoreType.DMA((2,)),
                            pltpu.SemaphoreType.DMA((2,))]),
        out_shape=jax.ShapeDtypeStruct((n_rows, D_MODEL), F32),
        compiler_params=_params(1),
        name="moe_dispatch",
    )(meta, dest3, hn)


def _expert_kernel(meta_ref, xs_ref, wg_ref, wu_ref, wd_ref, y_ref, wg_bf, wu_bf, wd_bf):
    b = pl.program_id(0)
    holds_rows = b < meta_ref[META_USED]
    new_expert = (b == 0) | (meta_ref[b] != meta_ref[jnp.maximum(b - 1, 0)])

    @pl.when(holds_rows & new_expert)
    def _():
        wg_bf[...] = wg_ref[0].astype(BF16)
        wu_bf[...] = wu_ref[0].astype(BF16)
        wd_bf[...] = wd_ref[0].astype(BF16)

    @pl.when(holds_rows)
    def _():
        x = xs_ref[...].astype(BF16)
        g = _dot(x, wg_bf[...])
        u = _dot(x, wu_bf[...])
        hid = g * (1.0 / (1.0 + jnp.exp(-g))) * u
        y_ref[...] = _dot(hid.astype(BF16), wd_bf[...])

    @pl.when(jnp.logical_not(holds_rows))
    def _():
        y_ref[...] = jnp.zeros(y_ref.shape, F32)


def _experts(meta, xs, wg, wu, wd, layer):
    n_blocks = xs.shape[0] // MOE_BLOCK
    rows = pl.BlockSpec((MOE_BLOCK, D_MODEL), lambda b, m: (b, 0))
    weight = lambda shape: pl.BlockSpec(
        (None, 1) + shape, lambda b, m: (layer, m[jnp.minimum(b, m[META_USED] - 1)], 0, 0))
    return pl.pallas_call(
        _expert_kernel,
        grid_spec=pltpu.PrefetchScalarGridSpec(
            num_scalar_prefetch=1,
            grid=(n_blocks,),
            in_specs=[rows, weight((D_MODEL, D_EXPERT)), weight((D_MODEL, D_EXPERT)),
                      weight((D_EXPERT, D_MODEL))],
            out_specs=rows,
            scratch_shapes=[pltpu.VMEM((D_MODEL, D_EXPERT), BF16), pltpu.VMEM((D_MODEL, D_EXPERT), BF16),
                            pltpu.VMEM((D_EXPERT, D_MODEL), BF16)]),
        out_shape=jax.ShapeDtypeStruct(xs.shape, F32),
        compiler_params=_params(1),
        name="moe_experts",
    )(meta, xs, wg, wu, wd)


def _combine_kernel(dest_ref, next_dest_ref, x1_ref, route_ref, y_ref, o_ref, buf_ref, sems):
    t = pl.program_id(0)
    tm = x1_ref.shape[0]

    def gather(d_ref, parity):
        def start(g, carry):
            first = pl.multiple_of(g * 8, 8)
            for u in range(8):
                for k in range(2):
                    _row_copy(y_ref, d_ref[0, 0, 2 * (g * 8 + u) + k], buf_ref.at[parity, k], first + u,
                              sems.at[parity]).start()
            return carry
        lax.fori_loop(0, tm // 8, start, 0)

    @pl.when(t == 0)
    def _():
        gather(dest_ref, 0)

    @pl.when(t + 1 < pl.num_programs(0))
    def _():
        gather(next_dest_ref, (t + 1) % 2)

    for k in range(2):
        pltpu.make_async_copy(y_ref.at[pl.ds(0, tm), :], buf_ref.at[t % 2, k], sems.at[t % 2]).wait()
    route = route_ref[...]
    o_ref[...] = x1_ref[...] + route[:, 2:3] * buf_ref[t % 2, 0] + route[:, 3:4] * buf_ref[t % 2, 1]


def _combine(dest3, x1, route, y):
    n = x1.shape[0]
    tm = dest3.shape[2] // 2
    row = lambda w: pl.BlockSpec((tm, w), lambda t: (t, 0))
    n_tiles = n // tm
    slots = lambda index: pl.BlockSpec((1, 1, 2 * tm), index, memory_space=pltpu.SMEM)
    return pl.pallas_call(
        _combine_kernel,
        grid=(n_tiles,),
        in_specs=[slots(lambda t: (t, 0, 0)), slots(lambda t: (jnp.minimum(t + 1, n_tiles - 1), 0, 0)),
                  row(D_MODEL), row(LANES), pl.BlockSpec(memory_space=pl.ANY)],
        out_specs=row(D_MODEL),
        out_shape=jax.ShapeDtypeStruct((n, D_MODEL), F32),
        scratch_shapes=[pltpu.VMEM((2, 2, tm, D_MODEL), F32), pltpu.SemaphoreType.DMA((2,))],
        compiler_params=_params(1),
        name="moe_combine",
    )(dest3, dest3, x1, route, y)


def _moe(x1, hn, route, wg, wu, wd, layer):
    n = x1.shape[0]
    tm = min(DENSE_TILE, n)
    n_blocks = -(-(2 * n + N_EXPERTS * (MOE_BLOCK - 1)) // MOE_BLOCK)
    assert n_blocks <= META_USED
    dest, meta = _positions(route)
    meta = meta.reshape(-1)
    dest3 = dest[:, :2].reshape(n // tm, 1, 2 * tm)
    xs = _dispatch(meta, dest3, hn, n_blocks * MOE_BLOCK)
    y = _experts(meta, xs, wg, wu, wd, layer)
    return _combine(dest3, x1, route, y)


def _rope_tables(seq):
    inv_freq = 1.0 / (ROPE_THETA ** (jnp.arange(0, HEAD_DIM, 2, dtype=F32) / HEAD_DIM))
    ang = jnp.arange(seq, dtype=F32)[:, None] * inv_freq[None, :]
    cos, sin = jnp.cos(ang), jnp.sin(ang)
    cos_t = jnp.tile(jnp.concatenate([cos, cos], axis=-1), (1, N_HEADS))
    sin_t = jnp.tile(jnp.concatenate([-sin, sin], axis=-1), (1, N_HEADS))
    return cos_t, sin_t


def _to_t(v, batch, n_chunk):
    feat = v.shape[1]
    return (v.reshape(batch, n_chunk, ROW_TILE, feat).transpose(0, 1, 3, 2)
            .reshape(batch * n_chunk, feat, ROW_TILE))


def _from_t(o_t):
    b, feat, seq = o_t.shape
    return o_t.transpose(0, 2, 1).reshape(b * seq, feat)


def kernel(x, norm_attn, w_in, qk_gain, idx_k_gain, diff_lambda, diff_subln_gain, w_proj_a, w_proj_b, w_proj_c, w_out, norm_ffn, w_group, b_group, w_router, b_router, w_e_gate, w_e_up, w_e_down):
    batch, seq, d = x.shape
    assert d == D_MODEL and seq % (2 * ROW_TILE) == 0 and ROW_TILE == MOBA_BLOCK
    n = batch * seq
    nq = seq // ROW_TILE
    depth = w_in.shape[0]
    cos_t, sin_t = _rope_tables(seq)
    x2 = x.reshape(n, d)
    w_pad = jnp.concatenate(
        [w_in[:, :, :KW_SRC], jnp.zeros((depth, d, LANES - IDX_DIM - IDX_HEADS), F32), w_in[:, :, KW_SRC:]],
        axis=2).astype(BF16)
    for l in range(depth):
        gains = jnp.tile(qk_gain[l][jnp.array([0, 1, 2, 3, 4, 4, 5, 5])], (1, N_HEADS))
        kgain = jnp.pad(idx_k_gain[l], (0, LANES - IDX_DIM))[None, :]
        (qa, ka, va, qb, kb, vb, qi, ki, wi, q1, q2, k1, k2, vc, sg, kmean) = _project(
            x2, norm_attn[l][None, :], w_pad, l, cos_t, sin_t, gains, kgain, seq)

        feat_major = lambda t: t.reshape(batch, seq, t.shape[1]).transpose(0, 2, 1)
        o_a = _from_t(_moba(feat_major(qa), ka, _to_t(va, batch, nq), kmean.reshape(batch, nq, HW),
                            batch, seq))
        o_b = _from_t(_dsa(feat_major(qi), wi, ki, feat_major(qb), kb, _to_t(vb, batch, nq), batch, seq))
        lam_init = 0.8 - 0.6 * math.exp(-0.3 * l)
        o_c = _from_t(_diff(feat_major(q1), feat_major(q2), k1, k2, _to_t(vc, batch, nq), diff_lambda[l],
                            diff_subln_gain[l][:, None], lam_init, batch, seq))

        w_r = jnp.concatenate([w_router[l], w_group[l],
                               jnp.zeros((d, LANES - N_EXPERTS - N_GROUPS), F32)], axis=1)
        wr_hi = w_r.astype(BF16)
        wr_lo = (w_r - wr_hi.astype(F32)).astype(BF16)
        b_r = jnp.concatenate([b_router[l], b_group[l],
                               jnp.zeros((LANES - N_EXPERTS - N_GROUPS,), F32)])[None, :]
        x1, hn, route = _merge(x2, o_a, o_b, o_c, sg, w_proj_a[l].astype(BF16), w_proj_b[l].astype(BF16),
                            w_proj_c[l].astype(BF16), w_out[l].astype(BF16), norm_ffn[l][None, :],
                            wr_hi, wr_lo, b_r)
        x2 = _moe(x1, hn, route, w_e_gate, w_e_up, w_e_down, l)
    return x2.reshape(batch, seq, d)
```

```python
import functools
import math

import jax
import jax.numpy as jnp
from jax import lax
from jax.experimental import pallas as pl
from jax.experimental.pallas import tpu as pltpu

F32 = jnp.float32
BF16 = jnp.bfloat16

D_MODEL = 1024
HEAD_DIM = 64
ROPE_THETA = 10000.0
EPS = 1e-6
N_HEADS = 4
MOBA_BLOCK = 256
MOBA_TOPK = 3
IDX_HEADS = 8
IDX_DIM = 64
DSA_TOPK_MAX = 256
C_VDIM = 2 * HEAD_DIM
N_GROUPS = 4
EXPERTS_PER_GROUP = 8
N_EXPERTS = N_GROUPS * EXPERTS_PER_GROUP
D_EXPERT = 512

HW = N_HEADS * HEAD_DIM
LANES = 128
SUBLANES = 8
ROW_TILE = 256
DENSE_TILE = 512
VMEM_LIMIT = 56 * 1024 * 1024

_SEG = {}
_off = 0
for _name, _w in (("qa", HW), ("ka", HW), ("va", HW), ("qb", HW), ("kb", HW), ("vb", HW),
                  ("qi", IDX_HEADS * IDX_DIM), ("kw", LANES), ("q1", HW), ("q2", HW), ("k1", HW),
                  ("k2", HW), ("vc", N_HEADS * C_VDIM), ("ga", D_MODEL), ("gb", D_MODEL),
                  ("gc", D_MODEL)):
    _SEG[_name] = (_off, _w)
    _off += _w
D_IN_PAD = _off
KW_SRC = 6 * HW + IDX_HEADS * IDX_DIM + IDX_DIM + IDX_HEADS

NEG_BIG = -1e30
M_FLOOR = -1e20
INT_MIN = -(2 ** 31)
LOG2E = math.log2(math.e)
Q_SCALE = HEAD_DIM ** -0.5 * LOG2E
V_PAD = 16


def _params(n_axes):
    return pltpu.CompilerParams(dimension_semantics=("arbitrary",) * n_axes,
                                vmem_limit_bytes=VMEM_LIMIT)


def _dot(a, b):
    return jnp.dot(a, b, preferred_element_type=F32)


def _split_bf16(a):
    hi = a.astype(BF16)
    return hi, (a - hi.astype(F32)).astype(BF16)


def _swap_halves(y, width):
    lane = lax.broadcasted_iota(jnp.int32, y.shape, 1)
    first = (lane % HEAD_DIM) < (HEAD_DIM // 2)
    return jnp.where(first, pltpu.roll(y, width - HEAD_DIM // 2, 1), pltpu.roll(y, HEAD_DIM // 2, 1))


def _proj_kernel(x_ref, g_ref, w_ref, cos_ref, sin_ref, gain_ref, kgain_ref,
                 qa_ref, ka_ref, va_ref, qb_ref, kb_ref, vb_ref, qi_ref, ki_ref, wi_ref,
                 q1_ref, q2_ref, k1_ref, k2_ref, vc_ref, sg_ref, kmean_ref):
    x = x_ref[...]
    ms = jnp.mean(x * x, axis=-1, keepdims=True)
    h = (x * lax.rsqrt(ms + EPS) * g_ref[...]).astype(BF16)
    cos = cos_ref[...]
    sin = sin_ref[...]
    r = lax.broadcasted_iota(jnp.int32, (HW, HW), 0) // HEAD_DIM
    c = lax.broadcasted_iota(jnp.int32, (HW, HW), 1) // HEAD_DIM
    head_ones = (r == c).astype(BF16)

    def seg(name, lo=0, width=None):
        off, w = _SEG[name]
        width = w if width is None else width
        return _dot(h, w_ref[:, off + lo:off + lo + width])

    def rope(y):
        return y * cos + _swap_halves(y, HW) * sin

    def norm_rope(t, gain_row):
        hi, lo = _split_bf16(t * t)
        ss = _dot(hi, head_ones) + _dot(lo, head_ones)
        yn = t * lax.rsqrt(ss * (1.0 / HEAD_DIM) + EPS) * gain_ref[gain_row:gain_row + 1, :]
        return rope(yn)

    qa_ref[...] = (norm_rope(seg("qa"), 0) * Q_SCALE).astype(BF16)
    ka = norm_rope(seg("ka"), 1)
    ka_ref[...] = ka.astype(BF16)
    for blk in range(ka.shape[0] // MOBA_BLOCK):
        kmean_ref[blk] = jnp.mean(ka[blk * MOBA_BLOCK:(blk + 1) * MOBA_BLOCK], axis=0, keepdims=True)
    va_ref[...] = seg("va").astype(BF16)
    qb_ref[...] = (norm_rope(seg("qb"), 2) * Q_SCALE).astype(BF16)
    kb_ref[...] = norm_rope(seg("kb"), 3).astype(BF16)
    vb_ref[...] = seg("vb").astype(BF16)
    for half in range(2):
        qi_ref[:, half * HW:(half + 1) * HW] = rope(seg("qi", half * HW, HW)).astype(BF16)

    t = seg("kw")
    lane = lax.broadcasted_iota(jnp.int32, t.shape, 1)
    is_k = lane < IDX_DIM
    kms = jnp.sum(jnp.where(is_k, t * t, 0.0), axis=-1, keepdims=True) * (1.0 / IDX_DIM)
    kn = t * lax.rsqrt(kms + EPS) * kgain_ref[...]
    kr = kn * cos[:, :LANES] + _swap_halves(kn, LANES) * sin[:, :LANES]
    ki_ref[...] = kr[:, :IDX_DIM].astype(BF16)
    w_scale = (IDX_HEADS ** -0.5) * (IDX_DIM ** -0.5)
    wi_ref[...] = jnp.where(lane < IDX_HEADS, pltpu.roll(t, LANES - IDX_DIM, 1) * w_scale, 0.0)

    q1_ref[...] = (norm_rope(seg("q1"), 4) * Q_SCALE).astype(BF16)
    q2_ref[...] = (norm_rope(seg("q2"), 5) * Q_SCALE).astype(BF16)
    k1_ref[...] = norm_rope(seg("k1"), 6).astype(BF16)
    k2_ref[...] = norm_rope(seg("k2"), 7).astype(BF16)
    for half in range(2):
        vc_ref[:, half * HW:(half + 1) * HW] = seg("vc", half * HW, HW).astype(BF16)
    for gi, name in enumerate(("ga", "gb", "gc")):
        for part in range(D_MODEL // 512):
            g = seg(name, part * 512, 512)
            lo = gi * D_MODEL + part * 512
            sg_ref[:, lo:lo + 512] = (1.0 / (1.0 + jnp.exp(-g))).astype(BF16)


def _project(x2, norm_g, w_pad, layer, cos_t, sin_t, gains, kgain, seq):
    n = x2.shape[0]
    tm = min(DENSE_TILE, seq)
    assert seq % tm == 0 and tm % MOBA_BLOCK == 0
    n_pos = seq // tm
    row = lambda w: pl.BlockSpec((tm, w), lambda i: (i, 0))
    const = lambda shape: pl.BlockSpec(shape, lambda i: (0,) * len(shape))
    out_widths = [HW] * 6 + [IDX_HEADS * IDX_DIM, IDX_DIM, LANES] + [HW] * 4 + [N_HEADS * C_VDIM, 3 * D_MODEL]
    out_dtypes = [BF16] * 8 + [F32] + [BF16] * 6
    out_shape = [jax.ShapeDtypeStruct((n, w), dt) for w, dt in zip(out_widths, out_dtypes)]
    out_shape.append(jax.ShapeDtypeStruct((n // MOBA_BLOCK, 1, HW), F32))
    out_specs = [row(w) for w in out_widths] + [pl.BlockSpec((tm // MOBA_BLOCK, 1, HW), lambda i: (i, 0, 0))]
    return pl.pallas_call(
        _proj_kernel,
        grid=(n // tm,),
        in_specs=[row(D_MODEL), const((1, D_MODEL)),
                  pl.BlockSpec((None, D_MODEL, D_IN_PAD), lambda i: (layer, 0, 0), pipeline_mode=pl.Buffered(1)),
                  pl.BlockSpec((tm, HW), lambda i: (i % n_pos, 0)),
                  pl.BlockSpec((tm, HW), lambda i: (i % n_pos, 0)),
                  const(gains.shape), const(kgain.shape)],
        out_specs=out_specs,
        out_shape=out_shape,
        compiler_params=_params(1),
        name="proj",
    )(x2, norm_g, w_pad, cos_t, sin_t, gains, kgain)


def _online_update(parts, ms, acc_ref):
    ps, out = [], []
    for c, tiles in enumerate(parts):
        m_new = ms[c]
        for s, _, ok in tiles:
            smax = jnp.max(s, axis=0, keepdims=True)
            m_new = jnp.maximum(m_new, smax if ok is None else jnp.where(ok, smax, NEG_BIG))
        m_eff = jnp.maximum(m_new, M_FLOOR)
        out.append(m_new)
        probs = [jnp.exp2(s - (m_eff if ok is None else jnp.where(ok, m_eff, -NEG_BIG))).astype(BF16)
                 for s, _, ok in tiles]
        ps.append((jnp.exp2(ms[c] - m_new), probs))
    for c, (alpha, probs) in enumerate(ps):
        acc = alpha * acc_ref[c]
        for (_, vt, _), p in zip(parts[c], probs):
            acc = acc + _dot(vt, p)
        acc_ref[c] = acc
    return out


def _init_max(n_chains, tq):
    return tuple(jnp.full((1, tq), NEG_BIG, F32) for _ in range(n_chains))


def _with_ones_row(vt):
    row = lax.broadcasted_iota(jnp.int32, (V_PAD, vt.shape[1]), 0)
    return jnp.concatenate([vt, jnp.where(row == 0, 1.0, 0.0).astype(vt.dtype)], axis=0)


def _normalized(acc_ref, c, dv):
    acc = acc_ref[c]
    return acc[:dv] / acc[dv:dv + 1]


def _head_slice(h, width=HEAD_DIM):
    return slice(h * width, (h + 1) * width)


MOBA_GROUP = 4


def _moba_kernel(qt_ref, k_ref, vt_ref, kmean_ref, o_ref, acc_ref, *, n_sel):
    i = pl.program_id(1)
    blk = MOBA_BLOCK
    nb = kmean_ref.shape[1]
    km = kmean_ref[0]
    brow = lax.broadcasted_iota(jnp.int32, (nb, blk), 0)
    browf = brow.astype(F32)
    causal = (lax.broadcasted_iota(jnp.int32, (blk, 1), 0)
              <= lax.broadcasted_iota(jnp.int32, (1, blk), 1))
    qts = [qt_ref[0, _head_slice(h), :] for h in range(N_HEADS)]
    sels = []
    for h in range(N_HEADS):
        km_hi, km_lo = _split_bf16(km[:, _head_slice(h)])
        gate = _dot(km_hi, qts[h]) + _dot(km_lo, qts[h])
        gate = jnp.where(brow < i, gate, -jnp.inf)
        sel = jnp.zeros((nb, blk), F32)
        for _ in range(n_sel):
            gm = jnp.max(gate, axis=0, keepdims=True)
            is_m = (gate == gm) & (gm > -jnp.inf)
            first = jnp.min(jnp.where(is_m, browf, float(nb)), axis=0, keepdims=True)
            pick = browf == first
            sel = jnp.where(pick, 1.0, sel)
            gate = jnp.where(pick, -jnp.inf, gate)
        sels.append(sel)
    acc_ref[...] = jnp.zeros(acc_ref.shape, F32)

    def tile(j, h, mask=None, seen=False):
        rows = pl.ds(pl.multiple_of(j * blk, blk), blk)
        s = _dot(k_ref[rows, _head_slice(h)], qts[h])
        if mask is not None:
            s = jnp.where(mask, s, NEG_BIG)
        ok = jnp.sum(jnp.where(brow == j, sels[h], 0.0), axis=0, keepdims=True) > 0.0 if seen else None
        return s, _with_ones_row(vt_ref[j, _head_slice(h), :]), ok

    def absorb(first, n_past, diag, ms):
        parts = [[tile(first + u, h, seen=True) for u in range(n_past)]
                 + ([tile(i, h, mask=causal)] if diag else []) for h in range(N_HEADS)]
        return tuple(_online_update(parts, ms, acc_ref))

    group = MOBA_GROUP
    ms = lax.fori_loop(0, i // group, lambda g, ms: absorb(g * group, group, False, ms),
                       _init_max(N_HEADS, blk))
    for rest in range(group):
        @pl.when(i % group == rest)
        def _():
            absorb(i - rest, rest, True, ms)

    for h in range(N_HEADS):
        o_ref[0, _head_slice(h), :] = _normalized(acc_ref, h, HEAD_DIM).astype(o_ref.dtype)


def _moba(qt, k, vt, kmean, batch, seq):
    blk = MOBA_BLOCK
    nb = seq // blk
    n_sel = min(MOBA_TOPK, nb - 1)
    tspec = pl.BlockSpec((1, HW, blk), lambda b, i: (b, 0, i))
    return pl.pallas_call(
        functools.partial(_moba_kernel, n_sel=n_sel),
        grid=(batch, nb),
        in_specs=[tspec,
                  pl.BlockSpec((seq, HW), lambda b, i: (b, 0)),
                  pl.BlockSpec((nb, vt.shape[1], blk), lambda b, i: (b, 0, 0)),
                  pl.BlockSpec((1, nb, HW), lambda b, i: (b, 0, 0))],
        out_specs=tspec,
        out_shape=jax.ShapeDtypeStruct((batch, HW, seq), BF16),
        scratch_shapes=[pltpu.VMEM((N_HEADS, HEAD_DIM + V_PAD, blk), F32)],
        compiler_params=_params(2),
        name="moba",
    )(qt, k, vt, kmean)


def _bit_planes(words):
    a = list(words)
    assert len(a) == 32
    mask, j = 0x0000FFFF, 16
    while j:
        k = 0
        while k < 32:
            t = (a[k] ^ (a[k + j] >> j)) & mask
            a[k] = a[k] ^ t
            a[k + j] = a[k + j] ^ (t << j)
            k = (k + j + 1) & ~j
        j >>= 1
        mask = (mask ^ (mask << j)) & 0xFFFFFFFF
    return a


DSA_GROUP = 4


def _dsa_kernel(qit_ref, wi_ref, ki_ref, qt_ref, k_ref, vt_ref, o_ref, plane_ref, sel_ref, acc_ref, *, n_keep):
    i = pl.program_id(1)
    blk = ROW_TILE
    n_chunk = i + 1
    n_planes, n_slots, sub = plane_ref.shape[:3]
    v_bits, s_bits = (n_planes - 1).bit_length(), (sub - 1).bit_length()
    idx_bits = v_bits + s_bits + (n_slots - 1).bit_length()

    @pl.when(i == 0)
    def _():
        plane_ref[:, 1:] = jnp.zeros((n_planes, n_slots - 1) + plane_ref.shape[2:], jnp.int32)

    w_t = wi_ref[...].T
    causal = (lax.broadcasted_iota(jnp.int32, (blk, 1), 0)
              <= lax.broadcasted_iota(jnp.int32, (1, blk), 1))

    def score_chunk(c, diag):
        kc = ki_ref[pl.ds(pl.multiple_of(c * blk, blk), blk), :]
        lgs = [_dot(kc, qit_ref[0, _head_slice(h, IDX_DIM), :]) for h in range(IDX_HEADS)]
        sc = jnp.zeros((blk, blk), F32)
        for h in range(IDX_HEADS):
            sc = sc + w_t[h:h + 1, :] * jnp.maximum(lgs[h], 0.0)
        sc = sc + 0.0
        bits = pltpu.bitcast(sc, jnp.int32)
        key = bits ^ ((bits >> 31) | INT_MIN)
        if diag:
            key = jnp.where(causal, key, 0)
        words = key.reshape(n_planes, sub, blk)
        planes = _bit_planes([words[v] for v in range(n_planes)])
        for b in range(n_planes):
            plane_ref[b, c] = planes[n_planes - 1 - b]

    def score_pair(c2, carry):
        score_chunk(2 * c2, False)
        score_chunk(2 * c2 + 1, False)
        return carry

    lax.fori_loop(0, i // 2, score_pair, 0)

    @pl.when(i % 2 == 1)
    def _():
        score_chunk(i - 1, False)
        score_chunk(i, True)

    @pl.when(i % 2 == 0)
    def _():
        score_chunk(i, True)

    keep = float(n_keep)

    def step(plane, eq, gt):
        cand = gt | (eq & plane)
        per_row = jnp.sum(lax.population_count(cand), axis=0)
        take = jnp.sum(per_row.astype(F32), axis=0, keepdims=True) >= keep
        return eq & jnp.where(take, plane, ~plane), jnp.where(take, gt, cand)

    def index_plane(t, shape):
        if t < s_bits:
            s = lax.broadcasted_iota(jnp.int32, shape, 1)
            return jnp.where(((s >> t) & 1) == 0, -1, 0)
        if t < s_bits + v_bits:
            word = sum(1 << j for j in range(n_planes) if (((n_planes - 1 - j) >> (t - s_bits)) & 1) == 0)
            return jnp.full(shape, word - (1 << 32) if word >= (1 << 31) else word, jnp.int32)
        slot = lax.broadcasted_iota(jnp.int32, shape, 0)
        return jnp.where(((slot >> (t - s_bits - v_bits)) & 1) == 0, -1, 0)

    def select(n_used, los):
        shape = (n_used, sub, LANES)

        def score_bit(t, carry):
            return tuple(step(plane_ref[n_planes - 1 - t, 0:n_used, :, lo:lo + LANES], *state)
                         for lo, state in zip(los, carry))

        init = (jnp.full(shape, -1, jnp.int32), jnp.zeros(shape, jnp.int32))
        states = list(lax.fori_loop(0, n_planes, score_bit, (init,) * len(los)))
        for t in reversed(range(idx_bits)):
            states = [step(index_plane(t, shape), *state) for state in states]
        slot = lax.broadcasted_iota(jnp.int32, shape, 0)
        s = lax.broadcasted_iota(jnp.int32, shape, 1)
        for lo, (eq, gt) in zip(los, states):
            qoff = lo + lax.broadcasted_iota(jnp.int32, shape, 2)
            reach = jnp.left_shift(-1, (n_planes - 1) - ((qoff - s) >> s_bits))
            valid = jnp.where(slot < i, -1, jnp.where((slot == i) & (qoff >= s), reach, 0))
            sel_ref[0:n_used, :, lo:lo + LANES] = (eq | gt) & valid
            if n_used < n_slots:
                sel_ref[n_used:, :, lo:lo + LANES] = jnp.zeros((n_slots - n_used, sub, LANES), jnp.int32)

    half_slots = n_slots // 2

    @pl.when(n_chunk <= half_slots)
    def _():
        select(half_slots, list(range(0, blk, LANES)))

    @pl.when(n_chunk > half_slots)
    def _():
        for lo in range(0, blk, LANES):
            select(n_slots, [lo])

    qts = [qt_ref[0, _head_slice(h), :] for h in range(N_HEADS)]
    acc_ref[...] = jnp.zeros(acc_ref.shape, F32)

    def absorb(first, count, ms):
        parts = [[] for _ in range(N_HEADS)]
        for u in range(count):
            c = first + u
            rows = pl.ds(pl.multiple_of(c * blk, blk), blk)
            w = sel_ref[c]
            allowed = jnp.concatenate([jnp.left_shift(w, v) for v in range(n_planes)], axis=0) < 0
            for h in range(N_HEADS):
                s = jnp.where(allowed, _dot(k_ref[rows, _head_slice(h)], qts[h]), NEG_BIG)
                parts[h].append((s, _with_ones_row(vt_ref[c, _head_slice(h), :]), None))
        return tuple(_online_update(parts, ms, acc_ref))

    group = DSA_GROUP
    ms = lax.fori_loop(0, n_chunk // group, lambda g, ms: absorb(g * group, group, ms),
                       _init_max(N_HEADS, blk))
    for rest in range(1, group):
        @pl.when(n_chunk % group == rest)
        def _():
            absorb(n_chunk - rest, rest, ms)

    for h in range(N_HEADS):
        o_ref[0, _head_slice(h), :] = _normalized(acc_ref, h, HEAD_DIM).astype(o_ref.dtype)


def _dsa(qit, wi, ki, qt, k, vt, batch, seq):
    blk = ROW_TILE
    nq = seq // blk
    n_keep = min(DSA_TOPK_MAX, seq // 4)
    n_planes = 32
    tspec = lambda w: pl.BlockSpec((1, w, blk), lambda b, i: (b, 0, i))
    full = lambda w: pl.BlockSpec((seq, w), lambda b, i: (b, 0))
    return pl.pallas_call(
        functools.partial(_dsa_kernel, n_keep=n_keep),
        grid=(batch, nq),
        in_specs=[tspec(IDX_HEADS * IDX_DIM), pl.BlockSpec((blk, LANES), lambda b, i: (b * nq + i, 0)),
                  full(IDX_DIM), tspec(HW), full(HW),
                  pl.BlockSpec((nq, vt.shape[1], blk), lambda b, i: (b, 0, 0))],
        out_specs=tspec(HW),
        out_shape=jax.ShapeDtypeStruct((batch, HW, seq), BF16),
        scratch_shapes=[pltpu.VMEM((n_planes, nq, blk // n_planes, blk), jnp.int32),
                        pltpu.VMEM((nq, blk // n_planes, blk), jnp.int32),
                        pltpu.VMEM((N_HEADS, HEAD_DIM + V_PAD, blk), F32)],
        compiler_params=_params(2),
        name="dsa",
    )(qit, wi, ki, qt, k, vt)


DIFF_GROUP = 4
DIFF_KEY_GROUP = 2


def _diff_kernel(q1t_ref, q2t_ref, k1_ref, k2_ref, vt_ref, dl_ref, gain_ref, o_ref, acc_ref, *, lam_init):
    i = pl.program_id(1)
    blk = ROW_TILE
    dl = dl_ref[...]
    lam = (jnp.exp(jnp.sum(dl[0:1] * dl[1:2], axis=-1, keepdims=True))
           - jnp.exp(jnp.sum(dl[2:3] * dl[3:4], axis=-1, keepdims=True)) + lam_init)
    causal = (lax.broadcasted_iota(jnp.int32, (blk, 1), 0)
              <= lax.broadcasted_iota(jnp.int32, (1, blk), 1))
    maps = ((q1t_ref, k1_ref), (q2t_ref, k2_ref))

    for h0 in range(0, N_HEADS, DIFF_GROUP):
        chains = [(h, mp) for h in range(h0, h0 + DIFF_GROUP) for mp in range(2)]
        qts = [maps[mp][0][0, _head_slice(h), :] for h, mp in chains]
        acc_ref[...] = jnp.zeros(acc_ref.shape, F32)

        def tile(j, c, mask=None):
            h, mp = chains[c]
            rows = pl.ds(pl.multiple_of(j * blk, blk), blk)
            s = _dot(maps[mp][1][rows, _head_slice(h)], qts[c])
            if mask is not None:
                s = jnp.where(mask, s, NEG_BIG)
            return s, _with_ones_row(vt_ref[j, _head_slice(h, C_VDIM), :]), None

        def absorb(first, n_past, diag, ms):
            parts = [[tile(first + u, c) for u in range(n_past)] + ([tile(i, c, causal)] if diag else [])
                     for c in range(len(chains))]
            return tuple(_online_update(parts, ms, acc_ref))

        group = DIFF_KEY_GROUP
        ms = lax.fori_loop(0, i // group, lambda g, ms: absorb(g * group, group, False, ms),
                           _init_max(len(chains), blk))
        for rest in range(group):
            @pl.when(i % group == rest)
            def _():
                absorb(i - rest, rest, True, ms)

        for g in range(DIFF_GROUP):
            h = h0 + g
            o = _normalized(acc_ref, 2 * g, C_VDIM) - lam * _normalized(acc_ref, 2 * g + 1, C_VDIM)
            ms = jnp.mean(o * o, axis=0, keepdims=True)
            o = o * lax.rsqrt(ms + EPS) * gain_ref[...] * (1.0 - lam_init)
            o_ref[0, _head_slice(h, C_VDIM), :] = o.astype(o_ref.dtype)


def _diff(q1t, q2t, k1, k2, vt, dl, gain, lam_init, batch, seq):
    blk = ROW_TILE
    nq = seq // blk
    tspec = lambda w: pl.BlockSpec((1, w, blk), lambda b, i: (b, 0, i))
    kspec = pl.BlockSpec((seq, HW), lambda b, i: (b, 0))
    vw = N_HEADS * C_VDIM
    return pl.pallas_call(
        functools.partial(_diff_kernel, lam_init=lam_init),
        grid=(batch, nq),
        in_specs=[tspec(HW), tspec(HW), kspec, kspec,
                  pl.BlockSpec((nq, vt.shape[1], blk), lambda b, i: (b, 0, 0)),
                  pl.BlockSpec((4, HEAD_DIM), lambda b, i: (0, 0)),
                  pl.BlockSpec((C_VDIM, 1), lambda b, i: (0, 0))],
        out_specs=tspec(vw),
        out_shape=jax.ShapeDtypeStruct((batch, vw, seq), BF16),
        scratch_shapes=[pltpu.VMEM((2 * DIFF_GROUP, C_VDIM + V_PAD, blk), F32)],
        compiler_params=_params(2),
        name="diff",
    )(q1t, q2t, k1, k2, vt, dl, gain)


def _merge_kernel(x_ref, oa_ref, ob_ref, oc_ref, sg_ref, wa_ref, wb_ref, wc_ref, wo_ref, g_ref,
                  wr_hi_ref, wr_lo_ref, br_ref, x1_ref, hn_ref, route_ref):
    merged = (sg_ref[:, 0:D_MODEL].astype(F32) * _dot(oa_ref[...], wa_ref[...])
              + sg_ref[:, D_MODEL:2 * D_MODEL].astype(F32) * _dot(ob_ref[...], wb_ref[...])
              + sg_ref[:, 2 * D_MODEL:3 * D_MODEL].astype(F32) * _dot(oc_ref[...], wc_ref[...]))
    x1 = x_ref[...] + _dot(merged.astype(BF16), wo_ref[...])
    x1_ref[...] = x1
    ms = jnp.mean(x1 * x1, axis=-1, keepdims=True)
    hn = x1 * lax.rsqrt(ms + EPS) * g_ref[...]
    hn_ref[...] = hn

    hi, lo = _split_bf16(hn)
    lg = (_dot(hi, wr_hi_ref[...]) + _dot(lo, wr_hi_ref[...]) + _dot(hi, wr_lo_ref[...])
          + br_ref[...])
    lane = lax.broadcasted_iota(jnp.int32, lg.shape, 1)
    lanef = lane.astype(F32)
    far = float(LANES)
    is_g = (lane >= N_EXPERTS) & (lane < N_EXPERTS + N_GROUPS)
    gl = jnp.where(is_g, lg, -jnp.inf)
    gmax = jnp.max(gl, axis=-1, keepdims=True)
    gidx = jnp.min(jnp.where(gl == gmax, lanef, far), axis=-1, keepdims=True) - float(N_EXPERTS)
    g_w = 1.0 / jnp.sum(jnp.where(is_g, jnp.exp(gl - gmax), 0.0), axis=-1, keepdims=True)
    in_group = (lane < N_EXPERTS) & ((lane // EXPERTS_PER_GROUP).astype(F32) == gidx)
    el = jnp.where(in_group, lg, -jnp.inf)
    e1 = jnp.max(el, axis=-1, keepdims=True)
    i1 = jnp.min(jnp.where(el == e1, lanef, far), axis=-1, keepdims=True)
    el2 = jnp.where(lanef == i1, -jnp.inf, el)
    e2 = jnp.max(el2, axis=-1, keepdims=True)
    i2 = jnp.min(jnp.where(el2 == e2, lanef, far), axis=-1, keepdims=True)
    t = jnp.exp(e2 - e1)
    w1 = g_w / (1.0 + t)
    w2 = g_w * t / (1.0 + t)
    route_ref[...] = jnp.where(lane == 0, i1, jnp.where(lane == 1, i2, jnp.where(lane == 2, w1, jnp.where(lane == 3, w2, 0.0))))


def _merge(x2, oa, ob, oc, sg, wa, wb, wc, wo, norm_g, wr_hi, wr_lo, br):
    n = x2.shape[0]
    tm = min(DENSE_TILE, n)
    row = lambda w: pl.BlockSpec((tm, w), lambda i: (i, 0))
    const = lambda a: pl.BlockSpec(a.shape, lambda i: (0, 0))
    return pl.pallas_call(
        _merge_kernel,
        grid=(n // tm,),
        in_specs=[row(D_MODEL), row(HW), row(HW), row(N_HEADS * C_VDIM), row(3 * D_MODEL),
                  const(wa), const(wb), const(wc), const(wo), const(norm_g), const(wr_hi),
                  const(wr_lo), const(br)],
        out_specs=[row(D_MODEL), row(D_MODEL), row(LANES)],
        out_shape=[jax.ShapeDtypeStruct((n, D_MODEL), F32), jax.ShapeDtypeStruct((n, D_MODEL), F32),
                   jax.ShapeDtypeStruct((n, LANES), F32)],
        compiler_params=_params(1),
        name="merge",
    )(x2, oa, ob, oc, sg, wa, wb, wc, wo, norm_g, wr_hi, wr_lo, br)


MOE_BLOCK = 512
META_ROWS = SUBLANES
META_USED = 3 * LANES
META_END = 4 * LANES
META_PADDED = 5 * LANES


def _lane_prefix_sum(x):
    lane = lax.broadcasted_iota(jnp.int32, x.shape, 1)
    shift = 1
    while shift < LANES:
        x = x + jnp.where(lane >= shift, pltpu.roll(x, shift, 1), 0.0)
        shift *= 2
    return x


def _positions_kernel(route_ref, dest_ref, meta_ref, cnt_ref, base_ref):
    phase = pl.program_id(0)
    t = pl.program_id(1)
    tm = route_ref.shape[0]
    route = route_ref[...]
    lane = lax.broadcasted_iota(jnp.int32, route.shape, 1)
    lanef = lane.astype(F32)
    e1 = route[:, 0:1]
    e2 = route[:, 1:2]
    uses = jnp.where((lanef == e1) | (lanef == e2), 1.0, 0.0)
    tile_cnt = jnp.sum(uses, axis=0, keepdims=True)

    @pl.when((phase == 0) & (t == 0))
    def _():
        cnt_ref[...] = jnp.zeros(cnt_ref.shape, F32)

    @pl.when(phase == 0)
    def _():
        cnt_ref[...] += tile_cnt

    @pl.when((phase == 1) & (t == 0))
    def _():
        cnt = cnt_ref[...]
        padded = jnp.floor((cnt + (MOE_BLOCK - 1)) * (1.0 / MOE_BLOCK)) * MOE_BLOCK
        end = _lane_prefix_sum(padded)
        base_ref[...] = end - padded
        cnt_ref[...] = jnp.zeros(cnt_ref.shape, F32)
        lane1 = lax.broadcasted_iota(jnp.int32, (1, LANES), 1)
        row = lax.broadcasted_iota(jnp.int32, (META_ROWS, LANES), 0)
        col = lax.broadcasted_iota(jnp.int32, (META_ROWS, LANES), 1)
        first_row = ((row * LANES + col) * MOE_BLOCK).astype(F32)
        owner = jnp.zeros((META_ROWS, LANES), F32)
        for e in range(N_EXPERTS):
            end_e = jnp.sum(jnp.where(lane1 == e, end, 0.0), axis=-1, keepdims=True)
            owner = owner + jnp.where(end_e <= first_row, 1.0, 0.0)
        owner = jnp.minimum(owner, float(N_EXPERTS - 1))
        used = jnp.sum(jnp.where(lane1 == N_EXPERTS - 1, end, 0.0), axis=-1, keepdims=True) * (1.0 / MOE_BLOCK)
        meta = jnp.where(row == META_USED // LANES, used,
                         jnp.where(row == META_END // LANES, end,
                                   jnp.where(row == META_PADDED // LANES, padded, owner)))
        meta_ref[...] = meta.astype(jnp.int32)

    @pl.when(phase == 1)
    def _():
        before = (lax.broadcasted_iota(jnp.int32, (tm, tm), 1)
                  < lax.broadcasted_iota(jnp.int32, (tm, tm), 0)).astype(BF16)
        rank = _dot(before, uses.astype(BF16))
        pos = base_ref[...] + cnt_ref[...] + rank
        d1 = jnp.sum(jnp.where(lanef == e1, pos, 0.0), axis=-1, keepdims=True)
        d2 = jnp.sum(jnp.where(lanef == e2, pos, 0.0), axis=-1, keepdims=True)
        dest_ref[...] = jnp.where(lane == 0, d1, jnp.where(lane == 1, d2, 0.0)).astype(jnp.int32)
        cnt_ref[...] += tile_cnt


def _positions(route):
    n = route.shape[0]
    tm = min(4 * ROW_TILE, n)
    return pl.pallas_call(
        _positions_kernel,
        grid=(2, n // tm),
        in_specs=[pl.BlockSpec((tm, LANES), lambda p, t: (t, 0))],
        out_specs=[pl.BlockSpec((tm, LANES), lambda p, t: (t * p, 0)),
                   pl.BlockSpec((META_ROWS, LANES), lambda p, t: (0, 0))],
        out_shape=[jax.ShapeDtypeStruct((n, LANES), jnp.int32),
                   jax.ShapeDtypeStruct((META_ROWS, LANES), jnp.int32)],
        scratch_shapes=[pltpu.VMEM((1, LANES), F32), pltpu.VMEM((1, LANES), F32)],
        compiler_params=_params(2),
        name="moe_positions",
    )(route)


def _row_copy(src_ref, src_row, dst_ref, dst_row, sem):
    return pltpu.make_async_copy(src_ref.at[pl.ds(src_row, 1), :], dst_ref.at[pl.ds(dst_row, 1), :], sem)


def _dispatch_kernel(meta_ref, dest_ref, hn_ref, xs_ref, zero_ref, stage_ref, sem, in_sems, row_sems):
    t = pl.program_id(0)
    last = pl.num_programs(0) - 1
    tm = stage_ref.shape[1]

    def fetch(tile):
        return pltpu.make_async_copy(hn_ref.at[pl.ds(pl.multiple_of(tile * tm, tm), tm), :],
                                     stage_ref.at[tile % 3], in_sems.at[tile % 2])

    @pl.when(t == 0)
    def _():
        fetch(0).start()

    @pl.when(t < last)
    def _():
        fetch(t + 1).start()

    @pl.when(t == 0)
    def _():
        zero_ref[...] = jnp.zeros(zero_ref.shape, F32)

        def fill(e):
            end = pl.multiple_of(meta_ref[META_END + e], MOE_BLOCK)
            return pltpu.make_async_copy(zero_ref, xs_ref.at[pl.ds(end - MOE_BLOCK, MOE_BLOCK), :], sem)

        for e in range(N_EXPERTS):
            @pl.when(meta_ref[META_PADDED + e] > 0)
            def _():
                fill(e).start()
        for e in range(N_EXPERTS):
            @pl.when(meta_ref[META_PADDED + e] > 0)
            def _():
                fill(e).wait()

        def spare(b):
            return pltpu.make_async_copy(
                zero_ref, xs_ref.at[pl.ds(pl.multiple_of(b * MOE_BLOCK, MOE_BLOCK), MOE_BLOCK), :], sem)

        n_blocks = xs_ref.shape[0] // MOE_BLOCK
        lax.fori_loop(meta_ref[META_USED], n_blocks, lambda b, c: (spare(b).start(), c)[1], 0)
        lax.fori_loop(meta_ref[META_USED], n_blocks, lambda b, c: (spare(b).wait(), c)[1], 0)

    fetch(t).wait()
    rows_ref = stage_ref.at[t % 3]

    def start(r, carry):
        for k in range(2):
            _row_copy(rows_ref, r, xs_ref, dest_ref[0, 0, 2 * r + k], row_sems.at[t % 2]).start()
        return carry

    lax.fori_loop(0, tm, start, 0, unroll=8)

    def wait_rows(tile):
        for _ in range(2):
            pltpu.make_async_copy(stage_ref.at[tile % 3], xs_ref.at[pl.ds(0, tm), :], row_sems.at[tile % 2]).wait()

    @pl.when(t > 0)
    def _():
        wait_rows(t - 1)

    @pl.when(t == last)
    def _():
        wait_rows(t)


def _dispatch(meta, dest3, hn, n_rows):
    n = hn.shape[0]
    tm = dest3.shape[2] // 2
    return pl.pallas_call(
        _dispatch_kernel,
        grid_spec=pltpu.PrefetchScalarGridSpec(
            num_scalar_prefetch=1,
            grid=(n // tm,),
            in_specs=[pl.BlockSpec((1, 1, 2 * tm), lambda t, m: (t, 0, 0), memory_space=pltpu.SMEM),
                      pl.BlockSpec(memory_space=pl.ANY)],
            out_specs=pl.BlockSpec(memory_space=pl.ANY),
            scratch_shapes=[pltpu.VMEM((MOE_BLOCK, D_MODEL), F32), pltpu.VMEM((3, tm, D_MODEL), F32),
                            pltpu.SemaphoreType.DMA(()), pltpu.SemaphoreType.DMA((2,)),
                            pltpu.SemaphoreType.DMA((2,))]),
        out_shape=jax.ShapeDtypeStruct((n_rows, D_MODEL), F32),
        compiler_params=_params(1),
        name="moe_dispatch",
    )(meta, dest3, hn)


def _expert_kernel(meta_ref, xs_ref, wg_ref, wu_ref, wd_ref, y_ref, wg_bf, wu_bf, wd_bf):
    b = pl.program_id(0)
    holds_rows = b < meta_ref[META_USED]
    new_expert = (b == 0) | (meta_ref[b] != meta_ref[jnp.maximum(b - 1, 0)])

    @pl.when(holds_rows & new_expert)
    def _():
        wg_bf[...] = wg_ref[0].astype(BF16)
        wu_bf[...] = wu_ref[0].astype(BF16)
        wd_bf[...] = wd_ref[0].astype(BF16)

    @pl.when(holds_rows)
    def _():
        x = xs_ref[...].astype(BF16)
        g = _dot(x, wg_bf[...])
        u = _dot(x, wu_bf[...])
        hid = g * (1.0 / (1.0 + jnp.exp(-g))) * u
        y_ref[...] = _dot(hid.astype(BF16), wd_bf[...])

    @pl.when(jnp.logical_not(holds_rows))
    def _():
        y_ref[...] = jnp.zeros(y_ref.shape, F32)


def _experts(meta, xs, wg, wu, wd, layer):
    n_blocks = xs.shape[0] // MOE_BLOCK
    rows = pl.BlockSpec((MOE_BLOCK, D_MODEL), lambda b, m: (b, 0))
    weight = lambda shape: pl.BlockSpec(
        (None, 1) + shape, lambda b, m: (layer, m[jnp.minimum(b, m[META_USED] - 1)], 0, 0))
    return pl.pallas_call(
        _expert_kernel,
        grid_spec=pltpu.PrefetchScalarGridSpec(
            num_scalar_prefetch=1,
            grid=(n_blocks,),
            in_specs=[rows, weight((D_MODEL, D_EXPERT)), weight((D_MODEL, D_EXPERT)),
                      weight((D_EXPERT, D_MODEL))],
            out_specs=rows,
            scratch_shapes=[pltpu.VMEM((D_MODEL, D_EXPERT), BF16), pltpu.VMEM((D_MODEL, D_EXPERT), BF16),
                            pltpu.VMEM((D_EXPERT, D_MODEL), BF16)]),
        out_shape=jax.ShapeDtypeStruct(xs.shape, F32),
        compiler_params=_params(1),
        name="moe_experts",
    )(meta, xs, wg, wu, wd)


def _combine_kernel(dest_ref, next_dest_ref, x1_ref, route_ref, y_ref, o_ref, buf_ref, sems):
    t = pl.program_id(0)
    tm = x1_ref.shape[0]

    def gather(d_ref, parity):
        def start(g, carry):
            first = pl.multiple_of(g * SUBLANES, SUBLANES)
            for u in range(SUBLANES):
                for k in range(2):
                    _row_copy(y_ref, d_ref[0, 0, 2 * (g * SUBLANES + u) + k], buf_ref.at[parity, k], first + u,
                              sems.at[parity]).start()
            return carry
        lax.fori_loop(0, tm // SUBLANES, start, 0)

    @pl.when(t == 0)
    def _():
        gather(dest_ref, 0)

    @pl.when(t + 1 < pl.num_programs(0))
    def _():
        gather(next_dest_ref, (t + 1) % 2)

    for k in range(2):
        pltpu.make_async_copy(y_ref.at[pl.ds(0, tm), :], buf_ref.at[t % 2, k], sems.at[t % 2]).wait()
    route = route_ref[...]
    o_ref[...] = x1_ref[...] + route[:, 2:3] * buf_ref[t % 2, 0] + route[:, 3:4] * buf_ref[t % 2, 1]


def _combine(dest3, x1, route, y):
    n = x1.shape[0]
    tm = dest3.shape[2] // 2
    row = lambda w: pl.BlockSpec((tm, w), lambda t: (t, 0))
    n_tiles = n // tm
    slots = lambda index: pl.BlockSpec((1, 1, 2 * tm), index, memory_space=pltpu.SMEM)
    return pl.pallas_call(
        _combine_kernel,
        grid=(n_tiles,),
        in_specs=[slots(lambda t: (t, 0, 0)), slots(lambda t: (jnp.minimum(t + 1, n_tiles - 1), 0, 0)),
                  row(D_MODEL), row(LANES), pl.BlockSpec(memory_space=pl.ANY)],
        out_specs=row(D_MODEL),
        out_shape=jax.ShapeDtypeStruct((n, D_MODEL), F32),
        scratch_shapes=[pltpu.VMEM((2, 2, tm, D_MODEL), F32), pltpu.SemaphoreType.DMA((2,))],
        compiler_params=_params(1),
        name="moe_combine",
    )(dest3, dest3, x1, route, y)


def _moe(x1, hn, route, wg, wu, wd, layer):
    n = x1.shape[0]
    tm = min(2 * DENSE_TILE, n)
    n_blocks = -(-(2 * n + N_EXPERTS * (MOE_BLOCK - 1)) // MOE_BLOCK)
    assert n_blocks <= META_USED
    dest, meta = _positions(route)
    meta = meta.reshape(-1)
    dest3 = dest[:, :2].reshape(n // tm, 1, 2 * tm)
    xs = _dispatch(meta, dest3, hn, n_blocks * MOE_BLOCK)
    y = _experts(meta, xs, wg, wu, wd, layer)
    return _combine(dest3, x1, route, y)


def _rope_tables(seq):
    inv_freq = 1.0 / (ROPE_THETA ** (jnp.arange(0, HEAD_DIM, 2, dtype=F32) / HEAD_DIM))
    ang = jnp.arange(seq, dtype=F32)[:, None] * inv_freq[None, :]
    cos, sin = jnp.cos(ang), jnp.sin(ang)
    cos_t = jnp.tile(jnp.concatenate([cos, cos], axis=-1), (1, N_HEADS))
    sin_t = jnp.tile(jnp.concatenate([-sin, sin], axis=-1), (1, N_HEADS))
    return cos_t, sin_t


def _to_t(v, batch, n_chunk):
    feat = v.shape[1]
    return (v.reshape(batch, n_chunk, ROW_TILE, feat).transpose(0, 1, 3, 2)
            .reshape(batch * n_chunk, feat, ROW_TILE))


def _from_t(o_t):
    b, feat, seq = o_t.shape
    return o_t.transpose(0, 2, 1).reshape(b * seq, feat)


def kernel(x, norm_attn, w_in, qk_gain, idx_k_gain, diff_lambda, diff_subln_gain, w_proj_a, w_proj_b, w_proj_c, w_out, norm_ffn, w_group, b_group, w_router, b_router, w_e_gate, w_e_up, w_e_down):
    batch, seq, d = x.shape
    assert d == D_MODEL and seq % (2 * ROW_TILE) == 0 and ROW_TILE == MOBA_BLOCK
    n = batch * seq
    nq = seq // ROW_TILE
    depth = w_in.shape[0]
    cos_t, sin_t = _rope_tables(seq)
    x2 = x.reshape(n, d)
    w_pad = jnp.concatenate(
        [w_in[:, :, :KW_SRC], jnp.zeros((depth, d, LANES - IDX_DIM - IDX_HEADS), F32), w_in[:, :, KW_SRC:]],
        axis=2).astype(BF16)
    for l in range(depth):
        gains = jnp.tile(qk_gain[l][jnp.array([0, 1, 2, 3, 4, 4, 5, 5])], (1, N_HEADS))
        kgain = jnp.pad(idx_k_gain[l], (0, LANES - IDX_DIM))[None, :]
        (qa, ka, va, qb, kb, vb, qi, ki, wi, q1, q2, k1, k2, vc, sg, kmean) = _project(
            x2, norm_attn[l][None, :], w_pad, l, cos_t, sin_t, gains, kgain, seq)

        feat_major = lambda t: t.reshape(batch, seq, t.shape[1]).transpose(0, 2, 1)
        o_a = _from_t(_moba(feat_major(qa), ka, _to_t(va, batch, nq), kmean.reshape(batch, nq, HW),
                            batch, seq))
        o_b = _from_t(_dsa(feat_major(qi), wi, ki, feat_major(qb), kb, _to_t(vb, batch, nq), batch, seq))
        lam_init = 0.8 - 0.6 * math.exp(-0.3 * l)
        o_c = _from_t(_diff(feat_major(q1), feat_major(q2), k1, k2, _to_t(vc, batch, nq), diff_lambda[l],
                            diff_subln_gain[l][:, None], lam_init, batch, seq))

        w_r = jnp.concatenate([w_router[l], w_group[l],
                               jnp.zeros((d, LANES - N_EXPERTS - N_GROUPS), F32)], axis=1)
        wr_hi = w_r.astype(BF16)
        wr_lo = (w_r - wr_hi.astype(F32)).astype(BF16)
        b_r = jnp.concatenate([b_router[l], b_group[l],
                               jnp.zeros((LANES - N_EXPERTS - N_GROUPS,), F32)])[None, :]
        x1, hn, route = _merge(x2, o_a, o_b, o_c, sg, w_proj_a[l].astype(BF16), w_proj_b[l].astype(BF16),
                            w_proj_c[l].astype(BF16), w_out[l].astype(BF16), norm_ffn[l][None, :],
                            wr_hi, wr_lo, b_r)
        x2 = _moe(x1, hn, route, w_e_gate, w_e_up, w_e_down, l)
    return x2.reshape(batch, seq, d)
```
